```python
import jax, jax.numpy as jnp
from jax import lax
import numpy as np

D_MODEL = 4096
BATCH = 2
SEQ = 4096
DEPTH = 2

CHUNK = 64
N_EVEN = (DEPTH + 1) // 2
N_ODD = DEPTH // 2

POOL_WIDTH = D_MODEL // 2
POOL_WINDOWS = (2, 4, 8, 16)
N_POOL_GROUPS = len(POOL_WINDOWS)
POOL_GROUP_DIM = POOL_WIDTH // N_POOL_GROUPS
HEAD_DIM = 128
FOX_HEADS = (D_MODEL // 2) // HEAD_DIM
FOX_WIDTH = FOX_HEADS * HEAD_DIM
Q_BLOCK = 128
MIX_WIDTH = POOL_WIDTH + FOX_WIDTH
EVEN_IN_COLS = POOL_WIDTH + 3 * FOX_WIDTH + FOX_HEADS
LRU_WIDTH = D_MODEL
LRU_BLOCKS = 16
LRU_BLOCK_DIM = LRU_WIDTH // LRU_BLOCKS
CONV_WIDTH = 4
LRU_C = 8.0
N_GROUPS = 4
EXPERTS_PER_GROUP = 8
N_EXPERTS = N_GROUPS * EXPERTS_PER_GROUP
EXPERT_FF = 512
TOP_K = 2
DN_ALPHA = (2.0 * DEPTH) ** 0.25
DN_BETA = (8.0 * DEPTH) ** -0.25
LN_EPS = 1e-5

kernel_name = "hybrid_pool_fox_rglru_hmoe_deepnorm"


def layer_norm(x, g, b):
    xf = x.astype(jnp.float32)
    mu = jnp.mean(xf, axis=-1, keepdims=True)
    var = jnp.mean(jnp.square(xf - mu), axis=-1, keepdims=True)
    return ((xf - mu) * lax.rsqrt(var + LN_EPS) * g + b).astype(x.dtype)


def pool_mixer(u, w_pool, pool_scale):
    B, S, _ = u.shape
    uf = u.astype(jnp.float32).reshape(B, S, N_POOL_GROUPS, POOL_GROUP_DIM)
    cs = jnp.cumsum(uf, axis=1)
    t = jnp.arange(S)
    outs = []
    for g, w in enumerate(POOL_WINDOWS):
        cs_g = cs[:, :, g]
        lag = jnp.pad(cs_g, ((0, 0), (w, 0), (0, 0)))[:, :S]
        cnt = jnp.minimum(t + 1, w).astype(jnp.float32)[None, :, None]
        outs.append((cs_g - lag) / cnt - uf[:, :, g])
    pooled = jnp.stack(outs, axis=2).astype(u.dtype)
    mixed = jnp.einsum('bsgc,gcd->bsgd', pooled, w_pool)
    return mixed.reshape(B, S, POOL_WIDTH) * pool_scale


def forgetting_attention(q, k, v, f_logit, b_f):
    B, S, H, Dh = q.shape
    log_f = jax.nn.log_sigmoid(f_logit.astype(jnp.float32) + b_f.astype(jnp.float32))
    c = jnp.cumsum(log_f, axis=1).transpose(0, 2, 1)
    qh = q.transpose(0, 2, 1, 3)
    kh = k.transpose(0, 2, 1, 3)
    vh = v.transpose(0, 2, 1, 3)
    nb = S // Q_BLOCK
    q_blocks = qh.reshape(B, H, nb, Q_BLOCK, Dh).transpose(2, 0, 1, 3, 4)
    c_blocks = c.reshape(B, H, nb, Q_BLOCK).transpose(2, 0, 1, 3)
    k_pos = jnp.arange(S)
    scale = HEAD_DIM ** -0.5

    def block(args):
        qb, cq, idx = args
        s = jnp.einsum('bhqd,bhkd->bhqk', qb, kh).astype(jnp.float32) * scale
        s = s + cq[..., None] - c[:, :, None, :]
        q_pos = idx * Q_BLOCK + jnp.arange(Q_BLOCK)
        s = jnp.where(k_pos[None, :] <= q_pos[:, None], s, -jnp.inf)
        p = jax.nn.softmax(s, axis=-1)
        return jnp.einsum('bhqk,bhkd->bhqd', p.astype(vh.dtype), vh)

    out = lax.map(block, (q_blocks, c_blocks, jnp.arange(nb)))
    return out.transpose(1, 0, 3, 2, 4).reshape(B, S, H * Dh)


def even_mixer(x, w_in, w_pool, pool_scale, b_f, w_out):
    B, S, _ = x.shape
    proj = jnp.einsum('bsd,dc->bsc', x, w_in)
    u_pool, q, k, v, f = jnp.split(
        proj, [POOL_WIDTH, POOL_WIDTH + FOX_WIDTH, POOL_WIDTH + 2 * FOX_WIDTH,
               POOL_WIDTH + 3 * FOX_WIDTH], axis=-1)
    a_out = pool_mixer(u_pool, w_pool, pool_scale)
    hs = (B, S, FOX_HEADS, HEAD_DIM)
    b_out = forgetting_attention(q.reshape(hs), k.reshape(hs), v.reshape(hs), f, b_f)
    mixed = jnp.concatenate([a_out, b_out], axis=-1)
    return jnp.einsum('bsc,cd->bsd', mixed, w_out)


def rglru_mixer(x, w_in, conv_w, conv_b, w_a, b_a, w_x, b_x, lam, w_out):
    B, S, _ = x.shape
    proj = jnp.einsum('bsd,dc->bsc', x, w_in)
    gate, xb = jnp.split(proj, [LRU_WIDTH], axis=-1)
    gate = jax.nn.gelu(gate)
    xp = jnp.pad(xb, ((0, 0), (CONV_WIDTH - 1, 0), (0, 0)))
    xc = conv_b + sum(xp[:, i:i + S] * conv_w[i] for i in range(CONV_WIDTH))
    xh = xc.reshape(B, S, LRU_BLOCKS, LRU_BLOCK_DIM)
    r = jax.nn.sigmoid(jnp.einsum('bshi,hij->bshj', xh, w_a) + b_a)
    ig = jax.nn.sigmoid(jnp.einsum('bshi,hij->bshj', xh, w_x) + b_x)
    log_a_base = -jax.nn.softplus(-lam.astype(jnp.float32)).reshape(LRU_BLOCKS, LRU_BLOCK_DIM)
    log_a = LRU_C * r.astype(jnp.float32) * log_a_base
    a = jnp.exp(log_a)
    mult = jnp.sqrt(-jnp.expm1(2.0 * log_a))
    bterm = mult * ig.astype(jnp.float32) * xh.astype(jnp.float32)

    def combine(lhs, rhs):
        a1, b1 = lhs
        a2, b2 = rhs
        return a1 * a2, a2 * b1 + b2

    _, h = lax.associative_scan(combine, (a, bterm), axis=1)
    y = h.reshape(B, S, LRU_WIDTH).astype(x.dtype) * gate
    return jnp.einsum('bsc,cd->bsd', y, w_out)


def hierarchical_moe(x, w_group, b_group, w_expert, b_expert, w1, w3, w2):
    B, S, D = x.shape
    xt = x.reshape(B * S, D)
    g_logits = (xt @ w_group + b_group).astype(jnp.float32)
    g_prob = jax.nn.softmax(g_logits, axis=-1)
    g_idx = jnp.argmax(g_logits, axis=-1)
    g_w = jnp.take_along_axis(g_prob, g_idx[:, None], axis=-1)
    e_logits = (xt @ w_expert + b_expert).astype(jnp.float32).reshape(-1, N_GROUPS, EXPERTS_PER_GROUP)
    e_sel = jnp.take_along_axis(e_logits, g_idx[:, None, None], axis=1)[:, 0]
    top_v, top_i = lax.top_k(e_sel, TOP_K)
    weights = jax.nn.softmax(top_v, axis=-1) * g_w
    expert_id = g_idx[:, None] * EXPERTS_PER_GROUP + top_i
    gates = jnp.sum(jax.nn.one_hot(expert_id, N_EXPERTS, dtype=jnp.float32) * weights[..., None], axis=1)
    h = jax.nn.silu(jnp.einsum('td,edf->tef', xt, w1)) * jnp.einsum('td,edf->tef', xt, w3)
    h = h * gates[..., None].astype(h.dtype)
    y = jnp.einsum('tef,efd->td', h, w2)
    return y.reshape(B, S, D)


def setup_inputs(seed: int = 0) -> dict:
    key = jax.random.key(seed)
    ks = jax.random.split(key, 32)

    def nrm(k, shape, scale):
        return jax.random.normal(k, shape, jnp.float32) * scale

    a0 = jax.random.uniform(ks[13], (N_ODD, LRU_WIDTH), jnp.float32, 0.9, 0.999)
    p = a0 ** (1.0 / LRU_C)
    lam = jnp.log(p) - jnp.log1p(-p)
    return {
        "x": nrm(ks[0], (BATCH, SEQ, D_MODEL), 1.0),
        "even_w_in": nrm(ks[1], (N_EVEN, D_MODEL, EVEN_IN_COLS), D_MODEL ** -0.5),
        "even_w_pool": nrm(ks[2], (N_EVEN, N_POOL_GROUPS, POOL_GROUP_DIM, POOL_GROUP_DIM), POOL_GROUP_DIM ** -0.5),
        "even_pool_scale": 1.0 + nrm(ks[3], (N_EVEN, POOL_WIDTH), 0.1),
        "even_b_f": jax.random.uniform(ks[4], (N_EVEN, FOX_HEADS), jnp.float32, 1.0, 4.0),
        "even_w_out": nrm(ks[5], (N_EVEN, MIX_WIDTH, D_MODEL), DN_BETA * MIX_WIDTH ** -0.5),
        "odd_w_in": nrm(ks[6], (N_ODD, D_MODEL, 2 * LRU_WIDTH), D_MODEL ** -0.5),
        "odd_conv_w": nrm(ks[7], (N_ODD, CONV_WIDTH, LRU_WIDTH), CONV_WIDTH ** -0.5),
        "odd_conv_b": nrm(ks[8], (N_ODD, LRU_WIDTH), 0.02),
        "odd_w_a": nrm(ks[9], (N_ODD, LRU_BLOCKS, LRU_BLOCK_DIM, LRU_BLOCK_DIM), LRU_BLOCK_DIM ** -0.5),
        "odd_b_a": nrm(ks[10], (N_ODD, LRU_BLOCKS, LRU_BLOCK_DIM), 0.1),
        "odd_w_x": nrm(ks[11], (N_ODD, LRU_BLOCKS, LRU_BLOCK_DIM, LRU_BLOCK_DIM), LRU_BLOCK_DIM ** -0.5),
        "odd_b_x": nrm(ks[12], (N_ODD, LRU_BLOCKS, LRU_BLOCK_DIM), 0.1),
        "odd_lambda": lam,
        "odd_w_out": nrm(ks[14], (N_ODD, LRU_WIDTH, D_MODEL), DN_BETA * LRU_WIDTH ** -0.5),
        "moe_w_group": nrm(ks[15], (DEPTH, D_MODEL, N_GROUPS), D_MODEL ** -0.5),
        "moe_b_group": nrm(ks[16], (DEPTH, N_GROUPS), 0.01),
        "moe_w_expert": nrm(ks[17], (DEPTH, D_MODEL, N_EXPERTS), D_MODEL ** -0.5),
        "moe_b_expert": nrm(ks[18], (DEPTH, N_EXPERTS), 0.01),
        "moe_w1": nrm(ks[19], (DEPTH, N_EXPERTS, D_MODEL, EXPERT_FF), D_MODEL ** -0.5),
        "moe_w3": nrm(ks[20], (DEPTH, N_EXPERTS, D_MODEL, EXPERT_FF), D_MODEL ** -0.5),
        "moe_w2": nrm(ks[21], (DEPTH, N_EXPERTS, EXPERT_FF, D_MODEL), DN_BETA * EXPERT_FF ** -0.5),
        "ln_g": 1.0 + nrm(ks[22], (DEPTH, 2, D_MODEL), 0.05),
        "ln_b": nrm(ks[23], (DEPTH, 2, D_MODEL), 0.02),
    }


def reference(x, even_w_in, even_w_pool, even_pool_scale, even_b_f, even_w_out,
              odd_w_in, odd_conv_w, odd_conv_b, odd_w_a, odd_b_a, odd_w_x, odd_b_x,
              odd_lambda, odd_w_out, moe_w_group, moe_b_group, moe_w_expert, moe_b_expert,
              moe_w1, moe_w3, moe_w2, ln_g, ln_b):
    for layer in range(DEPTH):
        i = layer // 2
        if layer % 2 == 0:
            mix = even_mixer(x, even_w_in[i], even_w_pool[i], even_pool_scale[i],
                             even_b_f[i], even_w_out[i])
        else:
            mix = rglru_mixer(x, odd_w_in[i], odd_conv_w[i], odd_conv_b[i], odd_w_a[i],
                              odd_b_a[i], odd_w_x[i], odd_b_x[i], odd_lambda[i], odd_w_out[i])
        x = layer_norm(DN_ALPHA * x + mix, ln_g[layer, 0], ln_b[layer, 0])
        ffn = hierarchical_moe(x, moe_w_group[layer], moe_b_group[layer], moe_w_expert[layer],
                               moe_b_expert[layer], moe_w1[layer], moe_w3[layer], moe_w2[layer])
        x = layer_norm(DN_ALPHA * x + ffn, ln_g[layer, 1], ln_b[layer, 1])
    return x
```

```python
import functools

import jax
import jax.numpy as jnp
from jax import lax
from jax.experimental import pallas as pl
from jax.experimental.pallas import tpu as pltpu

F32 = jnp.float32
BF16 = jnp.bfloat16

POOL_WINDOWS = (2, 4, 8, 16)
POOL_HALO = 16
HEAD_DIM = 128
LRU_BLOCKS = 16
CONV_WIDTH = 4
CONV_HALO = 8
LRU_C = 8.0
N_GROUPS = 4
EXPERTS_PER_GROUP = 8
TOP_K = 2
DEPTH = 2
DN_ALPHA = (2.0 * DEPTH) ** 0.25
LN_EPS = 1e-5

LANES = 128
SUBLANES = 8
VMEM_LIMIT_MB = 56


def _params(semantics, vmem_mb=VMEM_LIMIT_MB):
    return pltpu.CompilerParams(dimension_semantics=semantics,
                                vmem_limit_bytes=vmem_mb * 1024 * 1024)


def _mm_kernel(a_ref, w_ref, o_ref):
    o_ref[...] = jnp.dot(a_ref[...], w_ref[...],
                         preferred_element_type=F32).astype(o_ref.dtype)


def matmul(a, w, *, tm, tn, out_dtype):
    M, K = a.shape
    N = w.shape[1]
    return pl.pallas_call(
        _mm_kernel,
        grid=(M // tm, N // tn),
        in_specs=[pl.BlockSpec((tm, K), lambda i, j: (i, 0)),
                  pl.BlockSpec((K, tn), lambda i, j: (0, j))],
        out_specs=pl.BlockSpec((tm, tn), lambda i, j: (i, j)),
        out_shape=jax.ShapeDtypeStruct((M, N), out_dtype),
        compiler_params=_params(("parallel", "parallel")),
        name="proj_matmul",
    )(a, w)


def _layer_norm_rows(z, g, b):
    mu = jnp.mean(z, axis=-1, keepdims=True)
    d = z - mu
    var = jnp.mean(d * d, axis=-1, keepdims=True)
    return d * lax.rsqrt(var + LN_EPS) * g + b


def _mm_ln_kernel(a_ref, w_ref, x_ref, g_ref, b_ref, o_ref, ob_ref):
    k = pl.program_id(1)
    part = jnp.dot(a_ref[...], w_ref[...], preferred_element_type=F32)

    @pl.when(k == 0)
    def _():
        o_ref[...] = part

    @pl.when(k > 0)
    def _():
        o_ref[...] += part

    @pl.when(k == pl.num_programs(1) - 1)
    def _():
        y = _layer_norm_rows(DN_ALPHA * x_ref[...] + o_ref[...], g_ref[...], b_ref[...])
        o_ref[...] = y
        ob_ref[...] = y.astype(BF16)


def matmul_residual_ln(a, w, x, g, b, *, tm, tk):
    M, K = a.shape
    N = w.shape[1]
    return pl.pallas_call(
        _mm_ln_kernel,
        grid=(M // tm, K // tk),
        in_specs=[pl.BlockSpec((tm, tk), lambda i, k: (i, k)),
                  pl.BlockSpec((tk, N), lambda i, k: (k, 0)),
                  pl.BlockSpec((tm, N), lambda i, k: (i, 0)),
                  pl.BlockSpec((1, N), lambda i, k: (0, 0)),
                  pl.BlockSpec((1, N), lambda i, k: (0, 0))],
        out_specs=[pl.BlockSpec((tm, N), lambda i, k: (i, 0)),
                   pl.BlockSpec((tm, N), lambda i, k: (i, 0))],
        out_shape=[jax.ShapeDtypeStruct((M, N), F32),
                   jax.ShapeDtypeStruct((M, N), BF16)],
        compiler_params=_params(("parallel", "arbitrary")),
        name="out_proj_ln",
    )(a, w, x, g.reshape(1, N), b.reshape(1, N))


def _cumsum_rows(v, n_rows):
    row = lax.broadcasted_iota(jnp.int32, v.shape, 0)
    shift = 1
    while shift < n_rows:
        v = v + jnp.where(row >= shift, pltpu.roll(v, shift, 0), 0.0)
        shift *= 2
    return v


def _forget_kernel(x_ref, w_ref, bf_ref, c_ref, carry_ref, *, tm):
    @pl.when(pl.program_id(1) == 0)
    def _():
        carry_ref[...] = jnp.zeros_like(carry_ref)

    z = jnp.dot(x_ref[...], w_ref[...], preferred_element_type=F32) + bf_ref[...]
    log_f = jnp.minimum(z, 0.0) - jnp.log1p(jnp.exp(-jnp.abs(z)))
    c = _cumsum_rows(log_f, tm) + carry_ref[...]
    c_ref[...] = c
    carry_ref[...] = c[tm - 1:tm, :]


def forget_cumsum(xb, w_f, b_f, *, batch, tm):
    T, D = xb.shape
    steps = T // batch // tm
    return pl.pallas_call(
        functools.partial(_forget_kernel, tm=tm),
        grid=(batch, steps),
        in_specs=[pl.BlockSpec((tm, D), lambda b, s: (b * steps + s, 0)),
                  pl.BlockSpec((D, LANES), lambda b, s: (0, 0)),
                  pl.BlockSpec((1, LANES), lambda b, s: (0, 0))],
        out_specs=pl.BlockSpec((tm, LANES), lambda b, s: (b * steps + s, 0)),
        out_shape=jax.ShapeDtypeStruct((T, LANES), F32),
        scratch_shapes=[pltpu.VMEM((1, LANES), F32)],
        compiler_params=_params(("parallel", "arbitrary")),
        name="forget_cumsum",
    )(xb, w_f, b_f)


def _pool_kernel(u_ref, w_ref, sc_ref, o_ref, ext_ref, *, ts):
    g = pl.program_id(1)
    si = pl.program_id(2)

    @pl.when(si == 0)
    def _():
        ext_ref[0:POOL_HALO, :] = jnp.zeros((POOL_HALO, ext_ref.shape[1]), F32)

    @pl.when(si > 0)
    def _():
        ext_ref[0:POOL_HALO, :] = ext_ref[ts:ts + POOL_HALO, :]

    u = u_ref[...].astype(F32)
    ext_ref[POOL_HALO:, :] = u
    e1 = ext_ref[...]
    e2 = e1 + pltpu.roll(e1, 1, 0)
    e4 = e2 + pltpu.roll(e2, 2, 0)
    e8 = e4 + pltpu.roll(e4, 4, 0)
    e16 = e8 + pltpu.roll(e8, 8, 0)
    win = jnp.where(g == 0, e2, jnp.where(g == 1, e4, jnp.where(g == 2, e8, e16)))
    win = win[POOL_HALO:, :]
    width = jnp.left_shift(2, g)
    t = si * ts + lax.broadcasted_iota(jnp.int32, (ts, 1), 0)
    cnt = jnp.minimum(t + 1, width).astype(F32)
    pooled = win / cnt - u
    y = jnp.dot(pooled.astype(BF16), w_ref[...], preferred_element_type=F32)
    o_ref[...] = (y * sc_ref[...]).astype(o_ref.dtype)


def pool_mixer(proj, w_pool_b, pool_scale, *, batch, ts):
    T = proj.shape[0]
    G, C, _ = w_pool_b.shape
    assert POOL_WINDOWS == (2, 4, 8, 16) and G == len(POOL_WINDOWS)
    steps = T // batch // ts
    return pl.pallas_call(
        functools.partial(_pool_kernel, ts=ts),
        grid=(batch, G, steps),
        in_specs=[pl.BlockSpec((ts, C), lambda b, g, s: (b * steps + s, g)),
                  pl.BlockSpec((None, C, C), lambda b, g, s: (g, 0, 0)),
                  pl.BlockSpec((1, C), lambda b, g, s: (0, g))],
        out_specs=pl.BlockSpec((ts, C), lambda b, g, s: (b * steps + s, g)),
        out_shape=jax.ShapeDtypeStruct((T, G * C), BF16),
        scratch_shapes=[pltpu.VMEM((ts + POOL_HALO, C), F32)],
        compiler_params=_params(("parallel", "parallel", "arbitrary")),
        name="pool_mixer",
    )(proj, w_pool_b, pool_scale.reshape(1, G * C))


def _attn_kernel(q_ref, k_ref, v_ref, c_ref, o_ref, *, tq, scale):
    qi = pl.program_id(2)
    q = (q_ref[...].astype(F32) * scale).astype(BF16)
    q0 = pl.multiple_of(qi * tq, tq)
    cq_row = c_ref[:, pl.ds(q0, tq)]
    row = lax.broadcasted_iota(jnp.int32, (tq, tq), 0)
    col = lax.broadcasted_iota(jnp.int32, (tq, tq), 1)
    cq = jnp.sum(jnp.where(row == col, cq_row, 0.0), axis=1, keepdims=True)

    def step(k, v, ck, m, l, acc, masked):
        s = lax.dot_general(q, k, (((1,), (1,)), ((), ())), preferred_element_type=F32)
        s = s + cq - ck
        if masked:
            s = jnp.where(col <= row, s, -jnp.inf)
        m_new = jnp.maximum(m, jnp.max(s, axis=1, keepdims=True))
        alpha = jnp.exp(m - m_new)
        p = jnp.exp(s - m_new)
        l = alpha * l + jnp.sum(p, axis=1, keepdims=True)
        acc = alpha * acc + jnp.dot(p.astype(BF16), v, preferred_element_type=F32)
        return m_new, l, acc

    def body(kk, carry):
        k0 = pl.multiple_of(kk * tq, tq)
        return step(k_ref[pl.ds(k0, tq), :], v_ref[pl.ds(k0, tq), :],
                    c_ref[:, pl.ds(k0, tq)], *carry, masked=False)

    init = (jnp.full((tq, 1), -jnp.inf, F32), jnp.zeros((tq, 1), F32),
            jnp.zeros((tq, HEAD_DIM), F32))
    m, l, acc = lax.fori_loop(0, qi, body, init)
    m, l, acc = step(k_ref[pl.ds(q0, tq), :], v_ref[pl.ds(q0, tq), :], cq_row,
                     m, l, acc, masked=True)
    o_ref[...] = (acc / l).astype(o_ref.dtype)


def forgetting_attention(proj, c_row, *, batch, heads, q_col, k_col, v_col, tq):
    T = proj.shape[0]
    S = T // batch
    nq = S // tq
    return pl.pallas_call(
        functools.partial(_attn_kernel, tq=tq, scale=HEAD_DIM ** -0.5),
        grid=(batch, heads, nq),
        in_specs=[pl.BlockSpec((tq, HEAD_DIM), lambda b, h, i: (b * nq + i, q_col + h)),
                  pl.BlockSpec((S, HEAD_DIM), lambda b, h, i: (b, k_col + h)),
                  pl.BlockSpec((S, HEAD_DIM), lambda b, h, i: (b, v_col + h)),
                  pl.BlockSpec((None, None, 1, S), lambda b, h, i: (b, h, 0, 0))],
        out_specs=pl.BlockSpec((tq, HEAD_DIM), lambda b, h, i: (b * nq + i, h)),
        out_shape=jax.ShapeDtypeStruct((T, heads * HEAD_DIM), BF16),
        compiler_params=_params(("parallel", "parallel", "arbitrary")),
        name="forgetting_attention",
    )(proj, proj, proj, c_row)


def _rglru_kernel(xb_ref, gate_ref, cw_ref, cb_ref, wa_ref, ba_ref, wx_ref, bx_ref, lam_ref,
                  y_ref, ext_ref, a_ref, b_ref, h_ref, carry_ref, *, ts):
    si = pl.program_id(2)
    C = xb_ref.shape[1]

    @pl.when(si == 0)
    def _():
        ext_ref[0:CONV_HALO, :] = jnp.zeros((CONV_HALO, C), F32)
        carry_ref[...] = jnp.zeros_like(carry_ref)

    @pl.when(si > 0)
    def _():
        ext_ref[0:CONV_HALO, :] = ext_ref[ts:ts + CONV_HALO, :]

    xb = xb_ref[...].astype(F32)
    ext_ref[CONV_HALO:, :] = xb
    cw = cw_ref[...]
    xc = cb_ref[...] + xb * cw[CONV_WIDTH - 1:CONV_WIDTH, :]
    for lag in range(1, CONV_WIDTH):
        tap = CONV_WIDTH - 1 - lag
        xc = xc + ext_ref[CONV_HALO - lag:CONV_HALO - lag + ts, :] * cw[tap:tap + 1, :]

    xcb = xc.astype(BF16)
    r = jax.nn.sigmoid(jnp.dot(xcb, wa_ref[...], preferred_element_type=F32) + ba_ref[...])
    ig = jax.nn.sigmoid(jnp.dot(xcb, wx_ref[...], preferred_element_type=F32) + bx_ref[...])
    lam = lam_ref[...]
    log_a_base = -(jnp.maximum(-lam, 0.0) + jnp.log1p(jnp.exp(-jnp.abs(lam))))
    log_a = LRU_C * r * log_a_base
    a = jnp.exp(log_a)
    mult = jnp.sqrt(-jnp.tanh(log_a) * (1.0 + a * a))
    b = mult * ig * xc

    groups = ts // SUBLANES
    a3 = a.reshape(groups, SUBLANES, C)
    b3 = b.reshape(groups, SUBLANES, C)
    sub = lax.broadcasted_iota(jnp.int32, a3.shape, 1)
    shift = 1
    while shift < SUBLANES:
        keep = sub >= shift
        b3 = jnp.where(keep, a3 * pltpu.roll(b3, shift, 1) + b3, b3)
        a3 = jnp.where(keep, a3 * pltpu.roll(a3, shift, 1), a3)
        shift *= 2
    a_ref[...] = a3.reshape(ts, C)
    b_ref[...] = b3.reshape(ts, C)

    def body(gi, carry):
        r0 = pl.multiple_of(gi * SUBLANES, SUBLANES)
        hg = a_ref[pl.ds(r0, SUBLANES), :] * carry + b_ref[pl.ds(r0, SUBLANES), :]
        h_ref[pl.ds(r0, SUBLANES), :] = hg
        return hg[SUBLANES - 1:SUBLANES, :]

    carry_ref[...] = lax.fori_loop(0, groups, body, carry_ref[...], unroll=8)

    gate = gate_ref[...].astype(F32)
    gelu = 0.5 * gate * (1.0 + jnp.tanh(0.7978845608028654 * (gate + 0.044715 * gate * gate * gate)))
    y_ref[...] = (h_ref[...] * gelu).astype(y_ref.dtype)


def rglru_mixer(proj, conv_w, conv_b, w_a_b, b_a, w_x_b, b_x, lam, *, batch, ts):
    T = proj.shape[0]
    NB, C, _ = w_a_b.shape
    W = NB * C
    steps = T // batch // ts
    vec = lambda: pl.BlockSpec((1, C), lambda b, c, s: (0, c))
    blk = lambda: pl.BlockSpec((None, C, C), lambda b, c, s: (c, 0, 0))
    return pl.pallas_call(
        functools.partial(_rglru_kernel, ts=ts),
        grid=(batch, NB, steps),
        in_specs=[pl.BlockSpec((ts, C), lambda b, c, s: (b * steps + s, NB + c)),
                  pl.BlockSpec((ts, C), lambda b, c, s: (b * steps + s, c)),
                  pl.BlockSpec((CONV_WIDTH, C), lambda b, c, s: (0, c)),
                  vec(), blk(), vec(), blk(), vec(), vec()],
        out_specs=pl.BlockSpec((ts, C), lambda b, c, s: (b * steps + s, c)),
        out_shape=jax.ShapeDtypeStruct((T, W), BF16),
        scratch_shapes=[pltpu.VMEM((ts + CONV_HALO, C), F32),
                        pltpu.VMEM((ts, C), F32), pltpu.VMEM((ts, C), F32),
                        pltpu.VMEM((ts, C), F32), pltpu.VMEM((1, C), F32)],
        compiler_params=_params(("parallel", "parallel", "arbitrary")),
        name="rglru_mixer",
    )(proj, proj, conv_w, conv_b.reshape(1, W), w_a_b, b_a.reshape(1, W), w_x_b,
      b_x.reshape(1, W), lam.reshape(1, W))


def _router_kernel(x_ref, w_ref, b_ref, id_ref, wt_ref):
    logits = jnp.dot(x_ref[...], w_ref[...], preferred_element_type=F32,
                     precision=lax.Precision.HIGHEST) + b_ref[...]
    lane = lax.broadcasted_iota(jnp.int32, logits.shape, 1).astype(F32)
    neg = -jnp.inf
    big = float(LANES)

    is_group = lane < N_GROUPS
    gl = jnp.where(is_group, logits, neg)
    gmax = jnp.max(gl, axis=1, keepdims=True)
    g_idx = jnp.min(jnp.where(gl == gmax, lane, big), axis=1, keepdims=True)
    g_w = 1.0 / jnp.sum(jnp.where(is_group, jnp.exp(gl - gmax), 0.0), axis=1, keepdims=True)

    lo = N_GROUPS + g_idx * EXPERTS_PER_GROUP
    el = jnp.where(lane >= lo, jnp.where(lane < lo + EXPERTS_PER_GROUP, logits, neg), neg)
    v1 = jnp.max(el, axis=1, keepdims=True)
    i1 = jnp.min(jnp.where(el == v1, lane, big), axis=1, keepdims=True)
    el2 = jnp.where(lane == i1, neg, el)
    v2 = jnp.max(el2, axis=1, keepdims=True)
    i2 = jnp.min(jnp.where(el2 == v2, lane, big), axis=1, keepdims=True)
    t = jnp.exp(v2 - v1)
    w1 = g_w / (1.0 + t)
    w2 = w1 * t
    ids = jnp.where(lane == 0, i1 - N_GROUPS, jnp.where(lane == 1, i2 - N_GROUPS, 0.0))
    id_ref[...] = ids.astype(jnp.int32)
    wt_ref[...] = jnp.where(lane == 0, w1, jnp.where(lane == 1, w2, 0.0))


def router(x, w_route, b_route, *, tm):
    T, D = x.shape
    return pl.pallas_call(
        _router_kernel,
        grid=(T // tm,),
        in_specs=[pl.BlockSpec((tm, D), lambda i: (i, 0)),
                  pl.BlockSpec((D, LANES), lambda i: (0, 0)),
                  pl.BlockSpec((1, LANES), lambda i: (0, 0))],
        out_specs=[pl.BlockSpec((tm, LANES), lambda i: (i, 0)),
                   pl.BlockSpec((tm, LANES), lambda i: (i, 0))],
        out_shape=[jax.ShapeDtypeStruct((T, LANES), jnp.int32),
                   jax.ShapeDtypeStruct((T, LANES), F32)],
        compiler_params=_params(("parallel",)),
        name="router",
    )(x, w_route, b_route)


def _row_gather_start(src_hbm, dst_ref, sem, row_of, n_rows):
    def body(r, carry):
        pltpu.make_async_copy(src_hbm.at[pl.ds(row_of(r), 1), :],
                              dst_ref.at[pl.ds(r, 1), :], sem).start()
        return carry
    lax.fori_loop(0, n_rows, body, 0, unroll=8)


def _row_gather_wait(src_hbm, dst_ref, sem, n_rows):
    pltpu.make_async_copy(src_hbm.at[pl.ds(0, n_rows), :], dst_ref, sem).wait()


def _is_new_expert(te_ref, j):
    return jnp.logical_or(j == 0, te_ref[j] != te_ref[jnp.maximum(j - 1, 0)])


def _moe_up_kernel(te_ref, rt_ref, nu_ref, x_hbm, w1_ref, w3_ref, h_ref,
                   xg_ref, w1b_ref, w3b_ref, sem, *, tm):
    j = pl.program_id(0)
    n_used = nu_ref[0]

    def gather(tile, slot):
        _row_gather_start(x_hbm, xg_ref.at[slot], sem.at[slot],
                          lambda r: rt_ref[tile * tm + r], tm)

    @pl.when(j == 0)
    def _():
        gather(0, 0)

    @pl.when(j + 1 < n_used)
    def _():
        gather(j + 1, (j + 1) % 2)

    @pl.when(j < n_used)
    def _():
        slot = j % 2
        _row_gather_wait(x_hbm, xg_ref.at[slot], sem.at[slot], tm)

        @pl.when(_is_new_expert(te_ref, j))
        def _():
            w1b_ref[...] = w1_ref[...].astype(BF16)
            w3b_ref[...] = w3_ref[...].astype(BF16)

        x = xg_ref[slot].astype(BF16)
        a = jnp.dot(x, w1b_ref[...], preferred_element_type=F32)
        b = jnp.dot(x, w3b_ref[...], preferred_element_type=F32)
        h_ref[...] = (a * jax.nn.sigmoid(a) * b).astype(h_ref.dtype)

    @pl.when(j >= n_used)
    def _():
        h_ref[...] = jnp.zeros_like(h_ref)


def moe_up(x, w1, w3, tile_expert, row_token, n_used, *, tm):
    D = x.shape[1]
    E, _, FF = w1.shape
    n_tiles = tile_expert.shape[0]
    wspec = lambda: pl.BlockSpec((None, D, FF), lambda j, te, rt, nu: (te[j], 0, 0))
    return pl.pallas_call(
        functools.partial(_moe_up_kernel, tm=tm),
        grid_spec=pltpu.PrefetchScalarGridSpec(
            num_scalar_prefetch=3,
            grid=(n_tiles,),
            in_specs=[pl.BlockSpec(memory_space=pl.ANY), wspec(), wspec()],
            out_specs=pl.BlockSpec((tm, FF), lambda j, te, rt, nu: (j, 0)),
            scratch_shapes=[pltpu.VMEM((2, tm, D), F32),
                            pltpu.VMEM((D, FF), BF16), pltpu.VMEM((D, FF), BF16),
                            pltpu.SemaphoreType.DMA((2,))]),
        out_shape=jax.ShapeDtypeStruct((n_tiles * tm, FF), BF16),
        compiler_params=_params(("arbitrary",)),
        name="moe_up",
    )(tile_expert, row_token, n_used, x, w1, w3)


def _moe_down_kernel(te_ref, nu_ref, h_ref, w2_ref, y_ref, w2b_ref):
    j = pl.program_id(0)

    @pl.when(j < nu_ref[0])
    def _():
        @pl.when(_is_new_expert(te_ref, j))
        def _():
            w2b_ref[...] = w2_ref[...].astype(BF16)

        y_ref[...] = jnp.dot(h_ref[...], w2b_ref[...], preferred_element_type=F32)

    @pl.when(j >= nu_ref[0])
    def _():
        y_ref[...] = jnp.zeros_like(y_ref)


def moe_down(h, w2, tile_expert, n_used, *, tm):
    E, FF, D = w2.shape
    n_tiles = tile_expert.shape[0]
    tile = lambda j, te, nu: (j, 0)
    return pl.pallas_call(
        _moe_down_kernel,
        grid_spec=pltpu.PrefetchScalarGridSpec(
            num_scalar_prefetch=2,
            grid=(n_tiles,),
            in_specs=[pl.BlockSpec((tm, FF), tile),
                      pl.BlockSpec((None, FF, D), lambda j, te, nu: (te[j], 0, 0))],
            out_specs=pl.BlockSpec((tm, D), tile),
            scratch_shapes=[pltpu.VMEM((FF, D), BF16)]),
        out_shape=jax.ShapeDtypeStruct((n_tiles * tm, D), F32),
        compiler_params=_params(("arbitrary",)),
        name="moe_down",
    )(tile_expert, n_used, h, w2)


def _combine_ln_kernel(pos_ref, y_hbm, x_ref, wt_ref, g_ref, b_ref, o_ref, ob_ref,
                       yg_ref, sem, *, tm):
    i = pl.program_id(0)

    def gather(tile, slot):
        for k in range(TOP_K):
            _row_gather_start(y_hbm, yg_ref.at[slot, k], sem.at[slot],
                              lambda r: pos_ref[TOP_K * (tile * tm + r) + k], tm)

    @pl.when(i == 0)
    def _():
        gather(0, 0)

    @pl.when(i + 1 < pl.num_programs(0))
    def _():
        gather(i + 1, (i + 1) % 2)

    slot = i % 2
    for k in range(TOP_K):
        _row_gather_wait(y_hbm, yg_ref.at[slot, k], sem.at[slot], tm)
    wt = wt_ref[...]
    z = DN_ALPHA * x_ref[...]
    for k in range(TOP_K):
        z = z + wt[:, k:k + 1] * yg_ref[slot, k]
    y = _layer_norm_rows(z, g_ref[...], b_ref[...])
    o_ref[...] = y
    ob_ref[...] = y.astype(BF16)


def moe_combine_ln(y, pos, x, wts, g, b, *, tm):
    T, D = x.shape
    row = lambda i, pos: (i, 0)
    fixed = lambda i, pos: (0, 0)
    return pl.pallas_call(
        functools.partial(_combine_ln_kernel, tm=tm),
        grid_spec=pltpu.PrefetchScalarGridSpec(
            num_scalar_prefetch=1,
            grid=(T // tm,),
            in_specs=[pl.BlockSpec(memory_space=pl.ANY),
                      pl.BlockSpec((tm, D), row),
                      pl.BlockSpec((tm, LANES), row),
                      pl.BlockSpec((1, D), fixed),
                      pl.BlockSpec((1, D), fixed)],
            out_specs=[pl.BlockSpec((tm, D), row), pl.BlockSpec((tm, D), row)],
            scratch_shapes=[pltpu.VMEM((2, TOP_K, tm, D), F32),
                            pltpu.SemaphoreType.DMA((2,))]),
        out_shape=[jax.ShapeDtypeStruct((T, D), F32), jax.ShapeDtypeStruct((T, D), BF16)],
        compiler_params=_params(("arbitrary",)),
        name="moe_combine_ln",
    )(pos, y, x, wts, g.reshape(1, D), b.reshape(1, D))


def _dispatch_plan(ids, *, n_experts, tm, n_tiles):
    n_assign = ids.shape[0]
    onehot = (ids[:, None] == jnp.arange(n_experts, dtype=jnp.int32)[None, :]).astype(jnp.int32)
    csum = jnp.cumsum(onehot, axis=0)
    rank = jnp.sum((csum - onehot) * onehot, axis=1)
    counts = csum[-1]
    tiles_per = (counts + tm - 1) // tm
    tile_end = jnp.cumsum(tiles_per)
    tile_start = tile_end - tiles_per
    n_used = tile_end[-1]
    pos = tile_start[ids] * tm + rank
    row_token = jnp.zeros((n_tiles * tm,), jnp.int32).at[pos].set(
        jnp.arange(n_assign, dtype=jnp.int32) // TOP_K)
    tile_ids = jnp.minimum(jnp.arange(n_tiles, dtype=jnp.int32), n_used - 1)
    tile_expert = jnp.sum((tile_ids[:, None] >= tile_end[None, :]).astype(jnp.int32), axis=1)
    return pos.astype(jnp.int32), row_token, tile_expert.astype(jnp.int32), \
        n_used.reshape(1).astype(jnp.int32)


def hierarchical_moe_ln(x, w_group, b_group, w_expert, b_expert, w1, w3, w2, g, b, *,
                        tm_route, tm_moe, tm_comb):
    T, D = x.shape
    E = w1.shape[0]
    n_route = N_GROUPS + E
    w_route = jnp.zeros((D, LANES), F32).at[:, :N_GROUPS].set(w_group).at[:, N_GROUPS:n_route].set(w_expert)
    b_route = jnp.zeros((1, LANES), F32).at[0, :N_GROUPS].set(b_group).at[0, N_GROUPS:n_route].set(b_expert)
    ids, wts = router(x, w_route, b_route, tm=tm_route)
    n_tiles = (T * TOP_K) // tm_moe + E
    pos, row_token, tile_expert, n_used = _dispatch_plan(
        ids[:, :TOP_K].reshape(-1), n_experts=E, tm=tm_moe, n_tiles=n_tiles)
    h = moe_up(x, w1, w3, tile_expert, row_token, n_used, tm=tm_moe)
    y = moe_down(h, w2, tile_expert, n_used, tm=tm_moe)
    return moe_combine_ln(y, pos, x, wts, g, b, tm=tm_comb)


def kernel(x, even_w_in, even_w_pool, even_pool_scale, even_b_f, even_w_out, odd_w_in, odd_conv_w, odd_conv_b, odd_w_a, odd_b_a, odd_w_x, odd_b_x, odd_lambda, odd_w_out, moe_w_group, moe_b_group, moe_w_expert, moe_b_expert, moe_w1, moe_w3, moe_w2, ln_g, ln_b):
    B, S, D = x.shape
    T = B * S
    xf = x.reshape(T, D)
    xb = xf.astype(BF16)
    tiles = _tile_plan(T, S, D)

    for layer in range(DEPTH):
        i = layer // 2
        if layer % 2 == 0:
            pool_w = even_w_pool.shape[1] * even_w_pool.shape[2]
            heads = even_b_f.shape[1]
            fox_w = heads * HEAD_DIM
            w_in = even_w_in[i]
            proj = matmul(xb, w_in[:, :pool_w + 3 * fox_w].astype(BF16),
                          tm=tiles["mm_m"], tn=tiles["mm_n"], out_dtype=BF16)
            w_f = jnp.zeros((D, LANES), BF16).at[:, :heads].set(w_in[:, pool_w + 3 * fox_w:].astype(BF16))
            b_f = jnp.zeros((1, LANES), F32).at[0, :heads].set(even_b_f[i])
            c = forget_cumsum(xb, w_f, b_f, batch=B, tm=tiles["seq"])
            c_row = c.reshape(B, S, LANES)[:, :, :heads].transpose(0, 2, 1).reshape(B, heads, 1, S)
            a_out = pool_mixer(proj, even_w_pool[i].astype(BF16), even_pool_scale[i],
                               batch=B, ts=tiles["seq"])
            qc = pool_w // HEAD_DIM
            b_out = forgetting_attention(proj, c_row, batch=B, heads=heads, q_col=qc,
                                         k_col=qc + heads, v_col=qc + 2 * heads, tq=tiles["attn_q"])
            mixed = jnp.concatenate([a_out, b_out], axis=-1)
            w_out = even_w_out[i].astype(BF16)
        else:
            proj = matmul(xb, odd_w_in[i].astype(BF16), tm=tiles["mm_m"], tn=tiles["mm_n"],
                          out_dtype=BF16)
            mixed = rglru_mixer(proj, odd_conv_w[i], odd_conv_b[i], odd_w_a[i].astype(BF16),
                                odd_b_a[i].reshape(-1), odd_w_x[i].astype(BF16),
                                odd_b_x[i].reshape(-1), odd_lambda[i], batch=B, ts=tiles["seq"])
            w_out = odd_w_out[i].astype(BF16)
        xf, xb = matmul_residual_ln(mixed, w_out, xf, ln_g[layer, 0], ln_b[layer, 0],
                                    tm=tiles["ln_m"], tk=tiles["ln_k"])
        xf, xb = hierarchical_moe_ln(xf, moe_w_group[layer], moe_b_group[layer],
                                     moe_w_expert[layer], moe_b_expert[layer], moe_w1[layer],
                                     moe_w3[layer], moe_w2[layer], ln_g[layer, 1], ln_b[layer, 1],
                                     tm_route=tiles["route"], tm_moe=tiles["moe"],
                                     tm_comb=tiles["comb"])
    return xf.reshape(B, S, D)


def _tile_plan(T, S, D):
    return {
        "mm_m": min(1024, T), "mm_n": min(1024, D),
        "seq": min(512, S), "attn_q": min(512, S),
        "ln_m": min(256, T), "ln_k": min(512, D),
        "route": min(512, T), "moe": 256, "comb": min(256, T),
    }
```

```python
import functools

import jax
import jax.numpy as jnp
from jax import lax
from jax.experimental import pallas as pl
from jax.experimental.pallas import tpu as pltpu

F32 = jnp.float32
BF16 = jnp.bfloat16

POOL_WINDOWS = (2, 4, 8, 16)
POOL_HALO = 16
HEAD_DIM = 128
LRU_BLOCKS = 16
CONV_WIDTH = 4
CONV_HALO = 8
LRU_C = 8.0
N_GROUPS = 4
EXPERTS_PER_GROUP = 8
TOP_K = 2
DEPTH = 2
DN_ALPHA = (2.0 * DEPTH) ** 0.25
LN_EPS = 1e-5

LANES = 128
SUBLANES = 8
VMEM_LIMIT_MB = 56


def _params(semantics, vmem_mb=VMEM_LIMIT_MB):
    return pltpu.CompilerParams(dimension_semantics=semantics,
                                vmem_limit_bytes=vmem_mb * 1024 * 1024)


def _mm_kernel(a_ref, w_ref, o_ref):
    o_ref[...] = jnp.dot(a_ref[...], w_ref[...].astype(BF16),
                         preferred_element_type=F32).astype(o_ref.dtype)


def matmul(a, w, *, n_cols, tm, tn, out_dtype):
    M, K = a.shape
    return pl.pallas_call(
        _mm_kernel,
        grid=(n_cols // tn, M // tm),
        in_specs=[pl.BlockSpec((tm, K), lambda j, i: (i, 0)),
                  pl.BlockSpec((K, tn), lambda j, i: (0, j))],
        out_specs=pl.BlockSpec((tm, tn), lambda j, i: (i, j)),
        out_shape=jax.ShapeDtypeStruct((M, n_cols), out_dtype),
        compiler_params=_params(("parallel", "parallel")),
        name="proj_matmul",
    )(a, w)


def _layer_norm_rows(z, g, b):
    mu = jnp.mean(z, axis=-1, keepdims=True)
    d = z - mu
    var = jnp.mean(d * d, axis=-1, keepdims=True)
    return d * lax.rsqrt(var + LN_EPS) * g + b


def _mm_ln_kernel(*refs, n_a, k_per):
    a_refs = refs[:n_a]
    w_ref, x_ref, g_ref, b_ref, o_ref = refs[n_a:]
    k = pl.program_id(1)

    for idx, a_ref in enumerate(a_refs):
        @pl.when(jnp.logical_and(k >= idx * k_per, k < (idx + 1) * k_per))
        def _(a_ref=a_ref, idx=idx):
            part = jnp.dot(a_ref[...], w_ref[...], preferred_element_type=F32)
            if idx == 0:
                @pl.when(k == 0)
                def _():
                    o_ref[...] = part

                @pl.when(k > 0)
                def _():
                    o_ref[...] += part
            else:
                o_ref[...] += part

    @pl.when(k == pl.num_programs(1) - 1)
    def _():
        o_ref[...] = _layer_norm_rows(DN_ALPHA * x_ref[...] + o_ref[...], g_ref[...], b_ref[...])


def matmul_residual_ln(a_list, w, x, g, b, *, tm, tk):
    M, K_each = a_list[0].shape
    N = w.shape[1]
    n_a = len(a_list)
    k_per = K_each // tk
    a_spec = lambda idx: pl.BlockSpec(
        (tm, tk), lambda i, k: (i, jnp.clip(k - idx * k_per, 0, k_per - 1)))
    return pl.pallas_call(
        functools.partial(_mm_ln_kernel, n_a=n_a, k_per=k_per),
        grid=(M // tm, n_a * k_per),
        in_specs=[a_spec(idx) for idx in range(n_a)] + [
            pl.BlockSpec((tk, N), lambda i, k: (k, 0)),
            pl.BlockSpec((tm, N), lambda i, k: (i, 0)),
            pl.BlockSpec((1, N), lambda i, k: (0, 0)),
            pl.BlockSpec((1, N), lambda i, k: (0, 0))],
        out_specs=pl.BlockSpec((tm, N), lambda i, k: (i, 0)),
        out_shape=jax.ShapeDtypeStruct((M, N), F32),
        compiler_params=_params(("parallel", "arbitrary")),
        name="out_proj_ln",
    )(*a_list, w, x, g.reshape(1, N), b.reshape(1, N))


def _cumsum_rows(v, n_rows):
    row = lax.broadcasted_iota(jnp.int32, v.shape, 0)
    shift = 1
    while shift < n_rows:
        v = v + jnp.where(row >= shift, pltpu.roll(v, shift, 0), 0.0)
        shift *= 2
    return v


def _forget_kernel(x_ref, w_ref, bf_ref, c_ref, carry_ref, *, tm):
    @pl.when(pl.program_id(1) == 0)
    def _():
        carry_ref[...] = jnp.zeros_like(carry_ref)

    z = jnp.dot(x_ref[...], w_ref[...], preferred_element_type=F32) + bf_ref[...]
    log_f = jnp.minimum(z, 0.0) - jnp.log1p(jnp.exp(-jnp.abs(z)))
    c = _cumsum_rows(log_f, tm) + carry_ref[...]
    c_ref[...] = c
    carry_ref[...] = c[tm - 1:tm, :]


def forget_cumsum(xb, w_f, b_f, *, batch, tm):
    T, D = xb.shape
    steps = T // batch // tm
    return pl.pallas_call(
        functools.partial(_forget_kernel, tm=tm),
        grid=(batch, steps),
        in_specs=[pl.BlockSpec((tm, D), lambda b, s: (b * steps + s, 0)),
                  pl.BlockSpec((D, LANES), lambda b, s: (0, 0)),
                  pl.BlockSpec((1, LANES), lambda b, s: (0, 0))],
        out_specs=pl.BlockSpec((tm, LANES), lambda b, s: (b * steps + s, 0)),
        out_shape=jax.ShapeDtypeStruct((T, LANES), F32),
        scratch_shapes=[pltpu.VMEM((1, LANES), F32)],
        compiler_params=_params(("parallel", "arbitrary")),
        name="forget_cumsum",
    )(xb, w_f, b_f)


def _pool_kernel(u_ref, w_ref, sc_ref, o_ref, ext_ref, *, ts):
    g = pl.program_id(1)
    si = pl.program_id(2)

    @pl.when(si == 0)
    def _():
        ext_ref[0:POOL_HALO, :] = jnp.zeros((POOL_HALO, ext_ref.shape[1]), F32)

    @pl.when(si > 0)
    def _():
        ext_ref[0:POOL_HALO, :] = ext_ref[ts:ts + POOL_HALO, :]

    u = u_ref[...].astype(F32)
    ext_ref[POOL_HALO:, :] = u
    e1 = ext_ref[...]
    e2 = e1 + pltpu.roll(e1, 1, 0)
    e4 = e2 + pltpu.roll(e2, 2, 0)
    e8 = e4 + pltpu.roll(e4, 4, 0)
    e16 = e8 + pltpu.roll(e8, 8, 0)
    win = jnp.where(g == 0, e2, jnp.where(g == 1, e4, jnp.where(g == 2, e8, e16)))
    win = win[POOL_HALO:, :]
    width = jnp.left_shift(2, g)
    t = si * ts + lax.broadcasted_iota(jnp.int32, (ts, 1), 0)
    cnt = jnp.minimum(t + 1, width).astype(F32)
    pooled = win / cnt - u
    y = jnp.dot(pooled.astype(BF16), w_ref[...], preferred_element_type=F32)
    o_ref[...] = (y * sc_ref[...]).astype(o_ref.dtype)


def pool_mixer(proj, w_pool_b, pool_scale, *, batch, ts):
    T = proj.shape[0]
    G, C, _ = w_pool_b.shape
    assert POOL_WINDOWS == (2, 4, 8, 16) and G == len(POOL_WINDOWS)
    steps = T // batch // ts
    return pl.pallas_call(
        functools.partial(_pool_kernel, ts=ts),
        grid=(batch, G, steps),
        in_specs=[pl.BlockSpec((ts, C), lambda b, g, s: (b * steps + s, g)),
                  pl.BlockSpec((None, C, C), lambda b, g, s: (g, 0, 0)),
                  pl.BlockSpec((1, C), lambda b, g, s: (0, g))],
        out_specs=pl.BlockSpec((ts, C), lambda b, g, s: (b * steps + s, g)),
        out_shape=jax.ShapeDtypeStruct((T, G * C), BF16),
        scratch_shapes=[pltpu.VMEM((ts + POOL_HALO, C), F32)],
        compiler_params=_params(("parallel", "parallel", "arbitrary")),
        name="pool_mixer",
    )(proj, w_pool_b, pool_scale.reshape(1, G * C))


def _attn_kernel(q_ref, k_ref, v_ref, c_ref, o_ref, *, tq, scale):
    qi = pl.program_id(2)
    q = (q_ref[...].astype(F32) * scale).astype(BF16)
    q0 = pl.multiple_of(qi * tq, tq)
    cq_row = c_ref[:, pl.ds(q0, tq)]
    row = lax.broadcasted_iota(jnp.int32, (tq, tq), 0)
    col = lax.broadcasted_iota(jnp.int32, (tq, tq), 1)
    cq = jnp.sum(jnp.where(row == col, cq_row, 0.0), axis=1, keepdims=True)

    def step(k, v, ck, m, l, acc, masked):
        s = lax.dot_general(q, k, (((1,), (1,)), ((), ())), preferred_element_type=F32)
        s = s + cq - ck
        if masked:
            s = jnp.where(col <= row, s, -jnp.inf)
        m_new = jnp.maximum(m, jnp.max(s, axis=1, keepdims=True))
        alpha = jnp.exp(m - m_new)
        p = jnp.exp(s - m_new)
        l = alpha * l + jnp.sum(p, axis=1, keepdims=True)
        acc = alpha * acc + jnp.dot(p.astype(BF16), v, preferred_element_type=F32)
        return m_new, l, acc

    def body(kk, carry):
        k0 = pl.multiple_of(kk * tq, tq)
        return step(k_ref[pl.ds(k0, tq), :], v_ref[pl.ds(k0, tq), :],
                    c_ref[:, pl.ds(k0, tq)], *carry, masked=False)

    init = (jnp.full((tq, 1), -jnp.inf, F32), jnp.zeros((tq, 1), F32),
            jnp.zeros((tq, HEAD_DIM), F32))
    m, l, acc = lax.fori_loop(0, qi, body, init)
    m, l, acc = step(k_ref[pl.ds(q0, tq), :], v_ref[pl.ds(q0, tq), :], cq_row,
                     m, l, acc, masked=True)
    o_ref[...] = (acc / l).astype(o_ref.dtype)


def forgetting_attention(proj, c_row, *, batch, heads, q_col, k_col, v_col, tq):
    T = proj.shape[0]
    S = T // batch
    nq = S // tq
    return pl.pallas_call(
        functools.partial(_attn_kernel, tq=tq, scale=HEAD_DIM ** -0.5),
        grid=(batch, heads, nq),
        in_specs=[pl.BlockSpec((tq, HEAD_DIM), lambda b, h, i: (b * nq + i, q_col + h)),
                  pl.BlockSpec((S, HEAD_DIM), lambda b, h, i: (b, k_col + h)),
                  pl.BlockSpec((S, HEAD_DIM), lambda b, h, i: (b, v_col + h)),
                  pl.BlockSpec((None, None, 1, S), lambda b, h, i: (b, h, 0, 0))],
        out_specs=pl.BlockSpec((tq, HEAD_DIM), lambda b, h, i: (b * nq + i, h)),
        out_shape=jax.ShapeDtypeStruct((T, heads * HEAD_DIM), BF16),
        compiler_params=_params(("parallel", "parallel", "arbitrary")),
        name="forgetting_attention",
    )(proj, proj, proj, c_row)


def _rglru_kernel(xb_ref, gate_ref, cw_ref, cb_ref, wa_ref, ba_ref, wx_ref, bx_ref, lam_ref,
                  y_ref, ext_ref, a_ref, b_ref, h_ref, carry_ref, *, ts):
    si = pl.program_id(2)
    C = xb_ref.shape[1]

    @pl.when(si == 0)
    def _():
        ext_ref[0:CONV_HALO, :] = jnp.zeros((CONV_HALO, C), F32)
        carry_ref[...] = jnp.zeros_like(carry_ref)

    @pl.when(si > 0)
    def _():
        ext_ref[0:CONV_HALO, :] = ext_ref[ts:ts + CONV_HALO, :]

    xb = xb_ref[...].astype(F32)
    ext_ref[CONV_HALO:, :] = xb
    cw = cw_ref[...]
    xc = cb_ref[...] + xb * cw[CONV_WIDTH - 1:CONV_WIDTH, :]
    for lag in range(1, CONV_WIDTH):
        tap = CONV_WIDTH - 1 - lag
        xc = xc + ext_ref[CONV_HALO - lag:CONV_HALO - lag + ts, :] * cw[tap:tap + 1, :]

    xcb = xc.astype(BF16)
    r = jax.nn.sigmoid(jnp.dot(xcb, wa_ref[...], preferred_element_type=F32) + ba_ref[...])
    ig = jax.nn.sigmoid(jnp.dot(xcb, wx_ref[...], preferred_element_type=F32) + bx_ref[...])
    lam = lam_ref[...]
    log_a_base = -(jnp.maximum(-lam, 0.0) + jnp.log1p(jnp.exp(-jnp.abs(lam))))
    log_a = LRU_C * r * log_a_base
    a = jnp.exp(log_a)
    mult = jnp.sqrt(-jnp.tanh(log_a) * (1.0 + a * a))
    b = mult * ig * xc

    groups = ts // SUBLANES
    a3 = a.reshape(groups, SUBLANES, C)
    b3 = b.reshape(groups, SUBLANES, C)
    sub = lax.broadcasted_iota(jnp.int32, a3.shape, 1)
    shift = 1
    while shift < SUBLANES:
        keep = sub >= shift
        b3 = jnp.where(keep, a3 * pltpu.roll(b3, shift, 1) + b3, b3)
        a3 = jnp.where(keep, a3 * pltpu.roll(a3, shift, 1), a3)
        shift *= 2
    a_ref[...] = a3.reshape(ts, C)
    b_ref[...] = b3.reshape(ts, C)

    def body(gi, carry):
        r0 = pl.multiple_of(gi * SUBLANES, SUBLANES)
        hg = a_ref[pl.ds(r0, SUBLANES), :] * carry + b_ref[pl.ds(r0, SUBLANES), :]
        h_ref[pl.ds(r0, SUBLANES), :] = hg
        return hg[SUBLANES - 1:SUBLANES, :]

    carry_ref[...] = lax.fori_loop(0, groups, body, carry_ref[...], unroll=8)

    gate = gate_ref[...].astype(F32)
    gelu = 0.5 * gate * (1.0 + jnp.tanh(0.7978845608028654 * (gate + 0.044715 * gate * gate * gate)))
    y_ref[...] = (h_ref[...] * gelu).astype(y_ref.dtype)


def rglru_mixer(proj, conv_w, conv_b, w_a_b, b_a, w_x_b, b_x, lam, *, batch, ts):
    T = proj.shape[0]
    NB, C, _ = w_a_b.shape
    W = NB * C
    steps = T // batch // ts
    vec = lambda: pl.BlockSpec((1, C), lambda b, c, s: (0, c))
    blk = lambda: pl.BlockSpec((None, C, C), lambda b, c, s: (c, 0, 0))
    return pl.pallas_call(
        functools.partial(_rglru_kernel, ts=ts),
        grid=(batch, NB, steps),
        in_specs=[pl.BlockSpec((ts, C), lambda b, c, s: (b * steps + s, NB + c)),
                  pl.BlockSpec((ts, C), lambda b, c, s: (b * steps + s, c)),
                  pl.BlockSpec((CONV_WIDTH, C), lambda b, c, s: (0, c)),
                  vec(), blk(), vec(), blk(), vec(), vec()],
        out_specs=pl.BlockSpec((ts, C), lambda b, c, s: (b * steps + s, c)),
        out_shape=jax.ShapeDtypeStruct((T, W), BF16),
        scratch_shapes=[pltpu.VMEM((ts + CONV_HALO, C), F32),
                        pltpu.VMEM((ts, C), F32), pltpu.VMEM((ts, C), F32),
                        pltpu.VMEM((ts, C), F32), pltpu.VMEM((1, C), F32)],
        compiler_params=_params(("parallel", "parallel", "arbitrary")),
        name="rglru_mixer",
    )(proj, proj, conv_w, conv_b.reshape(1, W), w_a_b, b_a.reshape(1, W), w_x_b,
      b_x.reshape(1, W), lam.reshape(1, W))


def _router_kernel(x_ref, w_ref, b_ref, id_ref, wt_ref):
    logits = jnp.dot(x_ref[...], w_ref[...], preferred_element_type=F32,
                     precision=lax.Precision.HIGHEST) + b_ref[...]
    lane = lax.broadcasted_iota(jnp.int32, logits.shape, 1).astype(F32)
    neg = -jnp.inf
    big = float(LANES)

    is_group = lane < N_GROUPS
    gl = jnp.where(is_group, logits, neg)
    gmax = jnp.max(gl, axis=1, keepdims=True)
    g_idx = jnp.min(jnp.where(gl == gmax, lane, big), axis=1, keepdims=True)
    g_w = 1.0 / jnp.sum(jnp.where(is_group, jnp.exp(gl - gmax), 0.0), axis=1, keepdims=True)

    lo = N_GROUPS + g_idx * EXPERTS_PER_GROUP
    el = jnp.where(lane >= lo, jnp.where(lane < lo + EXPERTS_PER_GROUP, logits, neg), neg)
    v1 = jnp.max(el, axis=1, keepdims=True)
    i1 = jnp.min(jnp.where(el == v1, lane, big), axis=1, keepdims=True)
    el2 = jnp.where(lane == i1, neg, el)
    v2 = jnp.max(el2, axis=1, keepdims=True)
    i2 = jnp.min(jnp.where(el2 == v2, lane, big), axis=1, keepdims=True)
    t = jnp.exp(v2 - v1)
    w1 = g_w / (1.0 + t)
    w2 = w1 * t
    ids = jnp.where(lane == 0, i1 - N_GROUPS, jnp.where(lane == 1, i2 - N_GROUPS, 0.0))
    id_ref[...] = ids.astype(jnp.int32)
    wt_ref[...] = jnp.where(lane == 0, w1, jnp.where(lane == 1, w2, 0.0))


def router(x, w_route, b_route, *, tm):
    T, D = x.shape
    return pl.pallas_call(
        _router_kernel,
        grid=(T // tm,),
        in_specs=[pl.BlockSpec((tm, D), lambda i: (i, 0)),
                  pl.BlockSpec((D, LANES), lambda i: (0, 0)),
                  pl.BlockSpec((1, LANES), lambda i: (0, 0))],
        out_specs=[pl.BlockSpec((tm, LANES), lambda i: (i, 0)),
                   pl.BlockSpec((tm, LANES), lambda i: (i, 0))],
        out_shape=[jax.ShapeDtypeStruct((T, LANES), jnp.int32),
                   jax.ShapeDtypeStruct((T, LANES), F32)],
        compiler_params=_params(("parallel",)),
        name="router",
    )(x, w_route, b_route)


def _row_gather_start(src_hbm, dst_ref, sem, row_of, n_rows):
    def body(r, carry):
        pltpu.make_async_copy(src_hbm.at[pl.ds(row_of(r), 1), :],
                              dst_ref.at[pl.ds(r, 1), :], sem).start()
        return carry
    lax.fori_loop(0, n_rows, body, 0, unroll=8)


def _row_gather_wait(src_hbm, dst_ref, sem, n_rows):
    pltpu.make_async_copy(src_hbm.at[pl.ds(0, n_rows), :], dst_ref, sem).wait()


PLAN_EXPERT, PLAN_FIRST, PLAN_SLOT, PLAN_NEXT = range(4)


def _expert_weight_copies(w_hbm_list, buf_list, sem, layer, expert, slot):
    return [pltpu.make_async_copy(w.at[layer, expert], buf.at[slot], sem.at[slot])
            for w, buf in zip(w_hbm_list, buf_list)]


def _expert_weight_pipeline(plan_ref, n_used, w_hbm_list, buf_list, sem, layer):
    j = pl.program_id(0)
    slot = plan_ref[PLAN_SLOT, j]
    first = jnp.logical_and(j < n_used, plan_ref[PLAN_FIRST, j] == 1)

    @pl.when(j == 0)
    def _():
        for cp in _expert_weight_copies(w_hbm_list, buf_list, sem, layer,
                                        plan_ref[PLAN_EXPERT, 0], 0):
            cp.start()

    @pl.when(jnp.logical_and(first, plan_ref[PLAN_NEXT, j] >= 0))
    def _():
        for cp in _expert_weight_copies(w_hbm_list, buf_list, sem, layer,
                                        plan_ref[PLAN_NEXT, j], 1 - slot):
            cp.start()

    @pl.when(first)
    def _():
        for cp in _expert_weight_copies(w_hbm_list, buf_list, sem, layer,
                                        plan_ref[PLAN_EXPERT, j], slot):
            cp.wait()

    return slot


def _moe_up_kernel(plan_ref, rt_ref, nu_ref, x_hbm, w1_hbm, w3_hbm, h_ref,
                   xg_ref, w1_buf, w3_buf, xsem, wsem, *, tm, layer):
    j = pl.program_id(0)
    n_used = nu_ref[0]

    def gather(tile, slot):
        _row_gather_start(x_hbm, xg_ref.at[slot], xsem.at[slot],
                          lambda r: rt_ref[tile * tm + r], tm)

    @pl.when(j == 0)
    def _():
        gather(0, 0)

    @pl.when(j + 1 < n_used)
    def _():
        gather(j + 1, (j + 1) % 2)

    wslot = _expert_weight_pipeline(plan_ref, n_used, (w1_hbm, w3_hbm), (w1_buf, w3_buf),
                                    wsem, layer)

    @pl.when(j < n_used)
    def _():
        slot = j % 2
        _row_gather_wait(x_hbm, xg_ref.at[slot], xsem.at[slot], tm)
        x = xg_ref[slot].astype(BF16)
        a = jnp.dot(x, w1_buf[wslot].astype(BF16), preferred_element_type=F32)
        b = jnp.dot(x, w3_buf[wslot].astype(BF16), preferred_element_type=F32)
        h_ref[...] = (a * jax.nn.sigmoid(a) * b).astype(h_ref.dtype)

    @pl.when(j >= n_used)
    def _():
        h_ref[...] = jnp.zeros_like(h_ref)


def moe_up(x, w1, w3, plan, row_token, n_used, *, layer, tm):
    D = x.shape[1]
    FF = w1.shape[-1]
    n_tiles = plan.shape[1]
    any_spec = lambda: pl.BlockSpec(memory_space=pl.ANY)
    return pl.pallas_call(
        functools.partial(_moe_up_kernel, tm=tm, layer=layer),
        grid_spec=pltpu.PrefetchScalarGridSpec(
            num_scalar_prefetch=3,
            grid=(n_tiles,),
            in_specs=[any_spec(), any_spec(), any_spec()],
            out_specs=pl.BlockSpec((tm, FF), lambda j, plan, rt, nu: (j, 0)),
            scratch_shapes=[pltpu.VMEM((2, tm, D), F32),
                            pltpu.VMEM((2, D, FF), F32), pltpu.VMEM((2, D, FF), F32),
                            pltpu.SemaphoreType.DMA((2,)), pltpu.SemaphoreType.DMA((2,))]),
        out_shape=jax.ShapeDtypeStruct((n_tiles * tm, FF), BF16),
        compiler_params=_params(("arbitrary",)),
        name="moe_up",
    )(plan, row_token, n_used, x, w1, w3)


def _moe_down_kernel(plan_ref, nu_ref, h_ref, w2_hbm, y_ref, w2_buf, wsem, *, layer):
    j = pl.program_id(0)
    n_used = nu_ref[0]
    wslot = _expert_weight_pipeline(plan_ref, n_used, (w2_hbm,), (w2_buf,), wsem, layer)

    @pl.when(j < n_used)
    def _():
        y_ref[...] = jnp.dot(h_ref[...], w2_buf[wslot].astype(BF16),
                             preferred_element_type=F32)

    @pl.when(j >= n_used)
    def _():
        y_ref[...] = jnp.zeros_like(y_ref)


def moe_down(h, w2, plan, n_used, *, layer, tm):
    FF, D = w2.shape[-2:]
    n_tiles = plan.shape[1]
    tile = lambda j, plan, nu: (j, 0)
    return pl.pallas_call(
        functools.partial(_moe_down_kernel, layer=layer),
        grid_spec=pltpu.PrefetchScalarGridSpec(
            num_scalar_prefetch=2,
            grid=(n_tiles,),
            in_specs=[pl.BlockSpec((tm, FF), tile), pl.BlockSpec(memory_space=pl.ANY)],
            out_specs=pl.BlockSpec((tm, D), tile),
            scratch_shapes=[pltpu.VMEM((2, FF, D), F32), pltpu.SemaphoreType.DMA((2,))]),
        out_shape=jax.ShapeDtypeStruct((n_tiles * tm, D), F32),
        compiler_params=_params(("arbitrary",)),
        name="moe_down",
    )(plan, n_used, h, w2)


def _combine_ln_kernel(pos_ref, y_hbm, x_ref, wt_ref, g_ref, b_ref, o_ref, ob_ref,
                       yg_ref, sem, *, tm):
    i = pl.program_id(0)

    def gather(tile, slot):
        for k in range(TOP_K):
            _row_gather_start(y_hbm, yg_ref.at[slot, k], sem.at[slot],
                              lambda r: pos_ref[TOP_K * (tile * tm + r) + k], tm)

    @pl.when(i == 0)
    def _():
        gather(0, 0)

    @pl.when(i + 1 < pl.num_programs(0))
    def _():
        gather(i + 1, (i + 1) % 2)

    slot = i % 2
    for k in range(TOP_K):
        _row_gather_wait(y_hbm, yg_ref.at[slot, k], sem.at[slot], tm)
    wt = wt_ref[...]
    z = DN_ALPHA * x_ref[...]
    for k in range(TOP_K):
        z = z + wt[:, k:k + 1] * yg_ref[slot, k]
    y = _layer_norm_rows(z, g_ref[...], b_ref[...])
    o_ref[...] = y
    ob_ref[...] = y.astype(BF16)


def moe_combine_ln(y, pos, x, wts, g, b, *, tm):
    T, D = x.shape
    row = lambda i, pos: (i, 0)
    fixed = lambda i, pos: (0, 0)
    return pl.pallas_call(
        functools.partial(_combine_ln_kernel, tm=tm),
        grid_spec=pltpu.PrefetchScalarGridSpec(
            num_scalar_prefetch=1,
            grid=(T // tm,),
            in_specs=[pl.BlockSpec(memory_space=pl.ANY),
                      pl.BlockSpec((tm, D), row),
                      pl.BlockSpec((tm, LANES), row),
                      pl.BlockSpec((1, D), fixed),
                      pl.BlockSpec((1, D), fixed)],
            out_specs=[pl.BlockSpec((tm, D), row), pl.BlockSpec((tm, D), row)],
            scratch_shapes=[pltpu.VMEM((2, TOP_K, tm, D), F32),
                            pltpu.SemaphoreType.DMA((2,))]),
        out_shape=[jax.ShapeDtypeStruct((T, D), F32), jax.ShapeDtypeStruct((T, D), BF16)],
        compiler_params=_params(("arbitrary",)),
        name="moe_combine_ln",
    )(pos, y, x, wts, g.reshape(1, D), b.reshape(1, D))


def _dispatch_plan(ids, *, n_experts, tm, n_tiles):
    n_assign = ids.shape[0]
    experts = jnp.arange(n_experts, dtype=jnp.int32)
    onehot = (ids[:, None] == experts[None, :]).astype(jnp.int32)
    csum = jnp.cumsum(onehot, axis=0)
    rank = jnp.sum((csum - onehot) * onehot, axis=1)
    counts = csum[-1]
    tiles_per = (counts + tm - 1) // tm
    tile_end = jnp.cumsum(tiles_per)
    tile_start = tile_end - tiles_per
    n_used = tile_end[-1]
    pos = tile_start[ids] * tm + rank
    row_token = jnp.zeros((n_tiles * tm,), jnp.int32).at[pos].set(
        jnp.arange(n_assign, dtype=jnp.int32) // TOP_K)

    tile_ids = jnp.arange(n_tiles, dtype=jnp.int32)
    in_use = tile_ids < n_used
    tile_expert = jnp.sum((jnp.minimum(tile_ids, n_used - 1)[:, None] >= tile_end[None, :])
                          .astype(jnp.int32), axis=1)
    used = counts > 0
    ordinal = jnp.cumsum(used.astype(jnp.int32)) - 1
    later = lax.cummin(jnp.where(used, experts, n_experts), axis=0, reverse=True)
    next_used = jnp.concatenate([later[1:], jnp.full((1,), n_experts, jnp.int32)])
    next_used = jnp.where(next_used < n_experts, next_used, -1)
    plan = jnp.stack([
        tile_expert,
        jnp.logical_and(in_use, tile_ids == tile_start[tile_expert]).astype(jnp.int32),
        ordinal[tile_expert] % 2,
        next_used[tile_expert],
    ]).astype(jnp.int32)
    return pos.astype(jnp.int32), row_token, plan, n_used.reshape(1).astype(jnp.int32)


def hierarchical_moe_ln(x, w_group, b_group, w_expert, b_expert, w1, w3, w2, g, b, *,
                        layer, tm_route, tm_moe, tm_comb):
    T, D = x.shape
    E = w1.shape[1]
    n_route = N_GROUPS + E
    w_route = jnp.zeros((D, LANES), F32).at[:, :N_GROUPS].set(w_group).at[:, N_GROUPS:n_route].set(w_expert)
    b_route = jnp.zeros((1, LANES), F32).at[0, :N_GROUPS].set(b_group).at[0, N_GROUPS:n_route].set(b_expert)
    ids, wts = router(x, w_route, b_route, tm=tm_route)
    n_tiles = (T * TOP_K + E * (tm_moe - 1)) // tm_moe
    pos, row_token, plan, n_used = _dispatch_plan(
        ids[:, :TOP_K].reshape(-1), n_experts=E, tm=tm_moe, n_tiles=n_tiles)
    h = moe_up(x, w1, w3, plan, row_token, n_used, layer=layer, tm=tm_moe)
    y = moe_down(h, w2, plan, n_used, layer=layer, tm=tm_moe)
    return moe_combine_ln(y, pos, x, wts, g, b, tm=tm_comb)


def kernel(x, even_w_in, even_w_pool, even_pool_scale, even_b_f, even_w_out, odd_w_in, odd_conv_w, odd_conv_b, odd_w_a, odd_b_a, odd_w_x, odd_b_x, odd_lambda, odd_w_out, moe_w_group, moe_b_group, moe_w_expert, moe_b_expert, moe_w1, moe_w3, moe_w2, ln_g, ln_b):
    B, S, D = x.shape
    T = B * S
    xf = x.reshape(T, D)
    xb = xf.astype(BF16)
    tiles = _tile_plan(T, S, D)

    for layer in range(DEPTH):
        i = layer // 2
        if layer % 2 == 0:
            pool_w = even_w_pool.shape[1] * even_w_pool.shape[2]
            heads = even_b_f.shape[1]
            fox_w = heads * HEAD_DIM
            w_in = even_w_in[i]
            proj = matmul(xb, w_in, n_cols=pool_w + 3 * fox_w,
                          tm=tiles["mm_m"], tn=tiles["mm_n"], out_dtype=BF16)
            w_f = jnp.zeros((D, LANES), BF16).at[:, :heads].set(w_in[:, pool_w + 3 * fox_w:].astype(BF16))
            b_f = jnp.zeros((1, LANES), F32).at[0, :heads].set(even_b_f[i])
            c = forget_cumsum(xb, w_f, b_f, batch=B, tm=tiles["seq"])
            c_row = c.reshape(B, S, LANES)[:, :, :heads].transpose(0, 2, 1).reshape(B, heads, 1, S)
            a_out = pool_mixer(proj, even_w_pool[i].astype(BF16), even_pool_scale[i],
                               batch=B, ts=tiles["seq"])
            qc = pool_w // HEAD_DIM
            b_out = forgetting_attention(proj, c_row, batch=B, heads=heads, q_col=qc,
                                         k_col=qc + heads, v_col=qc + 2 * heads, tq=tiles["attn_q"])
            mixed = [a_out, b_out]
            w_out = even_w_out[i].astype(BF16)
        else:
            proj = matmul(xb, odd_w_in[i], n_cols=odd_w_in.shape[2], tm=tiles["mm_m"],
                          tn=tiles["mm_n"], out_dtype=BF16)
            mixed = [rglru_mixer(proj, odd_conv_w[i], odd_conv_b[i], odd_w_a[i].astype(BF16),
                                 odd_b_a[i].reshape(-1), odd_w_x[i].astype(BF16),
                                 odd_b_x[i].reshape(-1), odd_lambda[i], batch=B, ts=tiles["seq"])]
            w_out = odd_w_out[i].astype(BF16)
        xf = matmul_residual_ln(mixed, w_out, xf, ln_g[layer, 0], ln_b[layer, 0],
                                tm=tiles["ln_m"], tk=tiles["ln_k"])
        xf, xb = hierarchical_moe_ln(xf, moe_w_group[layer], moe_b_group[layer],
                                     moe_w_expert[layer], moe_b_expert[layer], moe_w1, moe_w3,
                                     moe_w2, ln_g[layer, 1], ln_b[layer, 1], layer=layer,
                                     tm_route=tiles["route"], tm_moe=tiles["moe"],
                                     tm_comb=tiles["comb"])
    return xf.reshape(B, S, D)


def _tile_plan(T, S, D):
    return {
        "mm_m": min(1024, T), "mm_n": min(512, D),
        "seq": min(512, S), "attn_q": min(512, S),
        "ln_m": min(512, T), "ln_k": min(512, D // 2),
        "route": min(512, T), "moe": 256, "comb": min(256, T),
    }
```

```python
import functools

import jax
import jax.numpy as jnp
from jax import lax
from jax.experimental import pallas as pl
from jax.experimental.pallas import tpu as pltpu

F32 = jnp.float32
BF16 = jnp.bfloat16

POOL_WINDOWS = (2, 4, 8, 16)
POOL_HALO = 16
HEAD_DIM = 128
LRU_BLOCKS = 16
CONV_WIDTH = 4
CONV_HALO = 8
LRU_C = 8.0
N_GROUPS = 4
EXPERTS_PER_GROUP = 8
TOP_K = 2
DEPTH = 2
DN_ALPHA = (2.0 * DEPTH) ** 0.25
LN_EPS = 1e-5

LANES = 128
SUBLANES = 8
VMEM_LIMIT_MB = 56


def _params(semantics, vmem_mb=VMEM_LIMIT_MB):
    return pltpu.CompilerParams(dimension_semantics=semantics,
                                vmem_limit_bytes=vmem_mb * 1024 * 1024)


def _mm_kernel(a_ref, w_ref, o_ref):
    o_ref[...] = jnp.dot(a_ref[...], w_ref[...].astype(BF16),
                         preferred_element_type=F32).astype(o_ref.dtype)


def matmul(a, w, *, n_cols, tm, tn, out_dtype):
    M, K = a.shape
    return pl.pallas_call(
        _mm_kernel,
        grid=(n_cols // tn, M // tm),
        in_specs=[pl.BlockSpec((tm, K), lambda j, i: (i, 0)),
                  pl.BlockSpec((K, tn), lambda j, i: (0, j))],
        out_specs=pl.BlockSpec((tm, tn), lambda j, i: (i, j)),
        out_shape=jax.ShapeDtypeStruct((M, n_cols), out_dtype),
        compiler_params=_params(("parallel", "parallel")),
        name="proj_matmul",
    )(a, w)


def _layer_norm_rows(z, g, b):
    mu = jnp.mean(z, axis=-1, keepdims=True)
    d = z - mu
    var = jnp.mean(d * d, axis=-1, keepdims=True)
    return d * lax.rsqrt(var + LN_EPS) * g + b


def _mm_ln_kernel(*refs, n_a, k_per):
    a_refs = refs[:n_a]
    w_ref, x_ref, g_ref, b_ref, o_ref = refs[n_a:]
    k = pl.program_id(1)

    for idx, a_ref in enumerate(a_refs):
        @pl.when(jnp.logical_and(k >= idx * k_per, k < (idx + 1) * k_per))
        def _(a_ref=a_ref, idx=idx):
            part = jnp.dot(a_ref[...], w_ref[...], preferred_element_type=F32)
            if idx == 0:
                @pl.when(k == 0)
                def _():
                    o_ref[...] = part

                @pl.when(k > 0)
                def _():
                    o_ref[...] += part
            else:
                o_ref[...] += part

    @pl.when(k == pl.num_programs(1) - 1)
    def _():
        o_ref[...] = _layer_norm_rows(DN_ALPHA * x_ref[...] + o_ref[...], g_ref[...], b_ref[...])


def matmul_residual_ln(a_list, w, x, g, b, *, tm, tk):
    M, K_each = a_list[0].shape
    N = w.shape[1]
    n_a = len(a_list)
    k_per = K_each // tk
    a_spec = lambda idx: pl.BlockSpec(
        (tm, tk), lambda i, k: (i, jnp.clip(k - idx * k_per, 0, k_per - 1)))
    return pl.pallas_call(
        functools.partial(_mm_ln_kernel, n_a=n_a, k_per=k_per),
        grid=(M // tm, n_a * k_per),
        in_specs=[a_spec(idx) for idx in range(n_a)] + [
            pl.BlockSpec((tk, N), lambda i, k: (k, 0)),
            pl.BlockSpec((tm, N), lambda i, k: (i, 0)),
            pl.BlockSpec((1, N), lambda i, k: (0, 0)),
            pl.BlockSpec((1, N), lambda i, k: (0, 0))],
        out_specs=pl.BlockSpec((tm, N), lambda i, k: (i, 0)),
        out_shape=jax.ShapeDtypeStruct((M, N), F32),
        compiler_params=_params(("parallel", "arbitrary")),
        name="out_proj_ln",
    )(*a_list, w, x, g.reshape(1, N), b.reshape(1, N))


def _cumsum_rows(v, n_rows):
    row = lax.broadcasted_iota(jnp.int32, v.shape, 0)
    shift = 1
    while shift < n_rows:
        v = v + jnp.where(row >= shift, pltpu.roll(v, shift, 0), 0.0)
        shift *= 2
    return v


def _forget_kernel(x_ref, w_ref, bf_ref, c_ref, carry_ref, *, tm):
    @pl.when(pl.program_id(1) == 0)
    def _():
        carry_ref[...] = jnp.zeros_like(carry_ref)

    z = jnp.dot(x_ref[...], w_ref[...].astype(BF16), preferred_element_type=F32) + bf_ref[...]
    log_f = jnp.minimum(z, 0.0) - jnp.log1p(jnp.exp(-jnp.abs(z)))
    c = _cumsum_rows(log_f, tm) + carry_ref[...]
    c_ref[...] = c
    carry_ref[...] = c[tm - 1:tm, :]


def forget_cumsum(xb, w_f, b_f, *, batch, tm):
    T, D = xb.shape
    steps = T // batch // tm
    return pl.pallas_call(
        functools.partial(_forget_kernel, tm=tm),
        grid=(batch, steps),
        in_specs=[pl.BlockSpec((tm, D), lambda b, s: (b * steps + s, 0)),
                  pl.BlockSpec((D, LANES), lambda b, s: (0, 0)),
                  pl.BlockSpec((1, LANES), lambda b, s: (0, 0))],
        out_specs=pl.BlockSpec((tm, LANES), lambda b, s: (b * steps + s, 0)),
        out_shape=jax.ShapeDtypeStruct((T, LANES), F32),
        scratch_shapes=[pltpu.VMEM((1, LANES), F32)],
        compiler_params=_params(("parallel", "arbitrary")),
        name="forget_cumsum",
    )(xb, w_f, b_f)


def _pool_kernel(u_ref, w_ref, sc_ref, o_ref, ext_ref, *, ts):
    g = pl.program_id(1)
    si = pl.program_id(2)

    @pl.when(si == 0)
    def _():
        ext_ref[0:POOL_HALO, :] = jnp.zeros((POOL_HALO, ext_ref.shape[1]), F32)

    @pl.when(si > 0)
    def _():
        ext_ref[0:POOL_HALO, :] = ext_ref[ts:ts + POOL_HALO, :]

    u = u_ref[...].astype(F32)
    ext_ref[POOL_HALO:, :] = u
    e1 = ext_ref[...]
    e2 = e1 + pltpu.roll(e1, 1, 0)
    e4 = e2 + pltpu.roll(e2, 2, 0)
    e8 = e4 + pltpu.roll(e4, 4, 0)
    e16 = e8 + pltpu.roll(e8, 8, 0)
    win = jnp.where(g == 0, e2, jnp.where(g == 1, e4, jnp.where(g == 2, e8, e16)))
    win = win[POOL_HALO:, :]
    width = jnp.left_shift(2, g)
    t = si * ts + lax.broadcasted_iota(jnp.int32, (ts, 1), 0)
    cnt = jnp.minimum(t + 1, width).astype(F32)
    pooled = win / cnt - u
    y = jnp.dot(pooled.astype(BF16), w_ref[...], preferred_element_type=F32)
    o_ref[...] = (y * sc_ref[...]).astype(o_ref.dtype)


def pool_mixer(proj, w_pool_b, pool_scale, *, batch, ts):
    T = proj.shape[0]
    G, C, _ = w_pool_b.shape
    assert POOL_WINDOWS == (2, 4, 8, 16) and G == len(POOL_WINDOWS)
    steps = T // batch // ts
    return pl.pallas_call(
        functools.partial(_pool_kernel, ts=ts),
        grid=(batch, G, steps),
        in_specs=[pl.BlockSpec((ts, C), lambda b, g, s: (b * steps + s, g)),
                  pl.BlockSpec((None, C, C), lambda b, g, s: (g, 0, 0)),
                  pl.BlockSpec((1, C), lambda b, g, s: (0, g))],
        out_specs=pl.BlockSpec((ts, C), lambda b, g, s: (b * steps + s, g)),
        out_shape=jax.ShapeDtypeStruct((T, G * C), BF16),
        scratch_shapes=[pltpu.VMEM((ts + POOL_HALO, C), F32)],
        compiler_params=_params(("parallel", "parallel", "arbitrary")),
        name="pool_mixer",
    )(proj, w_pool_b, pool_scale.reshape(1, G * C))


LOG2E = 1.4426950408889634
N_BIAS_PARTS = 3


def _attn_kernel(q_ref, k_ref, v_ref, c_ref, o_ref, kaug_ref, vaug_ref, qext_ref,
                 qaug_ref, sa_ref, sb_ref, m_ref, acc_ref, *, tq, tk, scale, head_lane0):
    h = pl.program_id(1)
    qi = pl.program_id(2)
    S = k_ref.shape[0]

    @pl.when(qi == 0)
    def _():
        lane = lax.broadcasted_iota(jnp.int32, (S, LANES), 1)
        c = jnp.sum(jnp.where(lane == head_lane0 + h, c_ref[...], 0.0), axis=1,
                    keepdims=True) * LOG2E
        hi = c.astype(BF16).astype(F32)
        mid = (c - hi).astype(BF16).astype(F32)
        lo = c - hi - mid
        ones = jnp.where(lane < N_BIAS_PARTS, 1.0, 0.0)
        k_ext = jnp.where(lane == 3, -hi, jnp.where(lane == 4, -mid, jnp.where(lane == 5, -lo, ones)))
        q_ext = jnp.where(lane == 0, hi, jnp.where(lane == 1, mid, jnp.where(
            lane == 2, lo, jnp.where(lane < 2 * N_BIAS_PARTS, 1.0, 0.0))))
        kaug_ref[:, :HEAD_DIM] = k_ref[...]
        kaug_ref[:, HEAD_DIM:] = k_ext.astype(BF16)
        vaug_ref[:, :HEAD_DIM] = v_ref[...]
        vaug_ref[:, HEAD_DIM:] = jnp.where(lane == 0, 1.0, 0.0).astype(BF16)
        qext_ref[...] = q_ext.astype(BF16)

    q0 = pl.multiple_of(qi * tq, tq)
    qaug_ref[:, :HEAD_DIM] = (q_ref[...].astype(F32) * (scale * LOG2E)).astype(BF16)
    qaug_ref[:, HEAD_DIM:] = qext_ref[pl.ds(q0, tq), :]
    m_ref[...] = jnp.full(m_ref.shape, -jnp.inf, F32)
    acc_ref[...] = jnp.zeros(acc_ref.shape, F32)

    def scores(block, lo=0):
        k0 = pl.multiple_of(block * tk, tk)
        return lax.dot_general(qaug_ref[lo:, :], kaug_ref[pl.ds(k0, tk), :],
                               (((1,), (1,)), ((), ())), preferred_element_type=F32)

    def softmax_pv(s, block, lo=0):
        k0 = pl.multiple_of(block * tk, tk)
        m = m_ref[lo:, :]
        m_new = jnp.maximum(m, jnp.max(s, axis=1, keepdims=True))
        alpha = jnp.exp2(m - m_new)
        p = jnp.exp2((s - m_new).astype(BF16))
        acc_ref[lo:, :] = alpha * acc_ref[lo:, :] + jnp.dot(
            p, vaug_ref[pl.ds(k0, tk), :], preferred_element_type=F32)
        m_ref[lo:, :] = m_new

    n_full = qi * (tq // tk)
    sa_ref[...] = scores(0)

    def body(i, carry):
        j = 2 * i
        sb_ref[...] = scores(j + 1)
        softmax_pv(sa_ref[...], j)
        sa_ref[...] = scores(j + 2)
        softmax_pv(sb_ref[...], j + 1)
        return carry

    lax.fori_loop(0, n_full // 2, body, 0)

    for u in range(tq // tk):
        lo = u * tk
        s = sa_ref[...] if u == 0 else scores(n_full + u, lo)
        row = lax.broadcasted_iota(jnp.int32, s.shape, 0)
        col = lax.broadcasted_iota(jnp.int32, s.shape, 1)
        softmax_pv(jnp.where(col <= row, s, -jnp.inf), n_full + u, lo)
        acc = acc_ref[lo:lo + tk, :]
        o_ref[lo:lo + tk, :] = (acc[:, :HEAD_DIM] / acc[:, HEAD_DIM:HEAD_DIM + 1]
                                ).astype(o_ref.dtype)


def forgetting_attention(proj, c, *, batch, heads, head_lane0, q_col, k_col, v_col, tq, tk):
    T = proj.shape[0]
    S = T // batch
    nq = S // tq
    assert (tq // tk) % 2 == 0, "full key blocks are consumed two per loop trip"
    return pl.pallas_call(
        functools.partial(_attn_kernel, tq=tq, tk=tk, scale=HEAD_DIM ** -0.5,
                          head_lane0=head_lane0),
        grid=(batch, heads, nq),
        in_specs=[pl.BlockSpec((tq, HEAD_DIM), lambda b, h, i: (b * nq + i, q_col + h)),
                  pl.BlockSpec((S, HEAD_DIM), lambda b, h, i: (b, k_col + h)),
                  pl.BlockSpec((S, HEAD_DIM), lambda b, h, i: (b, v_col + h)),
                  pl.BlockSpec((S, LANES), lambda b, h, i: (b, 0))],
        out_specs=pl.BlockSpec((tq, HEAD_DIM), lambda b, h, i: (b * nq + i, h)),
        out_shape=jax.ShapeDtypeStruct((T, heads * HEAD_DIM), BF16),
        scratch_shapes=[pltpu.VMEM((S, 2 * HEAD_DIM), BF16), pltpu.VMEM((S, 2 * HEAD_DIM), BF16),
                        pltpu.VMEM((S, LANES), BF16), pltpu.VMEM((tq, 2 * HEAD_DIM), BF16),
                        pltpu.VMEM((tq, tk), F32), pltpu.VMEM((tq, tk), F32),
                        pltpu.VMEM((tq, 1), F32), pltpu.VMEM((tq, 2 * HEAD_DIM), F32)],
        compiler_params=_params(("parallel", "parallel", "arbitrary")),
        name="forgetting_attention",
    )(proj, proj, proj, c)


def _rglru_kernel(xb_ref, gate_ref, cw_ref, cb_ref, wa_ref, ba_ref, wx_ref, bx_ref, lam_ref,
                  y_ref, ext_ref, a_ref, b_ref, h_ref, carry_ref, *, ts):
    si = pl.program_id(2)
    C = xb_ref.shape[1]

    @pl.when(si == 0)
    def _():
        ext_ref[0:CONV_HALO, :] = jnp.zeros((CONV_HALO, C), F32)
        carry_ref[...] = jnp.zeros_like(carry_ref)

    @pl.when(si > 0)
    def _():
        ext_ref[0:CONV_HALO, :] = ext_ref[ts:ts + CONV_HALO, :]

    xb = xb_ref[...].astype(F32)
    ext_ref[CONV_HALO:, :] = xb
    cw = cw_ref[...]
    xc = cb_ref[...] + xb * cw[CONV_WIDTH - 1:CONV_WIDTH, :]
    for lag in range(1, CONV_WIDTH):
        tap = CONV_WIDTH - 1 - lag
        xc = xc + ext_ref[CONV_HALO - lag:CONV_HALO - lag + ts, :] * cw[tap:tap + 1, :]

    xcb = xc.astype(BF16)
    r = jax.nn.sigmoid(jnp.dot(xcb, wa_ref[...], preferred_element_type=F32) + ba_ref[...])
    ig = jax.nn.sigmoid(jnp.dot(xcb, wx_ref[...], preferred_element_type=F32) + bx_ref[...])
    lam = lam_ref[...]
    log_a_base = -(jnp.maximum(-lam, 0.0) + jnp.log1p(jnp.exp(-jnp.abs(lam))))
    log_a = LRU_C * r * log_a_base
    a = jnp.exp(log_a)
    mult = jnp.sqrt(-jnp.tanh(log_a) * (1.0 + a * a))
    b = mult * ig * xc

    groups = ts // SUBLANES
    a3 = a.reshape(groups, SUBLANES, C)
    b3 = b.reshape(groups, SUBLANES, C)
    sub = lax.broadcasted_iota(jnp.int32, a3.shape, 1)
    shift = 1
    while shift < SUBLANES:
        keep = sub >= shift
        b3 = jnp.where(keep, a3 * pltpu.roll(b3, shift, 1) + b3, b3)
        a3 = jnp.where(keep, a3 * pltpu.roll(a3, shift, 1), a3)
        shift *= 2
    a_ref[...] = a3.reshape(ts, C)
    b_ref[...] = b3.reshape(ts, C)

    def body(gi, carry):
        r0 = pl.multiple_of(gi * SUBLANES, SUBLANES)
        hg = a_ref[pl.ds(r0, SUBLANES), :] * carry + b_ref[pl.ds(r0, SUBLANES), :]
        h_ref[pl.ds(r0, SUBLANES), :] = hg
        return hg[SUBLANES - 1:SUBLANES, :]

    carry_ref[...] = lax.fori_loop(0, groups, body, carry_ref[...], unroll=8)

    gate = gate_ref[...].astype(F32)
    gelu = 0.5 * gate * (1.0 + jnp.tanh(0.7978845608028654 * (gate + 0.044715 * gate * gate * gate)))
    y_ref[...] = (h_ref[...] * gelu).astype(y_ref.dtype)


def rglru_mixer(proj, conv_w, conv_b, w_a_b, b_a, w_x_b, b_x, lam, *, batch, ts):
    T = proj.shape[0]
    NB, C, _ = w_a_b.shape
    W = NB * C
    steps = T // batch // ts
    vec = lambda: pl.BlockSpec((1, C), lambda b, c, s: (0, c))
    blk = lambda: pl.BlockSpec((None, C, C), lambda b, c, s: (c, 0, 0))
    return pl.pallas_call(
        functools.partial(_rglru_kernel, ts=ts),
        grid=(batch, NB, steps),
        in_specs=[pl.BlockSpec((ts, C), lambda b, c, s: (b * steps + s, NB + c)),
                  pl.BlockSpec((ts, C), lambda b, c, s: (b * steps + s, c)),
                  pl.BlockSpec((CONV_WIDTH, C), lambda b, c, s: (0, c)),
                  vec(), blk(), vec(), blk(), vec(), vec()],
        out_specs=pl.BlockSpec((ts, C), lambda b, c, s: (b * steps + s, c)),
        out_shape=jax.ShapeDtypeStruct((T, W), BF16),
        scratch_shapes=[pltpu.VMEM((ts + CONV_HALO, C), F32),
                        pltpu.VMEM((ts, C), F32), pltpu.VMEM((ts, C), F32),
                        pltpu.VMEM((ts, C), F32), pltpu.VMEM((1, C), F32)],
        compiler_params=_params(("parallel", "parallel", "arbitrary")),
        name="rglru_mixer",
    )(proj, proj, conv_w, conv_b.reshape(1, W), w_a_b, b_a.reshape(1, W), w_x_b,
      b_x.reshape(1, W), lam.reshape(1, W))


def _router_kernel(x_ref, w_ref, b_ref, id_ref, wt_ref):
    logits = jnp.dot(x_ref[...], w_ref[...], preferred_element_type=F32,
                     precision=lax.Precision.HIGHEST) + b_ref[...]
    lane = lax.broadcasted_iota(jnp.int32, logits.shape, 1).astype(F32)
    neg = -jnp.inf
    big = float(LANES)

    is_group = lane < N_GROUPS
    gl = jnp.where(is_group, logits, neg)
    gmax = jnp.max(gl, axis=1, keepdims=True)
    g_idx = jnp.min(jnp.where(gl == gmax, lane, big), axis=1, keepdims=True)
    g_w = 1.0 / jnp.sum(jnp.where(is_group, jnp.exp(gl - gmax), 0.0), axis=1, keepdims=True)

    lo = N_GROUPS + g_idx * EXPERTS_PER_GROUP
    el = jnp.where(lane >= lo, jnp.where(lane < lo + EXPERTS_PER_GROUP, logits, neg), neg)
    v1 = jnp.max(el, axis=1, keepdims=True)
    i1 = jnp.min(jnp.where(el == v1, lane, big), axis=1, keepdims=True)
    el2 = jnp.where(lane == i1, neg, el)
    v2 = jnp.max(el2, axis=1, keepdims=True)
    i2 = jnp.min(jnp.where(el2 == v2, lane, big), axis=1, keepdims=True)
    t = jnp.exp(v2 - v1)
    w1 = g_w / (1.0 + t)
    w2 = w1 * t
    ids = jnp.where(lane == 0, i1 - N_GROUPS, jnp.where(lane == 1, i2 - N_GROUPS, 0.0))
    id_ref[...] = ids.astype(jnp.int32)
    wt_ref[...] = jnp.where(lane == 0, w1, jnp.where(lane == 1, w2, 0.0))


def router(x, w_route, b_route, *, tm):
    T, D = x.shape
    return pl.pallas_call(
        _router_kernel,
        grid=(T // tm,),
        in_specs=[pl.BlockSpec((tm, D), lambda i: (i, 0)),
                  pl.BlockSpec((D, LANES), lambda i: (0, 0)),
                  pl.BlockSpec((1, LANES), lambda i: (0, 0))],
        out_specs=[pl.BlockSpec((tm, LANES), lambda i: (i, 0)),
                   pl.BlockSpec((tm, LANES), lambda i: (i, 0))],
        out_shape=[jax.ShapeDtypeStruct((T, LANES), jnp.int32),
                   jax.ShapeDtypeStruct((T, LANES), F32)],
        compiler_params=_params(("parallel",)),
        name="router",
    )(x, w_route, b_route)


def _row_gather_start(src_hbm, dst_ref, sem, row_of, n_rows):
    def body(r, carry):
        pltpu.make_async_copy(src_hbm.at[pl.ds(row_of(r), 1), :],
                              dst_ref.at[pl.ds(r, 1), :], sem).start()
        return carry
    lax.fori_loop(0, n_rows, body, 0, unroll=8)


def _row_gather_wait(src_hbm, dst_ref, sem, n_rows):
    pltpu.make_async_copy(src_hbm.at[pl.ds(0, n_rows), :], dst_ref, sem).wait()


PLAN_EXPERT, PLAN_FIRST, PLAN_SLOT, PLAN_NEXT = range(4)


def _expert_weight_copies(w_hbm_list, buf_list, sem, layer, expert, slot):
    return [pltpu.make_async_copy(w.at[layer, expert], buf.at[slot], sem.at[slot])
            for w, buf in zip(w_hbm_list, buf_list)]


def _expert_weight_pipeline(plan_ref, n_used, w_hbm_list, buf_list, sem, layer):
    j = pl.program_id(0)
    slot = plan_ref[PLAN_SLOT, j]
    first = jnp.logical_and(j < n_used, plan_ref[PLAN_FIRST, j] == 1)

    @pl.when(j == 0)
    def _():
        for cp in _expert_weight_copies(w_hbm_list, buf_list, sem, layer,
                                        plan_ref[PLAN_EXPERT, 0], 0):
            cp.start()

    @pl.when(jnp.logical_and(first, plan_ref[PLAN_NEXT, j] >= 0))
    def _():
        for cp in _expert_weight_copies(w_hbm_list, buf_list, sem, layer,
                                        plan_ref[PLAN_NEXT, j], 1 - slot):
            cp.start()

    @pl.when(first)
    def _():
        for cp in _expert_weight_copies(w_hbm_list, buf_list, sem, layer,
                                        plan_ref[PLAN_EXPERT, j], slot):
            cp.wait()

    return slot


def _moe_up_kernel(plan_ref, rt_ref, nu_ref, x_hbm, w1_hbm, w3_hbm, h_ref,
                   xg_ref, w1_buf, w3_buf, xsem, wsem, *, tm, layer):
    j = pl.program_id(0)
    n_used = nu_ref[0]

    def gather(tile, slot):
        _row_gather_start(x_hbm, xg_ref.at[slot], xsem.at[slot],
                          lambda r: rt_ref[tile * tm + r], tm)

    @pl.when(j == 0)
    def _():
        gather(0, 0)

    @pl.when(j + 1 < n_used)
    def _():
        gather(j + 1, (j + 1) % 2)

    wslot = _expert_weight_pipeline(plan_ref, n_used, (w1_hbm, w3_hbm), (w1_buf, w3_buf),
                                    wsem, layer)

    @pl.when(j < n_used)
    def _():
        slot = j % 2
        _row_gather_wait(x_hbm, xg_ref.at[slot], xsem.at[slot], tm)
        x = xg_ref[slot].astype(BF16)
        a = jnp.dot(x, w1_buf[wslot].astype(BF16), preferred_element_type=F32)
        b = jnp.dot(x, w3_buf[wslot].astype(BF16), preferred_element_type=F32)
        h_ref[...] = (a * jax.nn.sigmoid(a) * b).astype(h_ref.dtype)

    @pl.when(j >= n_used)
    def _():
        h_ref[...] = jnp.zeros_like(h_ref)


def moe_up(x, w1, w3, plan, row_token, n_used, *, layer, tm):
    D = x.shape[1]
    FF = w1.shape[-1]
    n_tiles = plan.shape[1]
    any_spec = lambda: pl.BlockSpec(memory_space=pl.ANY)
    return pl.pallas_call(
        functools.partial(_moe_up_kernel, tm=tm, layer=layer),
        grid_spec=pltpu.PrefetchScalarGridSpec(
            num_scalar_prefetch=3,
            grid=(n_tiles,),
            in_specs=[any_spec(), any_spec(), any_spec()],
            out_specs=pl.BlockSpec((tm, FF), lambda j, plan, rt, nu: (j, 0)),
            scratch_shapes=[pltpu.VMEM((2, tm, D), F32),
                            pltpu.VMEM((2, D, FF), F32), pltpu.VMEM((2, D, FF), F32),
                            pltpu.SemaphoreType.DMA((2,)), pltpu.SemaphoreType.DMA((2,))]),
        out_shape=jax.ShapeDtypeStruct((n_tiles * tm, FF), BF16),
        compiler_params=_params(("arbitrary",)),
        name="moe_up",
    )(plan, row_token, n_used, x, w1, w3)


def _moe_down_kernel(plan_ref, nu_ref, h_ref, w2_hbm, y_ref, w2_buf, wsem, *, layer):
    j = pl.program_id(0)
    n_used = nu_ref[0]
    wslot = _expert_weight_pipeline(plan_ref, n_used, (w2_hbm,), (w2_buf,), wsem, layer)

    @pl.when(j < n_used)
    def _():
        y_ref[...] = jnp.dot(h_ref[...], w2_buf[wslot].astype(BF16),
                             preferred_element_type=F32)

    @pl.when(j >= n_used)
    def _():
        y_ref[...] = jnp.zeros_like(y_ref)


def moe_down(h, w2, plan, n_used, *, layer, tm):
    FF, D = w2.shape[-2:]
    n_tiles = plan.shape[1]
    tile = lambda j, plan, nu: (j, 0)
    return pl.pallas_call(
        functools.partial(_moe_down_kernel, layer=layer),
        grid_spec=pltpu.PrefetchScalarGridSpec(
            num_scalar_prefetch=2,
            grid=(n_tiles,),
            in_specs=[pl.BlockSpec((tm, FF), tile), pl.BlockSpec(memory_space=pl.ANY)],
            out_specs=pl.BlockSpec((tm, D), tile),
            scratch_shapes=[pltpu.VMEM((2, FF, D), F32), pltpu.SemaphoreType.DMA((2,))]),
        out_shape=jax.ShapeDtypeStruct((n_tiles * tm, D), F32),
        compiler_params=_params(("arbitrary",)),
        name="moe_down",
    )(plan, n_used, h, w2)


def _combine_ln_kernel(pos_ref, y_hbm, x_ref, wt_ref, g_ref, b_ref, o_ref, ob_ref,
                       yg_ref, sem, *, tm):
    i = pl.program_id(0)

    def gather(tile, slot):
        for k in range(TOP_K):
            _row_gather_start(y_hbm, yg_ref.at[slot, k], sem.at[slot],
                              lambda r: pos_ref[TOP_K * (tile * tm + r) + k], tm)

    @pl.when(i == 0)
    def _():
        gather(0, 0)

    @pl.when(i + 1 < pl.num_programs(0))
    def _():
        gather(i + 1, (i + 1) % 2)

    slot = i % 2
    for k in range(TOP_K):
        _row_gather_wait(y_hbm, yg_ref.at[slot, k], sem.at[slot], tm)
    wt = wt_ref[...]
    z = DN_ALPHA * x_ref[...]
    for k in range(TOP_K):
        z = z + wt[:, k:k + 1] * yg_ref[slot, k]
    y = _layer_norm_rows(z, g_ref[...], b_ref[...])
    o_ref[...] = y
    ob_ref[...] = y.astype(BF16)


def moe_combine_ln(y, pos, x, wts, g, b, *, tm):
    T, D = x.shape
    row = lambda i, pos: (i, 0)
    fixed = lambda i, pos: (0, 0)
    return pl.pallas_call(
        functools.partial(_combine_ln_kernel, tm=tm),
        grid_spec=pltpu.PrefetchScalarGridSpec(
            num_scalar_prefetch=1,
            grid=(T // tm,),
            in_specs=[pl.BlockSpec(memory_space=pl.ANY),
                      pl.BlockSpec((tm, D), row),
                      pl.BlockSpec((tm, LANES), row),
                      pl.BlockSpec((1, D), fixed),
                      pl.BlockSpec((1, D), fixed)],
            out_specs=[pl.BlockSpec((tm, D), row), pl.BlockSpec((tm, D), row)],
            scratch_shapes=[pltpu.VMEM((2, TOP_K, tm, D), F32),
                            pltpu.SemaphoreType.DMA((2,))]),
        out_shape=[jax.ShapeDtypeStruct((T, D), F32), jax.ShapeDtypeStruct((T, D), BF16)],
        compiler_params=_params(("arbitrary",)),
        name="moe_combine_ln",
    )(pos, y, x, wts, g.reshape(1, D), b.reshape(1, D))


def _dispatch_plan(ids, *, n_experts, tm, n_tiles):
    n_assign = ids.shape[0]
    experts = jnp.arange(n_experts, dtype=jnp.int32)
    onehot = (ids[:, None] == experts[None, :]).astype(jnp.int32)
    csum = jnp.cumsum(onehot, axis=0)
    rank = jnp.sum((csum - onehot) * onehot, axis=1)
    counts = csum[-1]
    tiles_per = (counts + tm - 1) // tm
    tile_end = jnp.cumsum(tiles_per)
    tile_start = tile_end - tiles_per
    n_used = tile_end[-1]
    pos = tile_start[ids] * tm + rank
    row_token = jnp.zeros((n_tiles * tm,), jnp.int32).at[pos].set(
        jnp.arange(n_assign, dtype=jnp.int32) // TOP_K)

    tile_ids = jnp.arange(n_tiles, dtype=jnp.int32)
    in_use = tile_ids < n_used
    tile_expert = jnp.sum((jnp.minimum(tile_ids, n_used - 1)[:, None] >= tile_end[None, :])
                          .astype(jnp.int32), axis=1)
    used = counts > 0
    ordinal = jnp.cumsum(used.astype(jnp.int32)) - 1
    later = lax.cummin(jnp.where(used, experts, n_experts), axis=0, reverse=True)
    next_used = jnp.concatenate([later[1:], jnp.full((1,), n_experts, jnp.int32)])
    next_used = jnp.where(next_used < n_experts, next_used, -1)
    plan = jnp.stack([
        tile_expert,
        jnp.logical_and(in_use, tile_ids == tile_start[tile_expert]).astype(jnp.int32),
        ordinal[tile_expert] % 2,
        next_used[tile_expert],
    ]).astype(jnp.int32)
    return pos.astype(jnp.int32), row_token, plan, n_used.reshape(1).astype(jnp.int32)


def hierarchical_moe_ln(x, w_group, b_group, w_expert, b_expert, w1, w3, w2, g, b, *,
                        layer, tm_route, tm_moe, tm_comb):
    T, D = x.shape
    E = w1.shape[1]
    n_route = N_GROUPS + E
    w_route = jnp.zeros((D, LANES), F32).at[:, :N_GROUPS].set(w_group).at[:, N_GROUPS:n_route].set(w_expert)
    b_route = jnp.zeros((1, LANES), F32).at[0, :N_GROUPS].set(b_group).at[0, N_GROUPS:n_route].set(b_expert)
    ids, wts = router(x, w_route, b_route, tm=tm_route)
    n_tiles = (T * TOP_K + E * (tm_moe - 1)) // tm_moe
    pos, row_token, plan, n_used = _dispatch_plan(
        ids[:, :TOP_K].reshape(-1), n_experts=E, tm=tm_moe, n_tiles=n_tiles)
    h = moe_up(x, w1, w3, plan, row_token, n_used, layer=layer, tm=tm_moe)
    y = moe_down(h, w2, plan, n_used, layer=layer, tm=tm_moe)
    return moe_combine_ln(y, pos, x, wts, g, b, tm=tm_comb)


def kernel(x, even_w_in, even_w_pool, even_pool_scale, even_b_f, even_w_out, odd_w_in, odd_conv_w, odd_conv_b, odd_w_a, odd_b_a, odd_w_x, odd_b_x, odd_lambda, odd_w_out, moe_w_group, moe_b_group, moe_w_expert, moe_b_expert, moe_w1, moe_w3, moe_w2, ln_g, ln_b):
    B, S, D = x.shape
    T = B * S
    xf = x.reshape(T, D)
    xb = xf.astype(BF16)
    tiles = _tile_plan(T, S, D)

    for layer in range(DEPTH):
        i = layer // 2
        if layer % 2 == 0:
            pool_w = even_w_pool.shape[1] * even_w_pool.shape[2]
            heads = even_b_f.shape[1]
            fox_w = heads * HEAD_DIM
            w_in = even_w_in[i]
            proj = matmul(xb, w_in, n_cols=pool_w + 3 * fox_w,
                          tm=tiles["mm_m"], tn=tiles["mm_n"], out_dtype=BF16)
            lane0 = LANES - heads
            w_f = w_in[:, w_in.shape[1] - LANES:]
            b_f = jnp.zeros((1, LANES), F32).at[0, lane0:].set(even_b_f[i])
            c = forget_cumsum(xb, w_f, b_f, batch=B, tm=tiles["seq"])
            a_out = pool_mixer(proj, even_w_pool[i].astype(BF16), even_pool_scale[i],
                               batch=B, ts=tiles["seq"])
            qc = pool_w // HEAD_DIM
            b_out = forgetting_attention(proj, c, batch=B, heads=heads, head_lane0=lane0,
                                         q_col=qc, k_col=qc + heads, v_col=qc + 2 * heads,
                                         tq=tiles["attn_q"], tk=tiles["attn_k"])
            mixed = [a_out, b_out]
            w_out = even_w_out[i].astype(BF16)
        else:
            proj = matmul(xb, odd_w_in[i], n_cols=odd_w_in.shape[2], tm=tiles["mm_m"],
                          tn=tiles["mm_n"], out_dtype=BF16)
            mixed = [rglru_mixer(proj, odd_conv_w[i], odd_conv_b[i], odd_w_a[i].astype(BF16),
                                 odd_b_a[i].reshape(-1), odd_w_x[i].astype(BF16),
                                 odd_b_x[i].reshape(-1), odd_lambda[i], batch=B, ts=tiles["seq"])]
            w_out = odd_w_out[i].astype(BF16)
        xf = matmul_residual_ln(mixed, w_out, xf, ln_g[layer, 0], ln_b[layer, 0],
                                tm=tiles["ln_m"], tk=tiles["ln_k"])
        xf, xb = hierarchical_moe_ln(xf, moe_w_group[layer], moe_b_group[layer],
                                     moe_w_expert[layer], moe_b_expert[layer], moe_w1, moe_w3,
                                     moe_w2, ln_g[layer, 1], ln_b[layer, 1], layer=layer,
                                     tm_route=tiles["route"], tm_moe=tiles["moe"],
                                     tm_comb=tiles["comb"])
    return xf.reshape(B, S, D)


def _tile_plan(T, S, D):
    return {
        "mm_m": min(1024, T), "mm_n": min(512, D),
        "seq": min(512, S), "attn_q": min(1024, S), "attn_k": min(512, S),
        "ln_m": min(512, T), "ln_k": min(512, D // 2),
        "route": min(512, T), "moe": 256, "comb": min(256, T),
    }
```

```python
import functools

import jax
import jax.numpy as jnp
from jax import lax
from jax.experimental import pallas as pl
from jax.experimental.pallas import tpu as pltpu

F32 = jnp.float32
BF16 = jnp.bfloat16

POOL_WINDOWS = (2, 4, 8, 16)
POOL_HALO = 16
HEAD_DIM = 128
LRU_BLOCKS = 16
CONV_WIDTH = 4
CONV_HALO = 8
LRU_C = 8.0
N_GROUPS = 4
EXPERTS_PER_GROUP = 8
TOP_K = 2
DEPTH = 2
DN_ALPHA = (2.0 * DEPTH) ** 0.25
LN_EPS = 1e-5

LANES = 128
SUBLANES = 8
VMEM_LIMIT_MB = 56
OUT_PROJ_VMEM_MB = 60


def _params(semantics, vmem_mb=VMEM_LIMIT_MB):
    return pltpu.CompilerParams(dimension_semantics=semantics,
                                vmem_limit_bytes=vmem_mb * 1024 * 1024)


_CONTRACT_LAST = (((1,), (1,)), ((), ()))


def _mm_kernel(a_ref, w_ref, o_ref, *, w_is_transposed):
    w = w_ref[...].astype(BF16)
    if w_is_transposed:
        out = lax.dot_general(a_ref[...], w, _CONTRACT_LAST, preferred_element_type=F32)
    else:
        out = jnp.dot(a_ref[...], w, preferred_element_type=F32)
    o_ref[...] = out.astype(o_ref.dtype)


def matmul(a, w, *, n_cols, tm, tn, out_dtype, w_is_transposed=False):
    M, K = a.shape
    if w_is_transposed:
        w_spec = pl.BlockSpec((tn, K), lambda j, i: (j, 0))
    else:
        w_spec = pl.BlockSpec((K, tn), lambda j, i: (0, j))
    return pl.pallas_call(
        functools.partial(_mm_kernel, w_is_transposed=w_is_transposed),
        grid=(n_cols // tn, M // tm),
        in_specs=[pl.BlockSpec((tm, K), lambda j, i: (i, 0)), w_spec],
        out_specs=pl.BlockSpec((tm, tn), lambda j, i: (i, j)),
        out_shape=jax.ShapeDtypeStruct((M, n_cols), out_dtype),
        compiler_params=_params(("parallel", "parallel")),
        name="proj_matmul",
    )(a, w)


def _layer_norm_rows(z, g, b):
    mu = jnp.mean(z, axis=-1, keepdims=True)
    d = z - mu
    var = jnp.mean(d * d, axis=-1, keepdims=True)
    return d * lax.rsqrt(var + LN_EPS) * g + b


LN_CHUNK_ROWS = 64


def _mm_ln_kernel(*refs, n_a):
    a_refs = refs[:n_a]
    w_hbm, x_hbm, g_ref, b_ref, o_ref, w_buf, acc_ref, sem = refs[n_a:]
    i = pl.program_id(0)
    tm = o_ref.shape[0]

    @pl.when(i == 0)
    def _():
        w_copy = pltpu.make_async_copy(w_hbm, w_buf, sem.at[0])
        w_copy.start()
        w_copy.wait()

    x_copy = pltpu.make_async_copy(x_hbm.at[pl.ds(pl.multiple_of(i * tm, tm), tm), :], o_ref,
                                   sem.at[1])
    x_copy.start()
    k_each = a_refs[0].shape[1]
    for idx, a_ref in enumerate(a_refs):
        part = jnp.dot(a_ref[...], w_buf[idx * k_each:(idx + 1) * k_each, :],
                       preferred_element_type=F32)
        if idx == 0:
            acc_ref[...] = part
        else:
            acc_ref[...] += part
    x_copy.wait()

    def normalise(c, carry):
        rows = pl.ds(pl.multiple_of(c * LN_CHUNK_ROWS, LN_CHUNK_ROWS), LN_CHUNK_ROWS)
        o_ref[rows, :] = _layer_norm_rows(DN_ALPHA * o_ref[rows, :] + acc_ref[rows, :],
                                          g_ref[...], b_ref[...])
        return carry

    lax.fori_loop(0, tm // LN_CHUNK_ROWS, normalise, 0)


def matmul_residual_ln(a_list, w, x, g, b, *, tm):
    M, K_each = a_list[0].shape
    K, N = w.shape
    n_a = len(a_list)
    assert n_a * K_each == K and tm % LN_CHUNK_ROWS == 0
    return pl.pallas_call(
        functools.partial(_mm_ln_kernel, n_a=n_a),
        grid=(M // tm,),
        in_specs=[pl.BlockSpec((tm, K_each), lambda i: (i, 0)) for _ in range(n_a)] + [
            pl.BlockSpec(memory_space=pl.ANY),
            pl.BlockSpec(memory_space=pl.ANY),
            pl.BlockSpec((1, N), lambda i: (0, 0)),
            pl.BlockSpec((1, N), lambda i: (0, 0))],
        out_specs=pl.BlockSpec((tm, N), lambda i: (i, 0)),
        out_shape=jax.ShapeDtypeStruct((M, N), F32),
        scratch_shapes=[pltpu.VMEM((K, N), BF16), pltpu.VMEM((tm, N), F32),
                        pltpu.SemaphoreType.DMA((2,))],
        compiler_params=_params(("arbitrary",), vmem_mb=OUT_PROJ_VMEM_MB),
        name="out_proj_ln",
    )(*a_list, w, x, g.reshape(1, N), b.reshape(1, N))


def _cumsum_rows(v, n_rows):
    row = lax.broadcasted_iota(jnp.int32, v.shape, 0)
    shift = 1
    while shift < n_rows:
        v = v + jnp.where(row >= shift, pltpu.roll(v, shift, 0), 0.0)
        shift *= 2
    return v


def _forget_kernel(x_ref, w_ref, bf_ref, c_ref, carry_ref, *, tm):
    @pl.when(pl.program_id(1) == 0)
    def _():
        carry_ref[...] = jnp.zeros_like(carry_ref)

    z = lax.dot_general(x_ref[...], w_ref[...].astype(BF16), _CONTRACT_LAST,
                        preferred_element_type=F32) + bf_ref[...]
    log_f = jnp.minimum(z, 0.0) - jnp.log1p(jnp.exp(-jnp.abs(z)))
    c = _cumsum_rows(log_f, tm) + carry_ref[...]
    c_ref[...] = c
    carry_ref[...] = c[tm - 1:tm, :]


def forget_cumsum(xb, w_f, b_f, *, batch, tm):
    T, D = xb.shape
    steps = T // batch // tm
    return pl.pallas_call(
        functools.partial(_forget_kernel, tm=tm),
        grid=(batch, steps),
        in_specs=[pl.BlockSpec((tm, D), lambda b, s: (b * steps + s, 0)),
                  pl.BlockSpec((LANES, D), lambda b, s: (0, 0)),
                  pl.BlockSpec((1, LANES), lambda b, s: (0, 0))],
        out_specs=pl.BlockSpec((tm, LANES), lambda b, s: (b * steps + s, 0)),
        out_shape=jax.ShapeDtypeStruct((T, LANES), F32),
        scratch_shapes=[pltpu.VMEM((1, LANES), F32)],
        compiler_params=_params(("parallel", "arbitrary")),
        name="forget_cumsum",
    )(xb, w_f, b_f)


def _pool_kernel(u_ref, w_ref, sc_ref, o_ref, ext_ref, *, ts):
    g = pl.program_id(1)
    si = pl.program_id(2)

    @pl.when(si == 0)
    def _():
        ext_ref[0:POOL_HALO, :] = jnp.zeros((POOL_HALO, ext_ref.shape[1]), F32)

    @pl.when(si > 0)
    def _():
        ext_ref[0:POOL_HALO, :] = ext_ref[ts:ts + POOL_HALO, :]

    u = u_ref[...].astype(F32)
    ext_ref[POOL_HALO:, :] = u
    e1 = ext_ref[...]
    e2 = e1 + pltpu.roll(e1, 1, 0)
    e4 = e2 + pltpu.roll(e2, 2, 0)
    e8 = e4 + pltpu.roll(e4, 4, 0)
    e16 = e8 + pltpu.roll(e8, 8, 0)
    win = jnp.where(g == 0, e2, jnp.where(g == 1, e4, jnp.where(g == 2, e8, e16)))
    win = win[POOL_HALO:, :]
    width = jnp.left_shift(2, g)
    t = si * ts + lax.broadcasted_iota(jnp.int32, (ts, 1), 0)
    cnt = jnp.minimum(t + 1, width).astype(F32)
    pooled = win / cnt - u
    y = jnp.dot(pooled.astype(BF16), w_ref[...], preferred_element_type=F32)
    o_ref[...] = (y * sc_ref[...]).astype(o_ref.dtype)


def pool_mixer(proj, w_pool_b, pool_scale, *, batch, ts):
    T = proj.shape[0]
    G, C, _ = w_pool_b.shape
    assert POOL_WINDOWS == (2, 4, 8, 16) and G == len(POOL_WINDOWS)
    steps = T // batch // ts
    return pl.pallas_call(
        functools.partial(_pool_kernel, ts=ts),
        grid=(batch, G, steps),
        in_specs=[pl.BlockSpec((ts, C), lambda b, g, s: (b * steps + s, g)),
                  pl.BlockSpec((None, C, C), lambda b, g, s: (g, 0, 0)),
                  pl.BlockSpec((1, C), lambda b, g, s: (0, g))],
        out_specs=pl.BlockSpec((ts, C), lambda b, g, s: (b * steps + s, g)),
        out_shape=jax.ShapeDtypeStruct((T, G * C), BF16),
        scratch_shapes=[pltpu.VMEM((ts + POOL_HALO, C), F32)],
        compiler_params=_params(("parallel", "parallel", "arbitrary")),
        name="pool_mixer",
    )(proj, w_pool_b, pool_scale.reshape(1, G * C))


LOG2E = 1.4426950408889634
N_BIAS_PARTS = 3


def _attn_kernel(q_ref, k_ref, v_ref, c_ref, o_ref, kaug_ref, vaug_ref, qext_ref,
                 qaug_ref, sa_ref, sb_ref, m_ref, acc_ref, *, tq, tk, scale, head_lane0):
    h = pl.program_id(1)
    qi = pl.program_id(2)
    S = k_ref.shape[0]

    @pl.when(qi == 0)
    def _():
        lane = lax.broadcasted_iota(jnp.int32, (S, LANES), 1)
        c = jnp.sum(jnp.where(lane == head_lane0 + h, c_ref[...], 0.0), axis=1,
                    keepdims=True) * LOG2E
        hi = c.astype(BF16).astype(F32)
        mid = (c - hi).astype(BF16).astype(F32)
        lo = c - hi - mid
        ones = jnp.where(lane < N_BIAS_PARTS, 1.0, 0.0)
        k_ext = jnp.where(lane == 3, -hi, jnp.where(lane == 4, -mid, jnp.where(lane == 5, -lo, ones)))
        q_ext = jnp.where(lane == 0, hi, jnp.where(lane == 1, mid, jnp.where(
            lane == 2, lo, jnp.where(lane < 2 * N_BIAS_PARTS, 1.0, 0.0))))
        kaug_ref[:, :HEAD_DIM] = k_ref[...]
        kaug_ref[:, HEAD_DIM:] = k_ext.astype(BF16)
        vaug_ref[:, :HEAD_DIM] = v_ref[...]
        vaug_ref[:, HEAD_DIM:] = jnp.where(lane == 0, 1.0, 0.0).astype(BF16)
        qext_ref[...] = q_ext.astype(BF16)

    q0 = pl.multiple_of(qi * tq, tq)
    qaug_ref[:, :HEAD_DIM] = (q_ref[...].astype(F32) * (scale * LOG2E)).astype(BF16)
    qaug_ref[:, HEAD_DIM:] = qext_ref[pl.ds(q0, tq), :]
    m_ref[...] = jnp.full(m_ref.shape, -jnp.inf, F32)
    acc_ref[...] = jnp.zeros(acc_ref.shape, F32)

    def scores(block, lo=0):
        k0 = pl.multiple_of(block * tk, tk)
        return lax.dot_general(qaug_ref[lo:, :], kaug_ref[pl.ds(k0, tk), :],
                               (((1,), (1,)), ((), ())), preferred_element_type=F32)

    def softmax_pv(s, block, lo=0):
        k0 = pl.multiple_of(block * tk, tk)
        m = m_ref[lo:, :]
        m_new = jnp.maximum(m, jnp.max(s, axis=1, keepdims=True))
        alpha = jnp.exp2(m - m_new)
        p = jnp.exp2((s - m_new).astype(BF16))
        acc_ref[lo:, :] = alpha * acc_ref[lo:, :] + jnp.dot(
            p, vaug_ref[pl.ds(k0, tk), :], preferred_element_type=F32)
        m_ref[lo:, :] = m_new

    n_full = qi * (tq // tk)
    sa_ref[...] = scores(0)

    def body(i, carry):
        j = 2 * i
        sb_ref[...] = scores(j + 1)
        softmax_pv(sa_ref[...], j)
        sa_ref[...] = scores(j + 2)
        softmax_pv(sb_ref[...], j + 1)
        return carry

    lax.fori_loop(0, n_full // 2, body, 0)

    diag_scores = [sa_ref[...]] + [scores(n_full + u, u * tk) for u in range(1, tq // tk)]
    for u, s in enumerate(diag_scores):
        lo = u * tk
        row = lax.broadcasted_iota(jnp.int32, s.shape, 0)
        col = lax.broadcasted_iota(jnp.int32, s.shape, 1)
        softmax_pv(jnp.where(col <= row, s, -jnp.inf), n_full + u, lo)
        acc = acc_ref[lo:lo + tk, :]
        o_ref[lo:lo + tk, :] = (acc[:, :HEAD_DIM] / acc[:, HEAD_DIM:HEAD_DIM + 1]
                                ).astype(o_ref.dtype)


def forgetting_attention(proj, c, *, batch, heads, head_lane0, q_col, k_col, v_col, tq, tk):
    T = proj.shape[0]
    S = T // batch
    nq = S // tq
    assert (tq // tk) % 2 == 0, "full key blocks are consumed two per loop trip"
    return pl.pallas_call(
        functools.partial(_attn_kernel, tq=tq, tk=tk, scale=HEAD_DIM ** -0.5,
                          head_lane0=head_lane0),
        grid=(batch, heads, nq),
        in_specs=[pl.BlockSpec((tq, HEAD_DIM), lambda b, h, i: (b * nq + i, q_col + h)),
                  pl.BlockSpec((S, HEAD_DIM), lambda b, h, i: (b, k_col + h)),
                  pl.BlockSpec((S, HEAD_DIM), lambda b, h, i: (b, v_col + h)),
                  pl.BlockSpec((S, LANES), lambda b, h, i: (b, 0))],
        out_specs=pl.BlockSpec((tq, HEAD_DIM), lambda b, h, i: (b * nq + i, h)),
        out_shape=jax.ShapeDtypeStruct((T, heads * HEAD_DIM), BF16),
        scratch_shapes=[pltpu.VMEM((S, 2 * HEAD_DIM), BF16), pltpu.VMEM((S, 2 * HEAD_DIM), BF16),
                        pltpu.VMEM((S, LANES), BF16), pltpu.VMEM((tq, 2 * HEAD_DIM), BF16),
                        pltpu.VMEM((tq, tk), F32), pltpu.VMEM((tq, tk), F32),
                        pltpu.VMEM((tq, 1), F32), pltpu.VMEM((tq, 2 * HEAD_DIM), F32)],
        compiler_params=_params(("parallel", "parallel", "arbitrary")),
        name="forgetting_attention",
    )(proj, proj, proj, c)


def _rglru_kernel(xb_ref, gate_ref, cw_ref, cb_ref, wa_ref, ba_ref, wx_ref, bx_ref, lam_ref,
                  y_ref, ext_ref, a_ref, b_ref, h_ref, carry_ref, *, ts):
    si = pl.program_id(2)
    C = xb_ref.shape[1]

    @pl.when(si == 0)
    def _():
        ext_ref[0:CONV_HALO, :] = jnp.zeros((CONV_HALO, C), F32)
        carry_ref[...] = jnp.zeros_like(carry_ref)

    @pl.when(si > 0)
    def _():
        ext_ref[0:CONV_HALO, :] = ext_ref[ts:ts + CONV_HALO, :]

    xb = xb_ref[...].astype(F32)
    ext_ref[CONV_HALO:, :] = xb
    cw = cw_ref[...]
    xc = cb_ref[...] + xb * cw[CONV_WIDTH - 1:CONV_WIDTH, :]
    for lag in range(1, CONV_WIDTH):
        tap = CONV_WIDTH - 1 - lag
        xc = xc + ext_ref[CONV_HALO - lag:CONV_HALO - lag + ts, :] * cw[tap:tap + 1, :]

    xcb = xc.astype(BF16)
    r = jax.nn.sigmoid(jnp.dot(xcb, wa_ref[...], preferred_element_type=F32) + ba_ref[...])
    ig = jax.nn.sigmoid(jnp.dot(xcb, wx_ref[...], preferred_element_type=F32) + bx_ref[...])
    lam = lam_ref[...]
    log_a_base = -(jnp.maximum(-lam, 0.0) + jnp.log1p(jnp.exp(-jnp.abs(lam))))
    log_a = LRU_C * r * log_a_base
    a = jnp.exp(log_a)
    mult = jnp.sqrt(-jnp.tanh(log_a) * (1.0 + a * a))
    b = mult * ig * xc

    groups = ts // SUBLANES
    a3 = a.reshape(groups, SUBLANES, C)
    b3 = b.reshape(groups, SUBLANES, C)
    sub = lax.broadcasted_iota(jnp.int32, a3.shape, 1)
    shift = 1
    while shift < SUBLANES:
        keep = sub >= shift
        b3 = jnp.where(keep, a3 * pltpu.roll(b3, shift, 1) + b3, b3)
        a3 = jnp.where(keep, a3 * pltpu.roll(a3, shift, 1), a3)
        shift *= 2
    a_ref[...] = a3.reshape(ts, C)
    b_ref[...] = b3.reshape(ts, C)

    def body(gi, carry):
        r0 = pl.multiple_of(gi * SUBLANES, SUBLANES)
        hg = a_ref[pl.ds(r0, SUBLANES), :] * carry + b_ref[pl.ds(r0, SUBLANES), :]
        h_ref[pl.ds(r0, SUBLANES), :] = hg
        return hg[SUBLANES - 1:SUBLANES, :]

    carry_ref[...] = lax.fori_loop(0, groups, body, carry_ref[...], unroll=8)

    gate = gate_ref[...].astype(F32)
    gelu = 0.5 * gate * (1.0 + jnp.tanh(0.7978845608028654 * (gate + 0.044715 * gate * gate * gate)))
    y_ref[...] = (h_ref[...] * gelu).astype(y_ref.dtype)


def rglru_mixer(proj, conv_w, conv_b, w_a_b, b_a, w_x_b, b_x, lam, *, batch, ts):
    T = proj.shape[0]
    NB, C, _ = w_a_b.shape
    W = NB * C
    steps = T // batch // ts
    vec = lambda: pl.BlockSpec((1, C), lambda b, c, s: (0, c))
    blk = lambda: pl.BlockSpec((None, C, C), lambda b, c, s: (c, 0, 0))
    return pl.pallas_call(
        functools.partial(_rglru_kernel, ts=ts),
        grid=(batch, NB, steps),
        in_specs=[pl.BlockSpec((ts, C), lambda b, c, s: (b * steps + s, NB + c)),
                  pl.BlockSpec((ts, C), lambda b, c, s: (b * steps + s, c)),
                  pl.BlockSpec((CONV_WIDTH, C), lambda b, c, s: (0, c)),
                  vec(), blk(), vec(), blk(), vec(), vec()],
        out_specs=pl.BlockSpec((ts, C), lambda b, c, s: (b * steps + s, c)),
        out_shape=jax.ShapeDtypeStruct((T, W), BF16),
        scratch_shapes=[pltpu.VMEM((ts + CONV_HALO, C), F32),
                        pltpu.VMEM((ts, C), F32), pltpu.VMEM((ts, C), F32),
                        pltpu.VMEM((ts, C), F32), pltpu.VMEM((1, C), F32)],
        compiler_params=_params(("parallel", "parallel", "arbitrary")),
        name="rglru_mixer",
    )(proj, proj, conv_w, conv_b.reshape(1, W), w_a_b, b_a.reshape(1, W), w_x_b,
      b_x.reshape(1, W), lam.reshape(1, W))


def _router_kernel(x_ref, w_ref, b_ref, id_ref, wt_ref):
    logits = jnp.dot(x_ref[...], w_ref[...], preferred_element_type=F32,
                     precision=lax.Precision.HIGHEST) + b_ref[...]
    lane = lax.broadcasted_iota(jnp.int32, logits.shape, 1).astype(F32)
    neg = -jnp.inf
    big = float(LANES)

    is_group = lane < N_GROUPS
    gl = jnp.where(is_group, logits, neg)
    gmax = jnp.max(gl, axis=1, keepdims=True)
    g_idx = jnp.min(jnp.where(gl == gmax, lane, big), axis=1, keepdims=True)
    g_w = 1.0 / jnp.sum(jnp.where(is_group, jnp.exp(gl - gmax), 0.0), axis=1, keepdims=True)

    lo = N_GROUPS + g_idx * EXPERTS_PER_GROUP
    el = jnp.where(lane >= lo, jnp.where(lane < lo + EXPERTS_PER_GROUP, logits, neg), neg)
    v1 = jnp.max(el, axis=1, keepdims=True)
    i1 = jnp.min(jnp.where(el == v1, lane, big), axis=1, keepdims=True)
    el2 = jnp.where(lane == i1, neg, el)
    v2 = jnp.max(el2, axis=1, keepdims=True)
    i2 = jnp.min(jnp.where(el2 == v2, lane, big), axis=1, keepdims=True)
    t = jnp.exp(v2 - v1)
    w1 = g_w / (1.0 + t)
    w2 = w1 * t
    ids = jnp.where(lane == 0, i1 - N_GROUPS, jnp.where(lane == 1, i2 - N_GROUPS, 0.0))
    id_ref[...] = ids.astype(jnp.int32)
    wt_ref[...] = jnp.where(lane == 0, w1, jnp.where(lane == 1, w2, 0.0))


def router(x, w_route, b_route, *, tm):
    T, D = x.shape
    return pl.pallas_call(
        _router_kernel,
        grid=(T // tm,),
        in_specs=[pl.BlockSpec((tm, D), lambda i: (i, 0)),
                  pl.BlockSpec((D, LANES), lambda i: (0, 0)),
                  pl.BlockSpec((1, LANES), lambda i: (0, 0))],
        out_specs=[pl.BlockSpec((tm, LANES), lambda i: (i, 0)),
                   pl.BlockSpec((tm, LANES), lambda i: (i, 0))],
        out_shape=[jax.ShapeDtypeStruct((T, LANES), jnp.int32),
                   jax.ShapeDtypeStruct((T, LANES), F32)],
        compiler_params=_params(("parallel",)),
        name="router",
    )(x, w_route, b_route)


def _row_gather_start(src_hbm, dst_ref, sem, row_of, n_rows):
    def body(r, carry):
        pltpu.make_async_copy(src_hbm.at[pl.ds(row_of(r), 1), :],
                              dst_ref.at[pl.ds(r, 1), :], sem).start()
        return carry
    lax.fori_loop(0, n_rows, body, 0, unroll=8)


def _row_gather_wait(src_hbm, dst_ref, sem, n_rows):
    pltpu.make_async_copy(src_hbm.at[pl.ds(0, n_rows), :], dst_ref, sem).wait()


PLAN_EXPERT, PLAN_FIRST, PLAN_SLOT, PLAN_NEXT = range(4)


def _expert_weight_copies(w_hbm_list, buf_list, sem, layer, expert, slot):
    return [pltpu.make_async_copy(w.at[layer, expert], buf.at[slot], sem.at[slot])
            for w, buf in zip(w_hbm_list, buf_list)]


def _expert_weight_pipeline(plan_ref, n_used, w_hbm_list, buf_list, sem, layer):
    j = pl.program_id(0)
    slot = plan_ref[PLAN_SLOT, j]
    first = jnp.logical_and(j < n_used, plan_ref[PLAN_FIRST, j] == 1)

    @pl.when(j == 0)
    def _():
        for cp in _expert_weight_copies(w_hbm_list, buf_list, sem, layer,
                                        plan_ref[PLAN_EXPERT, 0], 0):
            cp.start()

    @pl.when(jnp.logical_and(first, plan_ref[PLAN_NEXT, j] >= 0))
    def _():
        for cp in _expert_weight_copies(w_hbm_list, buf_list, sem, layer,
                                        plan_ref[PLAN_NEXT, j], 1 - slot):
            cp.start()

    @pl.when(first)
    def _():
        for cp in _expert_weight_copies(w_hbm_list, buf_list, sem, layer,
                                        plan_ref[PLAN_EXPERT, j], slot):
            cp.wait()

    return slot


def _moe_up_kernel(plan_ref, rt_ref, nu_ref, x_hbm, w1_hbm, w3_hbm, h_ref,
                   xg_ref, w1_buf, w3_buf, xsem, wsem, *, tm, layer):
    j = pl.program_id(0)
    n_used = nu_ref[0]

    def gather(tile, slot):
        _row_gather_start(x_hbm, xg_ref.at[slot], xsem.at[slot],
                          lambda r: rt_ref[tile * tm + r], tm)

    @pl.when(j == 0)
    def _():
        gather(0, 0)

    @pl.when(j + 1 < n_used)
    def _():
        gather(j + 1, (j + 1) % 2)

    wslot = _expert_weight_pipeline(plan_ref, n_used, (w1_hbm, w3_hbm), (w1_buf, w3_buf),
                                    wsem, layer)

    @pl.when(j < n_used)
    def _():
        slot = j % 2
        _row_gather_wait(x_hbm, xg_ref.at[slot], xsem.at[slot], tm)
        x = xg_ref[slot].astype(BF16)
        a = jnp.dot(x, w1_buf[wslot].astype(BF16), preferred_element_type=F32)
        b = jnp.dot(x, w3_buf[wslot].astype(BF16), preferred_element_type=F32)
        h_ref[...] = (a * jax.nn.sigmoid(a) * b).astype(h_ref.dtype)

    @pl.when(j >= n_used)
    def _():
        h_ref[...] = jnp.zeros_like(h_ref)


def moe_up(x, w1, w3, plan, row_token, n_used, *, layer, tm):
    D = x.shape[1]
    FF = w1.shape[-1]
    n_tiles = plan.shape[1]
    any_spec = lambda: pl.BlockSpec(memory_space=pl.ANY)
    return pl.pallas_call(
        functools.partial(_moe_up_kernel, tm=tm, layer=layer),
        grid_spec=pltpu.PrefetchScalarGridSpec(
            num_scalar_prefetch=3,
            grid=(n_tiles,),
            in_specs=[any_spec(), any_spec(), any_spec()],
            out_specs=pl.BlockSpec((tm, FF), lambda j, plan, rt, nu: (j, 0)),
            scratch_shapes=[pltpu.VMEM((2, tm, D), F32),
                            pltpu.VMEM((2, D, FF), F32), pltpu.VMEM((2, D, FF), F32),
                            pltpu.SemaphoreType.DMA((2,)), pltpu.SemaphoreType.DMA((2,))]),
        out_shape=jax.ShapeDtypeStruct((n_tiles * tm, FF), BF16),
        compiler_params=_params(("arbitrary",)),
        name="moe_up",
    )(plan, row_token, n_used, x, w1, w3)


def _moe_down_kernel(plan_ref, nu_ref, h_ref, w2_hbm, y_ref, w2_buf, wsem, *, layer):
    j = pl.program_id(0)
    n_used = nu_ref[0]
    wslot = _expert_weight_pipeline(plan_ref, n_used, (w2_hbm,), (w2_buf,), wsem, layer)

    @pl.when(j < n_used)
    def _():
        y_ref[...] = jnp.dot(h_ref[...], w2_buf[wslot].astype(BF16),
                             preferred_element_type=F32)

    @pl.when(j >= n_used)
    def _():
        y_ref[...] = jnp.zeros_like(y_ref)


def moe_down(h, w2, plan, n_used, *, layer, tm):
    FF, D = w2.shape[-2:]
    n_tiles = plan.shape[1]
    tile = lambda j, plan, nu: (j, 0)
    return pl.pallas_call(
        functools.partial(_moe_down_kernel, layer=layer),
        grid_spec=pltpu.PrefetchScalarGridSpec(
            num_scalar_prefetch=2,
            grid=(n_tiles,),
            in_specs=[pl.BlockSpec((tm, FF), tile), pl.BlockSpec(memory_space=pl.ANY)],
            out_specs=pl.BlockSpec((tm, D), tile),
            scratch_shapes=[pltpu.VMEM((2, FF, D), F32), pltpu.SemaphoreType.DMA((2,))]),
        out_shape=jax.ShapeDtypeStruct((n_tiles * tm, D), F32),
        compiler_params=_params(("arbitrary",)),
        name="moe_down",
    )(plan, n_used, h, w2)


def _combine_ln_kernel(pos_ref, y_hbm, x_ref, wt_ref, g_ref, b_ref, o_ref, ob_ref,
                       yg_ref, sem, *, tm):
    i = pl.program_id(0)

    def gather(tile, slot):
        for k in range(TOP_K):
            _row_gather_start(y_hbm, yg_ref.at[slot, k], sem.at[slot],
                              lambda r: pos_ref[TOP_K * (tile * tm + r) + k], tm)

    @pl.when(i == 0)
    def _():
        gather(0, 0)

    @pl.when(i + 1 < pl.num_programs(0))
    def _():
        gather(i + 1, (i + 1) % 2)

    slot = i % 2
    for k in range(TOP_K):
        _row_gather_wait(y_hbm, yg_ref.at[slot, k], sem.at[slot], tm)
    wt = wt_ref[...]
    z = DN_ALPHA * x_ref[...]
    for k in range(TOP_K):
        z = z + wt[:, k:k + 1] * yg_ref[slot, k]
    y = _layer_norm_rows(z, g_ref[...], b_ref[...])
    o_ref[...] = y
    ob_ref[...] = y.astype(BF16)


def moe_combine_ln(y, pos, x, wts, g, b, *, tm):
    T, D = x.shape
    row = lambda i, pos: (i, 0)
    fixed = lambda i, pos: (0, 0)
    return pl.pallas_call(
        functools.partial(_combine_ln_kernel, tm=tm),
        grid_spec=pltpu.PrefetchScalarGridSpec(
            num_scalar_prefetch=1,
            grid=(T // tm,),
            in_specs=[pl.BlockSpec(memory_space=pl.ANY),
                      pl.BlockSpec((tm, D), row),
                      pl.BlockSpec((tm, LANES), row),
                      pl.BlockSpec((1, D), fixed),
                      pl.BlockSpec((1, D), fixed)],
            out_specs=[pl.BlockSpec((tm, D), row), pl.BlockSpec((tm, D), row)],
            scratch_shapes=[pltpu.VMEM((2, TOP_K, tm, D), F32),
                            pltpu.SemaphoreType.DMA((2,))]),
        out_shape=[jax.ShapeDtypeStruct((T, D), F32), jax.ShapeDtypeStruct((T, D), BF16)],
        compiler_params=_params(("arbitrary",)),
        name="moe_combine_ln",
    )(pos, y, x, wts, g.reshape(1, D), b.reshape(1, D))


def _dispatch_plan(ids, *, n_experts, tm, n_tiles):
    n_assign = ids.shape[0]
    experts = jnp.arange(n_experts, dtype=jnp.int32)
    onehot = (ids[:, None] == experts[None, :]).astype(jnp.int32)
    csum = jnp.cumsum(onehot, axis=0)
    rank = jnp.sum((csum - onehot) * onehot, axis=1)
    counts = csum[-1]
    tiles_per = (counts + tm - 1) // tm
    tile_end = jnp.cumsum(tiles_per)
    tile_start = tile_end - tiles_per
    n_used = tile_end[-1]
    pos = tile_start[ids] * tm + rank
    row_token = jnp.zeros((n_tiles * tm,), jnp.int32).at[pos].set(
        jnp.arange(n_assign, dtype=jnp.int32) // TOP_K)

    tile_ids = jnp.arange(n_tiles, dtype=jnp.int32)
    in_use = tile_ids < n_used
    tile_expert = jnp.sum((jnp.minimum(tile_ids, n_used - 1)[:, None] >= tile_end[None, :])
                          .astype(jnp.int32), axis=1)
    used = counts > 0
    ordinal = jnp.cumsum(used.astype(jnp.int32)) - 1
    later = lax.cummin(jnp.where(used, experts, n_experts), axis=0, reverse=True)
    next_used = jnp.concatenate([later[1:], jnp.full((1,), n_experts, jnp.int32)])
    next_used = jnp.where(next_used < n_experts, next_used, -1)
    plan = jnp.stack([
        tile_expert,
        jnp.logical_and(in_use, tile_ids == tile_start[tile_expert]).astype(jnp.int32),
        ordinal[tile_expert] % 2,
        next_used[tile_expert],
    ]).astype(jnp.int32)
    return pos.astype(jnp.int32), row_token, plan, n_used.reshape(1).astype(jnp.int32)


def hierarchical_moe_ln(x, w_group, b_group, w_expert, b_expert, w1, w3, w2, g, b, *,
                        layer, tm_route, tm_moe, tm_comb):
    T, D = x.shape
    E = w1.shape[1]
    n_route = N_GROUPS + E
    w_route = jnp.zeros((D, LANES), F32).at[:, :N_GROUPS].set(w_group).at[:, N_GROUPS:n_route].set(w_expert)
    b_route = jnp.zeros((1, LANES), F32).at[0, :N_GROUPS].set(b_group).at[0, N_GROUPS:n_route].set(b_expert)
    ids, wts = router(x, w_route, b_route, tm=tm_route)
    n_tiles = (T * TOP_K + E * (tm_moe - 1)) // tm_moe
    pos, row_token, plan, n_used = _dispatch_plan(
        ids[:, :TOP_K].reshape(-1), n_experts=E, tm=tm_moe, n_tiles=n_tiles)
    h = moe_up(x, w1, w3, plan, row_token, n_used, layer=layer, tm=tm_moe)
    y = moe_down(h, w2, plan, n_used, layer=layer, tm=tm_moe)
    return moe_combine_ln(y, pos, x, wts, g, b, tm=tm_comb)


def kernel(x, even_w_in, even_w_pool, even_pool_scale, even_b_f, even_w_out, odd_w_in, odd_conv_w, odd_conv_b, odd_w_a, odd_b_a, odd_w_x, odd_b_x, odd_lambda, odd_w_out, moe_w_group, moe_b_group, moe_w_expert, moe_b_expert, moe_w1, moe_w3, moe_w2, ln_g, ln_b):
    B, S, D = x.shape
    T = B * S
    xf = x.reshape(T, D)
    xb = xf.astype(BF16)
    tiles = _tile_plan(T, S, D)

    for layer in range(DEPTH):
        i = layer // 2
        if layer % 2 == 0:
            pool_w = even_w_pool.shape[1] * even_w_pool.shape[2]
            heads = even_b_f.shape[1]
            fox_w = heads * HEAD_DIM
            w_in_t = jnp.swapaxes(even_w_in[i], 0, 1)
            proj = matmul(xb, w_in_t, n_cols=pool_w + 3 * fox_w, tm=tiles["mm_m"],
                          tn=tiles["mm_n"], out_dtype=BF16, w_is_transposed=True)
            lane0 = LANES - heads
            w_f = w_in_t[w_in_t.shape[0] - LANES:, :]
            b_f = jnp.zeros((1, LANES), F32).at[0, lane0:].set(even_b_f[i])
            c = forget_cumsum(xb, w_f, b_f, batch=B, tm=tiles["seq"])
            a_out = pool_mixer(proj, even_w_pool[i].astype(BF16), even_pool_scale[i],
                               batch=B, ts=tiles["seq"])
            qc = pool_w // HEAD_DIM
            b_out = forgetting_attention(proj, c, batch=B, heads=heads, head_lane0=lane0,
                                         q_col=qc, k_col=qc + heads, v_col=qc + 2 * heads,
                                         tq=tiles["attn_q"], tk=tiles["attn_k"])
            mixed = [a_out, b_out]
            w_out = even_w_out[i].astype(BF16)
        else:
            proj = matmul(xb, odd_w_in[i], n_cols=odd_w_in.shape[2], tm=tiles["mm_m"],
                          tn=tiles["mm_n"], out_dtype=BF16)
            mixed = [rglru_mixer(proj, odd_conv_w[i], odd_conv_b[i], odd_w_a[i].astype(BF16),
                                 odd_b_a[i].reshape(-1), odd_w_x[i].astype(BF16),
                                 odd_b_x[i].reshape(-1), odd_lambda[i], batch=B, ts=tiles["seq"])]
            w_out = odd_w_out[i].astype(BF16)
        xf = matmul_residual_ln(mixed, w_out, xf, ln_g[layer, 0], ln_b[layer, 0],
                                tm=tiles["ln_m"])
        xf, xb = hierarchical_moe_ln(xf, moe_w_group[layer], moe_b_group[layer],
                                     moe_w_expert[layer], moe_b_expert[layer], moe_w1, moe_w3,
                                     moe_w2, ln_g[layer, 1], ln_b[layer, 1], layer=layer,
                                     tm_route=tiles["route"], tm_moe=tiles["moe"],
                                     tm_comb=tiles["comb"])
    return xf.reshape(B, S, D)


def _tile_plan(T, S, D):
    return {
        "mm_m": min(1024, T), "mm_n": min(512, D),
        "seq": min(512, S), "attn_q": min(1024, S), "attn_k": min(512, S),
        "ln_m": min(256, T),
        "route": min(512, T), "moe": 256, "comb": min(256, T),
    }
```

```python
import functools

import jax
import jax.numpy as jnp
from jax import lax
from jax.experimental import pallas as pl
from jax.experimental.pallas import tpu as pltpu

F32 = jnp.float32
BF16 = jnp.bfloat16

POOL_WINDOWS = (2, 4, 8, 16)
POOL_HALO = 16
HEAD_DIM = 128
LRU_BLOCKS = 16
CONV_WIDTH = 4
CONV_HALO = 8
LRU_C = 8.0
N_GROUPS = 4
EXPERTS_PER_GROUP = 8
TOP_K = 2
DEPTH = 2
DN_ALPHA = (2.0 * DEPTH) ** 0.25
LN_EPS = 1e-5

LANES = 128
SUBLANES = 8
VMEM_LIMIT_MB = 56
OUT_PROJ_VMEM_MB = 60


def _params(semantics, vmem_mb=VMEM_LIMIT_MB):
    return pltpu.CompilerParams(dimension_semantics=semantics,
                                vmem_limit_bytes=vmem_mb * 1024 * 1024)


_CONTRACT_LAST = (((1,), (1,)), ((), ()))


def _mm_kernel(a_ref, w_ref, o_ref, *, w_is_transposed):
    w = w_ref[...].astype(BF16)
    if w_is_transposed:
        out = lax.dot_general(a_ref[...], w, _CONTRACT_LAST, preferred_element_type=F32)
    else:
        out = jnp.dot(a_ref[...], w, preferred_element_type=F32)
    o_ref[...] = out.astype(o_ref.dtype)


def matmul(a, w, *, n_cols, tm, tn, out_dtype, w_is_transposed=False):
    M, K = a.shape
    if w_is_transposed:
        w_spec = pl.BlockSpec((tn, K), lambda j, i: (j, 0))
    else:
        w_spec = pl.BlockSpec((K, tn), lambda j, i: (0, j))
    return pl.pallas_call(
        functools.partial(_mm_kernel, w_is_transposed=w_is_transposed),
        grid=(n_cols // tn, M // tm),
        in_specs=[pl.BlockSpec((tm, K), lambda j, i: (i, 0)), w_spec],
        out_specs=pl.BlockSpec((tm, tn), lambda j, i: (i, j)),
        out_shape=jax.ShapeDtypeStruct((M, n_cols), out_dtype),
        compiler_params=_params(("parallel", "parallel")),
        name="proj_matmul",
    )(a, w)


def _layer_norm_rows(z, g, b):
    mu = jnp.mean(z, axis=-1, keepdims=True)
    d = z - mu
    var = jnp.mean(d * d, axis=-1, keepdims=True)
    return d * lax.rsqrt(var + LN_EPS) * g + b


LN_CHUNK_ROWS = 64


def _mm_ln_kernel(*refs, n_a):
    a_refs = refs[:n_a]
    w_hbm, x_hbm, g_ref, b_ref, o_ref, w_buf, acc_ref, sem = refs[n_a:]
    i = pl.program_id(0)
    tm = o_ref.shape[0]

    @pl.when(i == 0)
    def _():
        w_copy = pltpu.make_async_copy(w_hbm, w_buf, sem.at[0])
        w_copy.start()
        w_copy.wait()

    x_copy = pltpu.make_async_copy(x_hbm.at[pl.ds(pl.multiple_of(i * tm, tm), tm), :], o_ref,
                                   sem.at[1])
    x_copy.start()
    k_each = a_refs[0].shape[1]
    for idx, a_ref in enumerate(a_refs):
        part = jnp.dot(a_ref[...], w_buf[idx * k_each:(idx + 1) * k_each, :],
                       preferred_element_type=F32)
        if idx == 0:
            acc_ref[...] = part
        else:
            acc_ref[...] += part
    x_copy.wait()

    def normalise(c, carry):
        rows = pl.ds(pl.multiple_of(c * LN_CHUNK_ROWS, LN_CHUNK_ROWS), LN_CHUNK_ROWS)
        o_ref[rows, :] = _layer_norm_rows(DN_ALPHA * o_ref[rows, :] + acc_ref[rows, :],
                                          g_ref[...], b_ref[...])
        return carry

    lax.fori_loop(0, tm // LN_CHUNK_ROWS, normalise, 0)


def matmul_residual_ln(a_list, w, x, g, b, *, tm):
    M, K_each = a_list[0].shape
    K, N = w.shape
    n_a = len(a_list)
    assert n_a * K_each == K and tm % LN_CHUNK_ROWS == 0
    return pl.pallas_call(
        functools.partial(_mm_ln_kernel, n_a=n_a),
        grid=(M // tm,),
        in_specs=[pl.BlockSpec((tm, K_each), lambda i: (i, 0)) for _ in range(n_a)] + [
            pl.BlockSpec(memory_space=pl.ANY),
            pl.BlockSpec(memory_space=pl.ANY),
            pl.BlockSpec((1, N), lambda i: (0, 0)),
            pl.BlockSpec((1, N), lambda i: (0, 0))],
        out_specs=pl.BlockSpec((tm, N), lambda i: (i, 0)),
        out_shape=jax.ShapeDtypeStruct((M, N), F32),
        scratch_shapes=[pltpu.VMEM((K, N), BF16), pltpu.VMEM((tm, N), F32),
                        pltpu.SemaphoreType.DMA((2,))],
        compiler_params=_params(("arbitrary",), vmem_mb=OUT_PROJ_VMEM_MB),
        name="out_proj_ln",
    )(*a_list, w, x, g.reshape(1, N), b.reshape(1, N))


def _cumsum_rows(v, n_rows):
    row = lax.broadcasted_iota(jnp.int32, v.shape, 0)
    shift = 1
    while shift < n_rows:
        v = v + jnp.where(row >= shift, pltpu.roll(v, shift, 0), 0.0)
        shift *= 2
    return v


def _forget_kernel(x_ref, w_ref, bf_ref, c_ref, carry_ref, *, tm):
    @pl.when(pl.program_id(1) == 0)
    def _():
        carry_ref[...] = jnp.zeros_like(carry_ref)

    z = lax.dot_general(x_ref[...], w_ref[...].astype(BF16), _CONTRACT_LAST,
                        preferred_element_type=F32) + bf_ref[...]
    log_f = jnp.minimum(z, 0.0) - jnp.log1p(jnp.exp(-jnp.abs(z)))
    c = _cumsum_rows(log_f, tm) + carry_ref[...]
    c_ref[...] = c
    carry_ref[...] = c[tm - 1:tm, :]


def forget_cumsum(xb, w_f, b_f, *, batch, tm):
    T, D = xb.shape
    steps = T // batch // tm
    return pl.pallas_call(
        functools.partial(_forget_kernel, tm=tm),
        grid=(batch, steps),
        in_specs=[pl.BlockSpec((tm, D), lambda b, s: (b * steps + s, 0)),
                  pl.BlockSpec((LANES, D), lambda b, s: (0, 0)),
                  pl.BlockSpec((1, LANES), lambda b, s: (0, 0))],
        out_specs=pl.BlockSpec((tm, LANES), lambda b, s: (b * steps + s, 0)),
        out_shape=jax.ShapeDtypeStruct((T, LANES), F32),
        scratch_shapes=[pltpu.VMEM((1, LANES), F32)],
        compiler_params=_params(("parallel", "arbitrary")),
        name="forget_cumsum",
    )(xb, w_f, b_f)


def _pool_kernel(u_ref, w_ref, sc_ref, o_ref, ext_ref, *, ts):
    g = pl.program_id(1)
    si = pl.program_id(2)

    @pl.when(si == 0)
    def _():
        ext_ref[0:POOL_HALO, :] = jnp.zeros((POOL_HALO, ext_ref.shape[1]), F32)

    @pl.when(si > 0)
    def _():
        ext_ref[0:POOL_HALO, :] = ext_ref[ts:ts + POOL_HALO, :]

    u = u_ref[...].astype(F32)
    ext_ref[POOL_HALO:, :] = u
    e1 = ext_ref[...]
    e2 = e1 + pltpu.roll(e1, 1, 0)
    e4 = e2 + pltpu.roll(e2, 2, 0)
    e8 = e4 + pltpu.roll(e4, 4, 0)
    e16 = e8 + pltpu.roll(e8, 8, 0)
    win = jnp.where(g == 0, e2, jnp.where(g == 1, e4, jnp.where(g == 2, e8, e16)))
    win = win[POOL_HALO:, :]
    width = jnp.left_shift(2, g)
    t = si * ts + lax.broadcasted_iota(jnp.int32, (ts, 1), 0)
    cnt = jnp.minimum(t + 1, width).astype(F32)
    pooled = win / cnt - u
    y = jnp.dot(pooled.astype(BF16), w_ref[...], preferred_element_type=F32)
    o_ref[...] = (y * sc_ref[...]).astype(o_ref.dtype)


def pool_mixer(proj, w_pool_b, pool_scale, *, batch, ts):
    T = proj.shape[0]
    G, C, _ = w_pool_b.shape
    assert POOL_WINDOWS == (2, 4, 8, 16) and G == len(POOL_WINDOWS)
    steps = T // batch // ts
    return pl.pallas_call(
        functools.partial(_pool_kernel, ts=ts),
        grid=(batch, G, steps),
        in_specs=[pl.BlockSpec((ts, C), lambda b, g, s: (b * steps + s, g)),
                  pl.BlockSpec((None, C, C), lambda b, g, s: (g, 0, 0)),
                  pl.BlockSpec((1, C), lambda b, g, s: (0, g))],
        out_specs=pl.BlockSpec((ts, C), lambda b, g, s: (b * steps + s, g)),
        out_shape=jax.ShapeDtypeStruct((T, G * C), BF16),
        scratch_shapes=[pltpu.VMEM((ts + POOL_HALO, C), F32)],
        compiler_params=_params(("parallel", "parallel", "arbitrary")),
        name="pool_mixer",
    )(proj, w_pool_b, pool_scale.reshape(1, G * C))


LOG2E = 1.4426950408889634
N_BIAS_PARTS = 3


def _attn_kernel(q_ref, k_ref, v_ref, c_ref, o_ref, kaug_ref, vaug_ref, qext_ref,
                 qaug_ref, sa_ref, sb_ref, m_ref, acc_ref, *, tq, tk, scale, head_lane0):
    h = pl.program_id(1)
    qi = pl.program_id(2)
    S = k_ref.shape[0]

    @pl.when(qi == 0)
    def _():
        lane = lax.broadcasted_iota(jnp.int32, (S, LANES), 1)
        c = jnp.sum(jnp.where(lane == head_lane0 + h, c_ref[...], 0.0), axis=1,
                    keepdims=True) * LOG2E
        hi = c.astype(BF16).astype(F32)
        mid = (c - hi).astype(BF16).astype(F32)
        lo = c - hi - mid
        ones = jnp.where(lane < N_BIAS_PARTS, 1.0, 0.0)
        k_ext = jnp.where(lane == 3, -hi, jnp.where(lane == 4, -mid, jnp.where(lane == 5, -lo, ones)))
        q_ext = jnp.where(lane == 0, hi, jnp.where(lane == 1, mid, jnp.where(
            lane == 2, lo, jnp.where(lane < 2 * N_BIAS_PARTS, 1.0, 0.0))))
        kaug_ref[:, :HEAD_DIM] = k_ref[...]
        kaug_ref[:, HEAD_DIM:] = k_ext.astype(BF16)
        vaug_ref[:, :HEAD_DIM] = v_ref[...]
        vaug_ref[:, HEAD_DIM:] = jnp.where(lane == 0, 1.0, 0.0).astype(BF16)
        qext_ref[...] = q_ext.astype(BF16)

    q0 = pl.multiple_of(qi * tq, tq)
    qaug_ref[:, :HEAD_DIM] = (q_ref[...].astype(F32) * (scale * LOG2E)).astype(BF16)
    qaug_ref[:, HEAD_DIM:] = qext_ref[pl.ds(q0, tq), :]
    m_ref[...] = jnp.full(m_ref.shape, -jnp.inf, F32)
    acc_ref[...] = jnp.zeros(acc_ref.shape, F32)

    def scores(block, lo=0):
        k0 = pl.multiple_of(block * tk, tk)
        return lax.dot_general(qaug_ref[lo:, :], kaug_ref[pl.ds(k0, tk), :],
                               (((1,), (1,)), ((), ())), preferred_element_type=F32)

    def softmax_pv(s, block, lo=0):
        k0 = pl.multiple_of(block * tk, tk)
        m = m_ref[lo:, :]
        m_new = jnp.maximum(m, jnp.max(s, axis=1, keepdims=True))
        alpha = jnp.exp2(m - m_new)
        p = jnp.exp2((s - m_new).astype(BF16))
        acc_ref[lo:, :] = alpha * acc_ref[lo:, :] + jnp.dot(
            p, vaug_ref[pl.ds(k0, tk), :], preferred_element_type=F32)
        m_ref[lo:, :] = m_new

    n_full = qi * (tq // tk)
    sa_ref[...] = scores(0)

    def body(i, carry):
        j = 2 * i
        sb_ref[...] = scores(j + 1)
        softmax_pv(sa_ref[...], j)
        sa_ref[...] = scores(j + 2)
        softmax_pv(sb_ref[...], j + 1)
        return carry

    lax.fori_loop(0, n_full // 2, body, 0)

    diag_scores = [sa_ref[...]] + [scores(n_full + u, u * tk) for u in range(1, tq // tk)]
    for u, s in enumerate(diag_scores):
        lo = u * tk
        row = lax.broadcasted_iota(jnp.int32, s.shape, 0)
        col = lax.broadcasted_iota(jnp.int32, s.shape, 1)
        softmax_pv(jnp.where(col <= row, s, -jnp.inf), n_full + u, lo)
        acc = acc_ref[lo:lo + tk, :]
        o_ref[lo:lo + tk, :] = (acc[:, :HEAD_DIM] / acc[:, HEAD_DIM:HEAD_DIM + 1]
                                ).astype(o_ref.dtype)


def forgetting_attention(proj, c, *, batch, heads, head_lane0, q_col, k_col, v_col, tq, tk):
    T = proj.shape[0]
    S = T // batch
    nq = S // tq
    assert (tq // tk) % 2 == 0, "full key blocks are consumed two per loop trip"
    return pl.pallas_call(
        functools.partial(_attn_kernel, tq=tq, tk=tk, scale=HEAD_DIM ** -0.5,
                          head_lane0=head_lane0),
        grid=(batch, heads, nq),
        in_specs=[pl.BlockSpec((tq, HEAD_DIM), lambda b, h, i: (b * nq + i, q_col + h)),
                  pl.BlockSpec((S, HEAD_DIM), lambda b, h, i: (b, k_col + h)),
                  pl.BlockSpec((S, HEAD_DIM), lambda b, h, i: (b, v_col + h)),
                  pl.BlockSpec((S, LANES), lambda b, h, i: (b, 0))],
        out_specs=pl.BlockSpec((tq, HEAD_DIM), lambda b, h, i: (b * nq + i, h)),
        out_shape=jax.ShapeDtypeStruct((T, heads * HEAD_DIM), BF16),
        scratch_shapes=[pltpu.VMEM((S, 2 * HEAD_DIM), BF16), pltpu.VMEM((S, 2 * HEAD_DIM), BF16),
                        pltpu.VMEM((S, LANES), BF16), pltpu.VMEM((tq, 2 * HEAD_DIM), BF16),
                        pltpu.VMEM((tq, tk), F32), pltpu.VMEM((tq, tk), F32),
                        pltpu.VMEM((tq, 1), F32), pltpu.VMEM((tq, 2 * HEAD_DIM), F32)],
        compiler_params=_params(("parallel", "parallel", "arbitrary")),
        name="forgetting_attention",
    )(proj, proj, proj, c)


def _rglru_kernel(xb_ref, gate_ref, cw_ref, cb_ref, wa_ref, ba_ref, wx_ref, bx_ref, lam_ref,
                  y_ref, ext_ref, a_ref, b_ref, h_ref, carry_ref, *, ts):
    si = pl.program_id(2)
    C = xb_ref.shape[1]

    @pl.when(si == 0)
    def _():
        ext_ref[0:CONV_HALO, :] = jnp.zeros((CONV_HALO, C), F32)
        carry_ref[...] = jnp.zeros_like(carry_ref)

    @pl.when(si > 0)
    def _():
        ext_ref[0:CONV_HALO, :] = ext_ref[ts:ts + CONV_HALO, :]

    xb = xb_ref[...].astype(F32)
    ext_ref[CONV_HALO:, :] = xb
    cw = cw_ref[...]
    xc = cb_ref[...] + xb * cw[CONV_WIDTH - 1:CONV_WIDTH, :]
    for lag in range(1, CONV_WIDTH):
        tap = CONV_WIDTH - 1 - lag
        xc = xc + ext_ref[CONV_HALO - lag:CONV_HALO - lag + ts, :] * cw[tap:tap + 1, :]

    xcb = xc.astype(BF16)
    r = jax.nn.sigmoid(jnp.dot(xcb, wa_ref[...], preferred_element_type=F32) + ba_ref[...])
    ig = jax.nn.sigmoid(jnp.dot(xcb, wx_ref[...], preferred_element_type=F32) + bx_ref[...])
    lam = lam_ref[...]
    log_a_base = -(jnp.maximum(-lam, 0.0) + jnp.log1p(jnp.exp(-jnp.abs(lam))))
    log_a = LRU_C * r * log_a_base
    a = jnp.exp(log_a)
    mult = jnp.sqrt(-jnp.tanh(log_a) * (1.0 + a * a))
    b = mult * ig * xc

    groups = ts // SUBLANES
    a3 = a.reshape(groups, SUBLANES, C)
    b3 = b.reshape(groups, SUBLANES, C)
    sub = lax.broadcasted_iota(jnp.int32, a3.shape, 1)
    shift = 1
    while shift < SUBLANES:
        keep = sub >= shift
        b3 = jnp.where(keep, a3 * pltpu.roll(b3, shift, 1) + b3, b3)
        a3 = jnp.where(keep, a3 * pltpu.roll(a3, shift, 1), a3)
        shift *= 2
    a_ref[...] = a3.reshape(ts, C)
    b_ref[...] = b3.reshape(ts, C)

    def body(gi, carry):
        r0 = pl.multiple_of(gi * SUBLANES, SUBLANES)
        hg = a_ref[pl.ds(r0, SUBLANES), :] * carry + b_ref[pl.ds(r0, SUBLANES), :]
        h_ref[pl.ds(r0, SUBLANES), :] = hg
        return hg[SUBLANES - 1:SUBLANES, :]

    carry_ref[...] = lax.fori_loop(0, groups, body, carry_ref[...], unroll=8)

    gate = gate_ref[...].astype(F32)
    gelu = 0.5 * gate * (1.0 + jnp.tanh(0.7978845608028654 * (gate + 0.044715 * gate * gate * gate)))
    y_ref[...] = (h_ref[...] * gelu).astype(y_ref.dtype)


def rglru_mixer(proj, conv_w, conv_b, w_a_b, b_a, w_x_b, b_x, lam, *, batch, ts):
    T = proj.shape[0]
    NB, C, _ = w_a_b.shape
    W = NB * C
    steps = T // batch // ts
    vec = lambda: pl.BlockSpec((1, C), lambda b, c, s: (0, c))
    blk = lambda: pl.BlockSpec((None, C, C), lambda b, c, s: (c, 0, 0))
    return pl.pallas_call(
        functools.partial(_rglru_kernel, ts=ts),
        grid=(batch, NB, steps),
        in_specs=[pl.BlockSpec((ts, C), lambda b, c, s: (b * steps + s, NB + c)),
                  pl.BlockSpec((ts, C), lambda b, c, s: (b * steps + s, c)),
                  pl.BlockSpec((CONV_WIDTH, C), lambda b, c, s: (0, c)),
                  vec(), blk(), vec(), blk(), vec(), vec()],
        out_specs=pl.BlockSpec((ts, C), lambda b, c, s: (b * steps + s, c)),
        out_shape=jax.ShapeDtypeStruct((T, W), BF16),
        scratch_shapes=[pltpu.VMEM((ts + CONV_HALO, C), F32),
                        pltpu.VMEM((ts, C), F32), pltpu.VMEM((ts, C), F32),
                        pltpu.VMEM((ts, C), F32), pltpu.VMEM((1, C), F32)],
        compiler_params=_params(("parallel", "parallel", "arbitrary")),
        name="rglru_mixer",
    )(proj, proj, conv_w, conv_b.reshape(1, W), w_a_b, b_a.reshape(1, W), w_x_b,
      b_x.reshape(1, W), lam.reshape(1, W))


def _router_kernel(x_ref, w_ref, b_ref, id_ref, wt_ref):
    logits = jnp.dot(x_ref[...].astype(BF16), w_ref[...].astype(BF16),
                     preferred_element_type=F32) + b_ref[...]
    lane = lax.broadcasted_iota(jnp.int32, logits.shape, 1).astype(F32)
    neg = -jnp.inf
    big = float(LANES)

    is_group = lane < N_GROUPS
    gl = jnp.where(is_group, logits, neg)
    gmax = jnp.max(gl, axis=1, keepdims=True)
    g_idx = jnp.min(jnp.where(gl == gmax, lane, big), axis=1, keepdims=True)
    g_w = 1.0 / jnp.sum(jnp.where(is_group, jnp.exp(gl - gmax), 0.0), axis=1, keepdims=True)

    lo = N_GROUPS + g_idx * EXPERTS_PER_GROUP
    el = jnp.where(lane >= lo, jnp.where(lane < lo + EXPERTS_PER_GROUP, logits, neg), neg)
    v1 = jnp.max(el, axis=1, keepdims=True)
    i1 = jnp.min(jnp.where(el == v1, lane, big), axis=1, keepdims=True)
    el2 = jnp.where(lane == i1, neg, el)
    v2 = jnp.max(el2, axis=1, keepdims=True)
    i2 = jnp.min(jnp.where(el2 == v2, lane, big), axis=1, keepdims=True)
    t = jnp.exp(v2 - v1)
    w1 = g_w / (1.0 + t)
    w2 = w1 * t
    ids = jnp.where(lane == 0, i1 - N_GROUPS, jnp.where(lane == 1, i2 - N_GROUPS, 0.0))
    id_ref[...] = ids.astype(jnp.int32)
    wt_ref[...] = jnp.where(lane == 0, w1, jnp.where(lane == 1, w2, 0.0))


def router(x, w_route, b_route, *, tm):
    T, D = x.shape
    return pl.pallas_call(
        _router_kernel,
        grid=(T // tm,),
        in_specs=[pl.BlockSpec((tm, D), lambda i: (i, 0)),
                  pl.BlockSpec((D, LANES), lambda i: (0, 0)),
                  pl.BlockSpec((1, LANES), lambda i: (0, 0))],
        out_specs=[pl.BlockSpec((tm, LANES), lambda i: (i, 0)),
                   pl.BlockSpec((tm, LANES), lambda i: (i, 0))],
        out_shape=[jax.ShapeDtypeStruct((T, LANES), jnp.int32),
                   jax.ShapeDtypeStruct((T, LANES), F32)],
        compiler_params=_params(("parallel",)),
        name="router",
    )(x, w_route, b_route)


GATHER_UNROLL = SUBLANES


def _row_gather_start(src_hbm, dst_ref, sem, row_of, n_rows):
    def body(g, carry):
        for u in range(GATHER_UNROLL):
            r = g * GATHER_UNROLL + u
            pltpu.make_async_copy(src_hbm.at[pl.ds(row_of(r), 1), :],
                                  dst_ref.at[pl.ds(r, 1), :], sem).start()
        return carry
    lax.fori_loop(0, n_rows // GATHER_UNROLL, body, 0)


def _row_gather_wait(src_hbm, dst_ref, sem, n_rows):
    if not isinstance(n_rows, int):
        n_rows = pl.multiple_of(n_rows, GATHER_UNROLL)
    pltpu.make_async_copy(src_hbm.at[pl.ds(0, n_rows), :], dst_ref.at[pl.ds(0, n_rows), :],
                          sem).wait()


PLAN_EXPERT, PLAN_FIRST, PLAN_SLOT, PLAN_NEXT, PLAN_ROWS = range(5)


def _expert_weight_copies(w_hbm_list, buf_list, sem, layer, expert, slot):
    return [pltpu.make_async_copy(w.at[layer, expert], buf.at[slot], sem.at[slot])
            for w, buf in zip(w_hbm_list, buf_list)]


def _expert_weight_pipeline(plan_ref, n_used, w_hbm_list, buf_list, sem, layer):
    j = pl.program_id(0)
    slot = plan_ref[PLAN_SLOT, j]
    first = jnp.logical_and(j < n_used, plan_ref[PLAN_FIRST, j] == 1)

    @pl.when(j == 0)
    def _():
        for cp in _expert_weight_copies(w_hbm_list, buf_list, sem, layer,
                                        plan_ref[PLAN_EXPERT, 0], 0):
            cp.start()

    @pl.when(jnp.logical_and(first, plan_ref[PLAN_NEXT, j] >= 0))
    def _():
        for cp in _expert_weight_copies(w_hbm_list, buf_list, sem, layer,
                                        plan_ref[PLAN_NEXT, j], 1 - slot):
            cp.start()

    @pl.when(first)
    def _():
        for cp in _expert_weight_copies(w_hbm_list, buf_list, sem, layer,
                                        plan_ref[PLAN_EXPERT, j], slot):
            cp.wait()

    return slot


def _moe_up_kernel(plan_ref, rt_ref, nu_ref, x_hbm, w1_hbm, w3_hbm, h_ref,
                   xg_even, xg_odd, w1_buf, w3_buf, xsem, wsem, *, tm, layer):
    j = pl.program_id(0)
    n_used = nu_ref[0]
    xg = (xg_even, xg_odd)

    @pl.when(j == 0)
    def _():
        xg_even[...] = jnp.zeros_like(xg_even)
        xg_odd[...] = jnp.zeros_like(xg_odd)
        _row_gather_start(x_hbm, xg_even, xsem.at[0], lambda r: rt_ref[r],
                          plan_ref[PLAN_ROWS, 0])

    wslot = _expert_weight_pipeline(plan_ref, n_used, (w1_hbm, w3_hbm), (w1_buf, w3_buf),
                                    wsem, layer)

    def swiglu(parity):
        x = xg[parity][...].astype(BF16)
        a = jnp.dot(x, w1_buf[wslot].astype(BF16), preferred_element_type=F32)
        b = jnp.dot(x, w3_buf[wslot].astype(BF16), preferred_element_type=F32)
        h_ref[...] = (a * jax.nn.sigmoid(a) * b).astype(h_ref.dtype)

    for parity in (0, 1):
        mine = jnp.logical_and(j < n_used, j % 2 == parity)

        @pl.when(jnp.logical_and(mine, j + 1 < n_used))
        def _(parity=parity):
            _row_gather_start(x_hbm, xg[1 - parity], xsem.at[1 - parity],
                              lambda r: rt_ref[(j + 1) * tm + r], plan_ref[PLAN_ROWS, j + 1])

        @pl.when(mine)
        def _(parity=parity):
            _row_gather_wait(x_hbm, xg[parity], xsem.at[parity], plan_ref[PLAN_ROWS, j])
            swiglu(parity)

    @pl.when(j >= n_used)
    def _():
        h_ref[...] = jnp.zeros_like(h_ref)


def moe_up(x, w1, w3, plan, row_token, n_used, *, layer, tm):
    D = x.shape[1]
    FF = w1.shape[-1]
    n_tiles = plan.shape[1]
    any_spec = lambda: pl.BlockSpec(memory_space=pl.ANY)
    return pl.pallas_call(
        functools.partial(_moe_up_kernel, tm=tm, layer=layer),
        grid_spec=pltpu.PrefetchScalarGridSpec(
            num_scalar_prefetch=3,
            grid=(n_tiles,),
            in_specs=[any_spec(), any_spec(), any_spec()],
            out_specs=pl.BlockSpec((tm, FF), lambda j, plan, rt, nu: (j, 0)),
            scratch_shapes=[pltpu.VMEM((tm, D), F32), pltpu.VMEM((tm, D), F32),
                            pltpu.VMEM((2, D, FF), F32), pltpu.VMEM((2, D, FF), F32),
                            pltpu.SemaphoreType.DMA((2,)), pltpu.SemaphoreType.DMA((2,))]),
        out_shape=jax.ShapeDtypeStruct((n_tiles * tm, FF), BF16),
        compiler_params=_params(("arbitrary",)),
        name="moe_up",
    )(plan, row_token, n_used, x, w1, w3)


def _moe_down_kernel(plan_ref, nu_ref, h_ref, w2_hbm, y_ref, w2_buf, wsem, *, layer):
    j = pl.program_id(0)
    n_used = nu_ref[0]
    wslot = _expert_weight_pipeline(plan_ref, n_used, (w2_hbm,), (w2_buf,), wsem, layer)

    @pl.when(j < n_used)
    def _():
        y_ref[...] = jnp.dot(h_ref[...], w2_buf[wslot].astype(BF16),
                             preferred_element_type=F32)

    @pl.when(j >= n_used)
    def _():
        y_ref[...] = jnp.zeros_like(y_ref)


def moe_down(h, w2, plan, n_used, *, layer, tm):
    FF, D = w2.shape[-2:]
    n_tiles = plan.shape[1]
    tile = lambda j, plan, nu: (j, 0)
    return pl.pallas_call(
        functools.partial(_moe_down_kernel, layer=layer),
        grid_spec=pltpu.PrefetchScalarGridSpec(
            num_scalar_prefetch=2,
            grid=(n_tiles,),
            in_specs=[pl.BlockSpec((tm, FF), tile), pl.BlockSpec(memory_space=pl.ANY)],
            out_specs=pl.BlockSpec((tm, D), tile),
            scratch_shapes=[pltpu.VMEM((2, FF, D), F32), pltpu.SemaphoreType.DMA((2,))]),
        out_shape=jax.ShapeDtypeStruct((n_tiles * tm, D), F32),
        compiler_params=_params(("arbitrary",)),
        name="moe_down",
    )(plan, n_used, h, w2)


def _combine_ln_kernel(pos_ref, y_hbm, x_ref, wt_ref, g_ref, b_ref, o_ref, ob_ref,
                       yg_ref, sem, *, tm):
    i = pl.program_id(0)

    def gather(tile, slot):
        for k in range(TOP_K):
            _row_gather_start(y_hbm, yg_ref.at[slot, k], sem.at[slot],
                              lambda r: pos_ref[TOP_K * (tile * tm + r) + k], tm)

    @pl.when(i == 0)
    def _():
        gather(0, 0)

    @pl.when(i + 1 < pl.num_programs(0))
    def _():
        gather(i + 1, (i + 1) % 2)

    slot = i % 2
    for k in range(TOP_K):
        _row_gather_wait(y_hbm, yg_ref.at[slot, k], sem.at[slot], tm)
    wt = wt_ref[...]
    z = DN_ALPHA * x_ref[...]
    for k in range(TOP_K):
        z = z + wt[:, k:k + 1] * yg_ref[slot, k]
    y = _layer_norm_rows(z, g_ref[...], b_ref[...])
    o_ref[...] = y
    ob_ref[...] = y.astype(BF16)


def moe_combine_ln(y, pos, x, wts, g, b, *, tm):
    T, D = x.shape
    row = lambda i, pos: (i, 0)
    fixed = lambda i, pos: (0, 0)
    return pl.pallas_call(
        functools.partial(_combine_ln_kernel, tm=tm),
        grid_spec=pltpu.PrefetchScalarGridSpec(
            num_scalar_prefetch=1,
            grid=(T // tm,),
            in_specs=[pl.BlockSpec(memory_space=pl.ANY),
                      pl.BlockSpec((tm, D), row),
                      pl.BlockSpec((tm, LANES), row),
                      pl.BlockSpec((1, D), fixed),
                      pl.BlockSpec((1, D), fixed)],
            out_specs=[pl.BlockSpec((tm, D), row), pl.BlockSpec((tm, D), row)],
            scratch_shapes=[pltpu.VMEM((2, TOP_K, tm, D), F32),
                            pltpu.SemaphoreType.DMA((2,))]),
        out_shape=[jax.ShapeDtypeStruct((T, D), F32), jax.ShapeDtypeStruct((T, D), BF16)],
        compiler_params=_params(("arbitrary",)),
        name="moe_combine_ln",
    )(pos, y, x, wts, g.reshape(1, D), b.reshape(1, D))


def _round_up(v, multiple):
    return (v + multiple - 1) // multiple * multiple


def _dispatch_plan(ids, *, n_experts, tm, n_tiles):
    n_assign = ids.shape[0]
    experts = jnp.arange(n_experts, dtype=jnp.int32)
    onehot = (ids[:, None] == experts[None, :]).astype(jnp.int32)
    csum = jnp.cumsum(onehot, axis=0)
    rank = jnp.sum((csum - onehot) * onehot, axis=1)
    counts = csum[-1]
    tiles_per = (counts + tm - 1) // tm
    tile_end = jnp.cumsum(tiles_per)
    tile_start = tile_end - tiles_per
    n_used = tile_end[-1]
    pos = tile_start[ids] * tm + rank
    row_token = jnp.zeros((n_tiles * tm,), jnp.int32).at[pos].set(
        jnp.arange(n_assign, dtype=jnp.int32) // TOP_K)

    tile_ids = jnp.arange(n_tiles, dtype=jnp.int32)
    in_use = tile_ids < n_used
    tile_expert = jnp.sum((jnp.minimum(tile_ids, n_used - 1)[:, None] >= tile_end[None, :])
                          .astype(jnp.int32), axis=1)
    used = counts > 0
    ordinal = jnp.cumsum(used.astype(jnp.int32)) - 1
    later = lax.cummin(jnp.where(used, experts, n_experts), axis=0, reverse=True)
    next_used = jnp.concatenate([later[1:], jnp.full((1,), n_experts, jnp.int32)])
    next_used = jnp.where(next_used < n_experts, next_used, -1)
    plan = jnp.stack([
        tile_expert,
        jnp.logical_and(in_use, tile_ids == tile_start[tile_expert]).astype(jnp.int32),
        ordinal[tile_expert] % 2,
        next_used[tile_expert],
        jnp.where(in_use, _round_up(jnp.clip(
            counts[tile_expert] - (tile_ids - tile_start[tile_expert]) * tm, 0, tm),
            GATHER_UNROLL), 0),
    ]).astype(jnp.int32)
    return pos.astype(jnp.int32), row_token, plan, n_used.reshape(1).astype(jnp.int32)


def hierarchical_moe_ln(x, w_group, b_group, w_expert, b_expert, w1, w3, w2, g, b, *,
                        layer, tm_route, tm_moe, tm_comb):
    T, D = x.shape
    E = w1.shape[1]
    n_route = N_GROUPS + E
    w_route = jnp.zeros((D, LANES), F32).at[:, :N_GROUPS].set(w_group).at[:, N_GROUPS:n_route].set(w_expert)
    b_route = jnp.zeros((1, LANES), F32).at[0, :N_GROUPS].set(b_group).at[0, N_GROUPS:n_route].set(b_expert)
    ids, wts = router(x, w_route, b_route, tm=tm_route)
    n_tiles = (T * TOP_K + E * (tm_moe - 1)) // tm_moe
    pos, row_token, plan, n_used = _dispatch_plan(
        ids[:, :TOP_K].reshape(-1), n_experts=E, tm=tm_moe, n_tiles=n_tiles)
    h = moe_up(x, w1, w3, plan, row_token, n_used, layer=layer, tm=tm_moe)
    y = moe_down(h, w2, plan, n_used, layer=layer, tm=tm_moe)
    return moe_combine_ln(y, pos, x, wts, g, b, tm=tm_comb)


def kernel(x, even_w_in, even_w_pool, even_pool_scale, even_b_f, even_w_out, odd_w_in, odd_conv_w, odd_conv_b, odd_w_a, odd_b_a, odd_w_x, odd_b_x, odd_lambda, odd_w_out, moe_w_group, moe_b_group, moe_w_expert, moe_b_expert, moe_w1, moe_w3, moe_w2, ln_g, ln_b):
    B, S, D = x.shape
    T = B * S
    xf = x.reshape(T, D)
    xb = xf.astype(BF16)
    tiles = _tile_plan(T, S, D)

    for layer in range(DEPTH):
        i = layer // 2
        if layer % 2 == 0:
            pool_w = even_w_pool.shape[1] * even_w_pool.shape[2]
            heads = even_b_f.shape[1]
            fox_w = heads * HEAD_DIM
            w_in_t = jnp.swapaxes(even_w_in[i], 0, 1)
            proj = matmul(xb, w_in_t, n_cols=pool_w + 3 * fox_w, tm=tiles["mm_m"],
                          tn=tiles["mm_n"], out_dtype=BF16, w_is_transposed=True)
            lane0 = LANES - heads
            w_f = w_in_t[w_in_t.shape[0] - LANES:, :]
            b_f = jnp.zeros((1, LANES), F32).at[0, lane0:].set(even_b_f[i])
            c = forget_cumsum(xb, w_f, b_f, batch=B, tm=tiles["seq"])
            a_out = pool_mixer(proj, even_w_pool[i].astype(BF16), even_pool_scale[i],
                               batch=B, ts=tiles["seq"])
            qc = pool_w // HEAD_DIM
            b_out = forgetting_attention(proj, c, batch=B, heads=heads, head_lane0=lane0,
                                         q_col=qc, k_col=qc + heads, v_col=qc + 2 * heads,
                                         tq=tiles["attn_q"], tk=tiles["attn_k"])
            mixed = [a_out, b_out]
            w_out = even_w_out[i].astype(BF16)
        else:
            proj = matmul(xb, odd_w_in[i], n_cols=odd_w_in.shape[2], tm=tiles["mm_m"],
                          tn=tiles["mm_n"], out_dtype=BF16)
            mixed = [rglru_mixer(proj, odd_conv_w[i], odd_conv_b[i], odd_w_a[i].astype(BF16),
                                 odd_b_a[i].reshape(-1), odd_w_x[i].astype(BF16),
                                 odd_b_x[i].reshape(-1), odd_lambda[i], batch=B, ts=tiles["seq"])]
            w_out = odd_w_out[i].astype(BF16)
        xf = matmul_residual_ln(mixed, w_out, xf, ln_g[layer, 0], ln_b[layer, 0],
                                tm=tiles["ln_m"])
        xf, xb = hierarchical_moe_ln(xf, moe_w_group[layer], moe_b_group[layer],
                                     moe_w_expert[layer], moe_b_expert[layer], moe_w1, moe_w3,
                                     moe_w2, ln_g[layer, 1], ln_b[layer, 1], layer=layer,
                                     tm_route=tiles["route"], tm_moe=tiles["moe"],
                                     tm_comb=tiles["comb"])
    return xf.reshape(B, S, D)


def _tile_plan(T, S, D):
    return {
        "mm_m": min(1024, T), "mm_n": min(512, D),
        "seq": min(512, S), "attn_q": min(1024, S), "attn_k": min(512, S),
        "ln_m": min(256, T),
        "route": min(512, T), "moe": 256, "comb": min(256, T),
    }
```

```python
import functools

import jax
import jax.numpy as jnp
from jax import lax
from jax.experimental import pallas as pl
from jax.experimental.pallas import tpu as pltpu

F32 = jnp.float32
BF16 = jnp.bfloat16

POOL_WINDOWS = (2, 4, 8, 16)
POOL_HALO = 16
HEAD_DIM = 128
LRU_BLOCKS = 16
CONV_WIDTH = 4
CONV_HALO = 8
LRU_C = 8.0
N_GROUPS = 4
EXPERTS_PER_GROUP = 8
TOP_K = 2
DEPTH = 2
DN_ALPHA = (2.0 * DEPTH) ** 0.25
LN_EPS = 1e-5

LANES = 128
SUBLANES = 8
VMEM_LIMIT_MB = 56
OUT_PROJ_VMEM_MB = 60


def _params(semantics, vmem_mb=VMEM_LIMIT_MB):
    return pltpu.CompilerParams(dimension_semantics=semantics,
                                vmem_limit_bytes=vmem_mb * 1024 * 1024)


_CONTRACT_LAST = (((1,), (1,)), ((), ()))


def _mm_kernel(a_ref, w_hbm, o_ref, w_stage, w_bf, sem, *, w_is_transposed, tn):
    j = pl.program_id(0)
    i = pl.program_id(1)

    def w_copy(block):
        cols = pl.ds(pl.multiple_of(block * tn, tn), tn)
        src = w_hbm.at[cols, :] if w_is_transposed else w_hbm.at[:, cols]
        return pltpu.make_async_copy(src, w_stage, sem)

    @pl.when(jnp.logical_and(j == 0, i == 0))
    def _():
        w_copy(0).start()

    @pl.when(i == 0)
    def _():
        w_copy(j).wait()
        w_bf[...] = w_stage[...].astype(BF16)

        @pl.when(j + 1 < pl.num_programs(0))
        def _():
            w_copy(j + 1).start()

    if w_is_transposed:
        out = lax.dot_general(a_ref[...], w_bf[...], _CONTRACT_LAST, preferred_element_type=F32)
    else:
        out = jnp.dot(a_ref[...], w_bf[...], preferred_element_type=F32)
    o_ref[...] = out.astype(o_ref.dtype)


def matmul(a, w, *, n_cols, tm, tn, out_dtype, w_is_transposed=False):
    M, K = a.shape
    w_block = (tn, K) if w_is_transposed else (K, tn)
    return pl.pallas_call(
        functools.partial(_mm_kernel, w_is_transposed=w_is_transposed, tn=tn),
        grid=(n_cols // tn, M // tm),
        in_specs=[pl.BlockSpec((tm, K), lambda j, i: (i, 0)),
                  pl.BlockSpec(memory_space=pl.ANY)],
        out_specs=pl.BlockSpec((tm, tn), lambda j, i: (i, j)),
        out_shape=jax.ShapeDtypeStruct((M, n_cols), out_dtype),
        scratch_shapes=[pltpu.VMEM(w_block, F32), pltpu.VMEM(w_block, BF16),
                        pltpu.SemaphoreType.DMA(())],
        compiler_params=_params(("arbitrary", "arbitrary")),
        name="proj_matmul",
    )(a, w)


def _layer_norm_rows(z, g, b):
    mu = jnp.mean(z, axis=-1, keepdims=True)
    d = z - mu
    var = jnp.mean(d * d, axis=-1, keepdims=True)
    return d * lax.rsqrt(var + LN_EPS) * g + b


LN_CHUNK_ROWS = 64


def _mm_ln_kernel(*refs, n_a):
    a_refs = refs[:n_a]
    w_hbm, x_hbm, g_ref, b_ref, o_ref, w_buf, acc_ref, sem = refs[n_a:]
    i = pl.program_id(0)
    tm = o_ref.shape[0]

    @pl.when(i == 0)
    def _():
        w_copy = pltpu.make_async_copy(w_hbm, w_buf, sem.at[0])
        w_copy.start()
        w_copy.wait()

    x_copy = pltpu.make_async_copy(x_hbm.at[pl.ds(pl.multiple_of(i * tm, tm), tm), :], o_ref,
                                   sem.at[1])
    x_copy.start()
    k_each = a_refs[0].shape[1]
    for idx, a_ref in enumerate(a_refs):
        part = jnp.dot(a_ref[...], w_buf[idx * k_each:(idx + 1) * k_each, :],
                       preferred_element_type=F32)
        if idx == 0:
            acc_ref[...] = part
        else:
            acc_ref[...] += part
    x_copy.wait()

    def normalise(c, carry):
        rows = pl.ds(pl.multiple_of(c * LN_CHUNK_ROWS, LN_CHUNK_ROWS), LN_CHUNK_ROWS)
        o_ref[rows, :] = _layer_norm_rows(DN_ALPHA * o_ref[rows, :] + acc_ref[rows, :],
                                          g_ref[...], b_ref[...])
        return carry

    lax.fori_loop(0, tm // LN_CHUNK_ROWS, normalise, 0)


def matmul_residual_ln(a_list, w, x, g, b, *, tm):
    M, K_each = a_list[0].shape
    K, N = w.shape
    n_a = len(a_list)
    assert n_a * K_each == K and tm % LN_CHUNK_ROWS == 0
    return pl.pallas_call(
        functools.partial(_mm_ln_kernel, n_a=n_a),
        grid=(M // tm,),
        in_specs=[pl.BlockSpec((tm, K_each), lambda i: (i, 0)) for _ in range(n_a)] + [
            pl.BlockSpec(memory_space=pl.ANY),
            pl.BlockSpec(memory_space=pl.ANY),
            pl.BlockSpec((1, N), lambda i: (0, 0)),
            pl.BlockSpec((1, N), lambda i: (0, 0))],
        out_specs=pl.BlockSpec((tm, N), lambda i: (i, 0)),
        out_shape=jax.ShapeDtypeStruct((M, N), F32),
        scratch_shapes=[pltpu.VMEM((K, N), BF16), pltpu.VMEM((tm, N), F32),
                        pltpu.SemaphoreType.DMA((2,))],
        compiler_params=_params(("arbitrary",), vmem_mb=OUT_PROJ_VMEM_MB),
        name="out_proj_ln",
    )(*a_list, w, x, g.reshape(1, N), b.reshape(1, N))


def _cumsum_rows(v, n_rows):
    row = lax.broadcasted_iota(jnp.int32, v.shape, 0)
    shift = 1
    while shift < n_rows:
        v = v + jnp.where(row >= shift, pltpu.roll(v, shift, 0), 0.0)
        shift *= 2
    return v


def _forget_kernel(x_ref, w_ref, bf_ref, c_ref, xb_ref, carry_ref, *, tm):
    @pl.when(pl.program_id(1) == 0)
    def _():
        carry_ref[...] = jnp.zeros_like(carry_ref)

    xb = x_ref[...].astype(BF16)
    xb_ref[...] = xb
    z = lax.dot_general(xb, w_ref[...].astype(BF16), _CONTRACT_LAST,
                        preferred_element_type=F32) + bf_ref[...]
    log_f = jnp.minimum(z, 0.0) - jnp.log1p(jnp.exp(-jnp.abs(z)))
    c = _cumsum_rows(log_f, tm) + carry_ref[...]
    c_ref[...] = c
    carry_ref[...] = c[tm - 1:tm, :]


def forget_cumsum(x, w_f, b_f, *, batch, tm):
    T, D = x.shape
    steps = T // batch // tm
    return pl.pallas_call(
        functools.partial(_forget_kernel, tm=tm),
        grid=(batch, steps),
        in_specs=[pl.BlockSpec((tm, D), lambda b, s: (b * steps + s, 0)),
                  pl.BlockSpec((LANES, D), lambda b, s: (0, 0)),
                  pl.BlockSpec((1, LANES), lambda b, s: (0, 0))],
        out_specs=[pl.BlockSpec((tm, LANES), lambda b, s: (b * steps + s, 0)),
                   pl.BlockSpec((tm, D), lambda b, s: (b * steps + s, 0))],
        out_shape=[jax.ShapeDtypeStruct((T, LANES), F32), jax.ShapeDtypeStruct((T, D), BF16)],
        scratch_shapes=[pltpu.VMEM((1, LANES), F32)],
        compiler_params=_params(("parallel", "arbitrary")),
        name="forget_cumsum",
    )(x, w_f, b_f)


def _pool_kernel(u_ref, w_ref, sc_ref, o_ref, ext_ref, *, ts):
    g = pl.program_id(1)
    si = pl.program_id(2)

    @pl.when(si == 0)
    def _():
        ext_ref[0:POOL_HALO, :] = jnp.zeros((POOL_HALO, ext_ref.shape[1]), F32)

    @pl.when(si > 0)
    def _():
        ext_ref[0:POOL_HALO, :] = ext_ref[ts:ts + POOL_HALO, :]

    u = u_ref[...].astype(F32)
    ext_ref[POOL_HALO:, :] = u
    e1 = ext_ref[...]
    e2 = e1 + pltpu.roll(e1, 1, 0)
    e4 = e2 + pltpu.roll(e2, 2, 0)
    e8 = e4 + pltpu.roll(e4, 4, 0)
    e16 = e8 + pltpu.roll(e8, 8, 0)
    win = jnp.where(g == 0, e2, jnp.where(g == 1, e4, jnp.where(g == 2, e8, e16)))
    win = win[POOL_HALO:, :]
    width = jnp.left_shift(2, g)
    t = si * ts + lax.broadcasted_iota(jnp.int32, (ts, 1), 0)
    cnt = jnp.minimum(t + 1, width).astype(F32)
    pooled = win / cnt - u
    y = jnp.dot(pooled.astype(BF16), w_ref[...], preferred_element_type=F32)
    o_ref[...] = (y * sc_ref[...]).astype(o_ref.dtype)


def pool_mixer(proj, w_pool_b, pool_scale, *, batch, ts):
    T = proj.shape[0]
    G, C, _ = w_pool_b.shape
    assert POOL_WINDOWS == (2, 4, 8, 16) and G == len(POOL_WINDOWS)
    steps = T // batch // ts
    return pl.pallas_call(
        functools.partial(_pool_kernel, ts=ts),
        grid=(batch, G, steps),
        in_specs=[pl.BlockSpec((ts, C), lambda b, g, s: (b * steps + s, g)),
                  pl.BlockSpec((None, C, C), lambda b, g, s: (g, 0, 0)),
                  pl.BlockSpec((1, C), lambda b, g, s: (0, g))],
        out_specs=pl.BlockSpec((ts, C), lambda b, g, s: (b * steps + s, g)),
        out_shape=jax.ShapeDtypeStruct((T, G * C), BF16),
        scratch_shapes=[pltpu.VMEM((ts + POOL_HALO, C), F32)],
        compiler_params=_params(("parallel", "parallel", "arbitrary")),
        name="pool_mixer",
    )(proj, w_pool_b, pool_scale.reshape(1, G * C))


LOG2E = 1.4426950408889634
N_BIAS_PARTS = 3


def _attn_kernel(q_ref, k_ref, v_ref, c_ref, o_ref, kaug_ref, vaug_ref, qext_ref,
                 qaug_ref, sa_ref, sb_ref, m_ref, acc_ref, *, tq, tk, scale, head_lane0):
    h = pl.program_id(1)
    qi = pl.program_id(2)
    S = k_ref.shape[0]

    @pl.when(qi == 0)
    def _():
        lane = lax.broadcasted_iota(jnp.int32, (S, LANES), 1)
        c = jnp.sum(jnp.where(lane == head_lane0 + h, c_ref[...], 0.0), axis=1,
                    keepdims=True) * LOG2E
        hi = c.astype(BF16).astype(F32)
        mid = (c - hi).astype(BF16).astype(F32)
        lo = c - hi - mid
        ones = jnp.where(lane < N_BIAS_PARTS, 1.0, 0.0)
        k_ext = jnp.where(lane == 3, -hi, jnp.where(lane == 4, -mid, jnp.where(lane == 5, -lo, ones)))
        q_ext = jnp.where(lane == 0, hi, jnp.where(lane == 1, mid, jnp.where(
            lane == 2, lo, jnp.where(lane < 2 * N_BIAS_PARTS, 1.0, 0.0))))
        kaug_ref[:, :HEAD_DIM] = k_ref[...]
        kaug_ref[:, HEAD_DIM:] = k_ext.astype(BF16)
        vaug_ref[:, :HEAD_DIM] = v_ref[...]
        vaug_ref[:, HEAD_DIM:] = jnp.where(lane == 0, 1.0, 0.0).astype(BF16)
        qext_ref[...] = q_ext.astype(BF16)

    q0 = pl.multiple_of(qi * tq, tq)
    qaug_ref[:, :HEAD_DIM] = (q_ref[...].astype(F32) * (scale * LOG2E)).astype(BF16)
    qaug_ref[:, HEAD_DIM:] = qext_ref[pl.ds(q0, tq), :]
    m_ref[...] = jnp.full(m_ref.shape, -jnp.inf, F32)
    acc_ref[...] = jnp.zeros(acc_ref.shape, F32)

    def scores(block, lo=0):
        k0 = pl.multiple_of(block * tk, tk)
        return lax.dot_general(qaug_ref[lo:, :], kaug_ref[pl.ds(k0, tk), :],
                               (((1,), (1,)), ((), ())), preferred_element_type=F32)

    def softmax_pv(s, block, lo=0):
        k0 = pl.multiple_of(block * tk, tk)
        m = m_ref[lo:, :]
        m_new = jnp.maximum(m, jnp.max(s, axis=1, keepdims=True))
        alpha = jnp.exp2(m - m_new)
        p = jnp.exp2((s - m_new).astype(BF16))
        acc_ref[lo:, :] = alpha * acc_ref[lo:, :] + jnp.dot(
            p, vaug_ref[pl.ds(k0, tk), :], preferred_element_type=F32)
        m_ref[lo:, :] = m_new

    n_full = qi * (tq // tk)
    sa_ref[...] = scores(0)

    def body(i, carry):
        j = 2 * i
        sb_ref[...] = scores(j + 1)
        softmax_pv(sa_ref[...], j)
        sa_ref[...] = scores(j + 2)
        softmax_pv(sb_ref[...], j + 1)
        return carry

    lax.fori_loop(0, n_full // 2, body, 0)

    diag_scores = [sa_ref[...]] + [scores(n_full + u, u * tk) for u in range(1, tq // tk)]
    for u, s in enumerate(diag_scores):
        lo = u * tk
        row = lax.broadcasted_iota(jnp.int32, s.shape, 0)
        col = lax.broadcasted_iota(jnp.int32, s.shape, 1)
        softmax_pv(jnp.where(col <= row, s, -jnp.inf), n_full + u, lo)
        acc = acc_ref[lo:lo + tk, :]
        o_ref[lo:lo + tk, :] = (acc[:, :HEAD_DIM] / acc[:, HEAD_DIM:HEAD_DIM + 1]
                                ).astype(o_ref.dtype)


def forgetting_attention(proj, c, *, batch, heads, head_lane0, q_col, k_col, v_col, tq, tk):
    T = proj.shape[0]
    S = T // batch
    nq = S // tq
    assert (tq // tk) % 2 == 0, "full key blocks are consumed two per loop trip"
    return pl.pallas_call(
        functools.partial(_attn_kernel, tq=tq, tk=tk, scale=HEAD_DIM ** -0.5,
                          head_lane0=head_lane0),
        grid=(batch, heads, nq),
        in_specs=[pl.BlockSpec((tq, HEAD_DIM), lambda b, h, i: (b * nq + i, q_col + h)),
                  pl.BlockSpec((S, HEAD_DIM), lambda b, h, i: (b, k_col + h)),
                  pl.BlockSpec((S, HEAD_DIM), lambda b, h, i: (b, v_col + h)),
                  pl.BlockSpec((S, LANES), lambda b, h, i: (b, 0))],
        out_specs=pl.BlockSpec((tq, HEAD_DIM), lambda b, h, i: (b * nq + i, h)),
        out_shape=jax.ShapeDtypeStruct((T, heads * HEAD_DIM), BF16),
        scratch_shapes=[pltpu.VMEM((S, 2 * HEAD_DIM), BF16), pltpu.VMEM((S, 2 * HEAD_DIM), BF16),
                        pltpu.VMEM((S, LANES), BF16), pltpu.VMEM((tq, 2 * HEAD_DIM), BF16),
                        pltpu.VMEM((tq, tk), F32), pltpu.VMEM((tq, tk), F32),
                        pltpu.VMEM((tq, 1), F32), pltpu.VMEM((tq, 2 * HEAD_DIM), F32)],
        compiler_params=_params(("parallel", "parallel", "arbitrary")),
        name="forgetting_attention",
    )(proj, proj, proj, c)


def _sigmoid(z):
    return 0.5 * jnp.tanh(0.5 * z) + 0.5


def _rglru_kernel(xb_ref, gate_ref, cw_ref, cb_ref, wa_ref, ba_ref, wx_ref, bx_ref, lam_ref,
                  y_ref, ext_ref, a_ref, b_ref, h_ref, carry_ref, *, ts):
    si = pl.program_id(2)
    C = xb_ref.shape[1]

    @pl.when(si == 0)
    def _():
        ext_ref[0:CONV_HALO, :] = jnp.zeros((CONV_HALO, C), F32)
        carry_ref[...] = jnp.zeros_like(carry_ref)

    @pl.when(si > 0)
    def _():
        ext_ref[0:CONV_HALO, :] = ext_ref[ts:ts + CONV_HALO, :]

    xb = xb_ref[...].astype(F32)
    ext_ref[CONV_HALO:, :] = xb
    cw = cw_ref[...]
    xc = cb_ref[...] + xb * cw[CONV_WIDTH - 1:CONV_WIDTH, :]
    for lag in range(1, CONV_WIDTH):
        tap = CONV_WIDTH - 1 - lag
        xc = xc + ext_ref[CONV_HALO - lag:CONV_HALO - lag + ts, :] * cw[tap:tap + 1, :]

    xcb = xc.astype(BF16)
    r = _sigmoid(jnp.dot(xcb, wa_ref[...], preferred_element_type=F32) + ba_ref[...])
    ig = _sigmoid(jnp.dot(xcb, wx_ref[...], preferred_element_type=F32) + bx_ref[...])
    lam = lam_ref[...]
    log_a_base = -(jnp.maximum(-lam, 0.0) + jnp.log1p(jnp.exp(-jnp.abs(lam))))
    log_a = LRU_C * r * log_a_base
    a = jnp.exp(log_a)
    mult = jnp.exp2(0.5 * jnp.log2(-jnp.tanh(log_a) * (1.0 + a * a)))
    b = mult * ig * xc

    groups = ts // SUBLANES
    a3 = a.reshape(groups, SUBLANES, C)
    b3 = b.reshape(groups, SUBLANES, C)
    sub = lax.broadcasted_iota(jnp.int32, a3.shape, 1)
    shift = 1
    while shift < SUBLANES:
        keep = sub >= shift
        b3 = jnp.where(keep, a3 * pltpu.roll(b3, shift, 1) + b3, b3)
        a3 = jnp.where(keep, a3 * pltpu.roll(a3, shift, 1), a3)
        shift *= 2
    a_ref[...] = a3.reshape(ts, C)
    b_ref[...] = b3.reshape(ts, C)

    def body(gi, carry):
        r0 = pl.multiple_of(gi * SUBLANES, SUBLANES)
        hg = a_ref[pl.ds(r0, SUBLANES), :] * carry + b_ref[pl.ds(r0, SUBLANES), :]
        h_ref[pl.ds(r0, SUBLANES), :] = hg
        return hg[SUBLANES - 1:SUBLANES, :]

    carry_ref[...] = lax.fori_loop(0, groups, body, carry_ref[...], unroll=8)

    gate = gate_ref[...].astype(F32)
    gelu = 0.5 * gate * (1.0 + jnp.tanh(0.7978845608028654 * (gate + 0.044715 * gate * gate * gate)))
    y_ref[...] = (h_ref[...] * gelu).astype(y_ref.dtype)


def rglru_mixer(proj, conv_w, conv_b, w_a_b, b_a, w_x_b, b_x, lam, *, batch, ts):
    T = proj.shape[0]
    NB, C, _ = w_a_b.shape
    W = NB * C
    steps = T // batch // ts
    vec = lambda: pl.BlockSpec((1, C), lambda b, c, s: (0, c))
    blk = lambda: pl.BlockSpec((None, C, C), lambda b, c, s: (c, 0, 0))
    return pl.pallas_call(
        functools.partial(_rglru_kernel, ts=ts),
        grid=(batch, NB, steps),
        in_specs=[pl.BlockSpec((ts, C), lambda b, c, s: (b * steps + s, NB + c)),
                  pl.BlockSpec((ts, C), lambda b, c, s: (b * steps + s, c)),
                  pl.BlockSpec((CONV_WIDTH, C), lambda b, c, s: (0, c)),
                  vec(), blk(), vec(), blk(), vec(), vec()],
        out_specs=pl.BlockSpec((ts, C), lambda b, c, s: (b * steps + s, c)),
        out_shape=jax.ShapeDtypeStruct((T, W), BF16),
        scratch_shapes=[pltpu.VMEM((ts + CONV_HALO, C), F32),
                        pltpu.VMEM((ts, C), F32), pltpu.VMEM((ts, C), F32),
                        pltpu.VMEM((ts, C), F32), pltpu.VMEM((1, C), F32)],
        compiler_params=_params(("parallel", "parallel", "arbitrary")),
        name="rglru_mixer",
    )(proj, proj, conv_w, conv_b.reshape(1, W), w_a_b, b_a.reshape(1, W), w_x_b,
      b_x.reshape(1, W), lam.reshape(1, W))


def _router_kernel(x_ref, w_ref, b_ref, id_ref, wt_ref):
    logits = jnp.dot(x_ref[...].astype(BF16), w_ref[...].astype(BF16),
                     preferred_element_type=F32) + b_ref[...]
    lane = lax.broadcasted_iota(jnp.int32, logits.shape, 1).astype(F32)
    neg = -jnp.inf
    big = float(LANES)

    is_group = lane < N_GROUPS
    gl = jnp.where(is_group, logits, neg)
    gmax = jnp.max(gl, axis=1, keepdims=True)
    g_idx = jnp.min(jnp.where(gl == gmax, lane, big), axis=1, keepdims=True)
    g_w = 1.0 / jnp.sum(jnp.where(is_group, jnp.exp(gl - gmax), 0.0), axis=1, keepdims=True)

    lo = N_GROUPS + g_idx * EXPERTS_PER_GROUP
    el = jnp.where(lane >= lo, jnp.where(lane < lo + EXPERTS_PER_GROUP, logits, neg), neg)
    v1 = jnp.max(el, axis=1, keepdims=True)
    i1 = jnp.min(jnp.where(el == v1, lane, big), axis=1, keepdims=True)
    el2 = jnp.where(lane == i1, neg, el)
    v2 = jnp.max(el2, axis=1, keepdims=True)
    i2 = jnp.min(jnp.where(el2 == v2, lane, big), axis=1, keepdims=True)
    t = jnp.exp(v2 - v1)
    w1 = g_w / (1.0 + t)
    w2 = w1 * t
    ids = jnp.where(lane == 0, i1 - N_GROUPS, jnp.where(lane == 1, i2 - N_GROUPS, 0.0))
    id_ref[...] = ids.astype(jnp.int32)
    wt_ref[...] = jnp.where(lane == 0, w1, jnp.where(lane == 1, w2, 0.0))


def router(x, w_route, b_route, *, tm):
    T, D = x.shape
    return pl.pallas_call(
        _router_kernel,
        grid=(T // tm,),
        in_specs=[pl.BlockSpec((tm, D), lambda i: (i, 0)),
                  pl.BlockSpec((D, LANES), lambda i: (0, 0)),
                  pl.BlockSpec((1, LANES), lambda i: (0, 0))],
        out_specs=[pl.BlockSpec((tm, LANES), lambda i: (i, 0)),
                   pl.BlockSpec((tm, LANES), lambda i: (i, 0))],
        out_shape=[jax.ShapeDtypeStruct((T, LANES), jnp.int32),
                   jax.ShapeDtypeStruct((T, LANES), F32)],
        compiler_params=_params(("parallel",)),
        name="router",
    )(x, w_route, b_route)


GATHER_UNROLL = SUBLANES


def _row_gather_start(src_hbm, dst_ref, sem, row_of, n_rows):
    def body(g, carry):
        for u in range(GATHER_UNROLL):
            r = g * GATHER_UNROLL + u
            pltpu.make_async_copy(src_hbm.at[pl.ds(row_of(r), 1), :],
                                  dst_ref.at[pl.ds(r, 1), :], sem).start()
        return carry
    lax.fori_loop(0, n_rows // GATHER_UNROLL, body, 0)


def _row_gather_wait(src_hbm, dst_ref, sem, n_rows):
    if not isinstance(n_rows, int):
        n_rows = pl.multiple_of(n_rows, GATHER_UNROLL)
    pltpu.make_async_copy(src_hbm.at[pl.ds(0, n_rows), :], dst_ref.at[pl.ds(0, n_rows), :],
                          sem).wait()


PLAN_EXPERT, PLAN_FIRST, PLAN_SLOT, PLAN_NEXT, PLAN_ROWS = range(5)


def _expert_weight_copies(w_hbm_list, buf_list, sem, layer, expert, slot):
    return [pltpu.make_async_copy(w.at[layer, expert], buf.at[slot], sem.at[slot])
            for w, buf in zip(w_hbm_list, buf_list)]


def _expert_weight_pipeline(plan_ref, n_used, w_hbm_list, buf_list, sem, layer):
    j = pl.program_id(0)
    slot = plan_ref[PLAN_SLOT, j]
    first = jnp.logical_and(j < n_used, plan_ref[PLAN_FIRST, j] == 1)

    @pl.when(j == 0)
    def _():
        for cp in _expert_weight_copies(w_hbm_list, buf_list, sem, layer,
                                        plan_ref[PLAN_EXPERT, 0], 0):
            cp.start()

    @pl.when(jnp.logical_and(first, plan_ref[PLAN_NEXT, j] >= 0))
    def _():
        for cp in _expert_weight_copies(w_hbm_list, buf_list, sem, layer,
                                        plan_ref[PLAN_NEXT, j], 1 - slot):
            cp.start()

    @pl.when(first)
    def _():
        for cp in _expert_weight_copies(w_hbm_list, buf_list, sem, layer,
                                        plan_ref[PLAN_EXPERT, j], slot):
            cp.wait()

    return slot


def _moe_up_kernel(plan_ref, rt_ref, nu_ref, x_hbm, w1_hbm, w3_hbm, h_ref,
                   xg_even, xg_odd, w1_buf, w3_buf, xsem, wsem, *, tm, layer):
    j = pl.program_id(0)
    n_used = nu_ref[0]
    xg = (xg_even, xg_odd)

    @pl.when(j == 0)
    def _():
        xg_even[...] = jnp.zeros_like(xg_even)
        xg_odd[...] = jnp.zeros_like(xg_odd)
        _row_gather_start(x_hbm, xg_even, xsem.at[0], lambda r: rt_ref[r],
                          plan_ref[PLAN_ROWS, 0])

    wslot = _expert_weight_pipeline(plan_ref, n_used, (w1_hbm, w3_hbm), (w1_buf, w3_buf),
                                    wsem, layer)

    def swiglu(parity):
        x = xg[parity][...].astype(BF16)
        a = jnp.dot(x, w1_buf[wslot].astype(BF16), preferred_element_type=F32)
        b = jnp.dot(x, w3_buf[wslot].astype(BF16), preferred_element_type=F32)
        h_ref[...] = (a * jax.nn.sigmoid(a) * b).astype(h_ref.dtype)

    for parity in (0, 1):
        mine = jnp.logical_and(j < n_used, j % 2 == parity)

        @pl.when(jnp.logical_and(mine, j + 1 < n_used))
        def _(parity=parity):
            _row_gather_start(x_hbm, xg[1 - parity], xsem.at[1 - parity],
                              lambda r: rt_ref[(j + 1) * tm + r], plan_ref[PLAN_ROWS, j + 1])

        @pl.when(mine)
        def _(parity=parity):
            _row_gather_wait(x_hbm, xg[parity], xsem.at[parity], plan_ref[PLAN_ROWS, j])
            swiglu(parity)

    @pl.when(j >= n_used)
    def _():
        h_ref[...] = jnp.zeros_like(h_ref)


def moe_up(x, w1, w3, plan, row_token, n_used, *, layer, tm):
    D = x.shape[1]
    FF = w1.shape[-1]
    n_tiles = plan.shape[1]
    any_spec = lambda: pl.BlockSpec(memory_space=pl.ANY)
    return pl.pallas_call(
        functools.partial(_moe_up_kernel, tm=tm, layer=layer),
        grid_spec=pltpu.PrefetchScalarGridSpec(
            num_scalar_prefetch=3,
            grid=(n_tiles,),
            in_specs=[any_spec(), any_spec(), any_spec()],
            out_specs=pl.BlockSpec((tm, FF), lambda j, plan, rt, nu: (j, 0)),
            scratch_shapes=[pltpu.VMEM((tm, D), F32), pltpu.VMEM((tm, D), F32),
                            pltpu.VMEM((2, D, FF), F32), pltpu.VMEM((2, D, FF), F32),
                            pltpu.SemaphoreType.DMA((2,)), pltpu.SemaphoreType.DMA((2,))]),
        out_shape=jax.ShapeDtypeStruct((n_tiles * tm, FF), BF16),
        compiler_params=_params(("arbitrary",)),
        name="moe_up",
    )(plan, row_token, n_used, x, w1, w3)


def _moe_down_kernel(plan_ref, nu_ref, h_ref, w2_hbm, y_ref, w2_buf, wsem, *, layer):
    j = pl.program_id(0)
    n_used = nu_ref[0]
    wslot = _expert_weight_pipeline(plan_ref, n_used, (w2_hbm,), (w2_buf,), wsem, layer)

    @pl.when(j < n_used)
    def _():
        y_ref[...] = jnp.dot(h_ref[...], w2_buf[wslot].astype(BF16),
                             preferred_element_type=F32)

    @pl.when(j >= n_used)
    def _():
        y_ref[...] = jnp.zeros_like(y_ref)


def moe_down(h, w2, plan, n_used, *, layer, tm):
    FF, D = w2.shape[-2:]
    n_tiles = plan.shape[1]
    tile = lambda j, plan, nu: (j, 0)
    return pl.pallas_call(
        functools.partial(_moe_down_kernel, layer=layer),
        grid_spec=pltpu.PrefetchScalarGridSpec(
            num_scalar_prefetch=2,
            grid=(n_tiles,),
            in_specs=[pl.BlockSpec((tm, FF), tile), pl.BlockSpec(memory_space=pl.ANY)],
            out_specs=pl.BlockSpec((tm, D), tile),
            scratch_shapes=[pltpu.VMEM((2, FF, D), F32), pltpu.SemaphoreType.DMA((2,))]),
        out_shape=jax.ShapeDtypeStruct((n_tiles * tm, D), F32),
        compiler_params=_params(("arbitrary",)),
        name="moe_down",
    )(plan, n_used, h, w2)


def _combine_ln_kernel(pos_ref, y_hbm, x_ref, wt_ref, g_ref, b_ref, o_ref, *rest, tm):
    ob_ref = rest[0] if len(rest) == 3 else None
    yg_ref, sem = rest[-2:]
    i = pl.program_id(0)

    def gather(tile, slot):
        for k in range(TOP_K):
            _row_gather_start(y_hbm, yg_ref.at[slot, k], sem.at[slot],
                              lambda r: pos_ref[TOP_K * (tile * tm + r) + k], tm)

    @pl.when(i == 0)
    def _():
        gather(0, 0)

    @pl.when(i + 1 < pl.num_programs(0))
    def _():
        gather(i + 1, (i + 1) % 2)

    slot = i % 2
    for k in range(TOP_K):
        _row_gather_wait(y_hbm, yg_ref.at[slot, k], sem.at[slot], tm)
    wt = wt_ref[...]
    z = DN_ALPHA * x_ref[...]
    for k in range(TOP_K):
        z = z + wt[:, k:k + 1] * yg_ref[slot, k]
    y = _layer_norm_rows(z, g_ref[...], b_ref[...])
    o_ref[...] = y
    if ob_ref is not None:
        ob_ref[...] = y.astype(BF16)


def moe_combine_ln(y, pos, x, wts, g, b, *, tm, with_bf16_copy):
    T, D = x.shape
    row = lambda i, pos: (i, 0)
    fixed = lambda i, pos: (0, 0)
    n_out = 2 if with_bf16_copy else 1
    return pl.pallas_call(
        functools.partial(_combine_ln_kernel, tm=tm),
        grid_spec=pltpu.PrefetchScalarGridSpec(
            num_scalar_prefetch=1,
            grid=(T // tm,),
            in_specs=[pl.BlockSpec(memory_space=pl.ANY),
                      pl.BlockSpec((tm, D), row),
                      pl.BlockSpec((tm, LANES), row),
                      pl.BlockSpec((1, D), fixed),
                      pl.BlockSpec((1, D), fixed)],
            out_specs=[pl.BlockSpec((tm, D), row)] * n_out,
            scratch_shapes=[pltpu.VMEM((2, TOP_K, tm, D), F32),
                            pltpu.SemaphoreType.DMA((2,))]),
        out_shape=[jax.ShapeDtypeStruct((T, D), F32), jax.ShapeDtypeStruct((T, D), BF16)][:n_out],
        compiler_params=_params(("arbitrary",)),
        name="moe_combine_ln",
    )(pos, y, x, wts, g.reshape(1, D), b.reshape(1, D))


def _round_up(v, multiple):
    return (v + multiple - 1) // multiple * multiple


def _dispatch_plan(ids, *, n_experts, tm, n_tiles):
    n_assign = ids.shape[0]
    experts = jnp.arange(n_experts, dtype=jnp.int32)
    onehot = (ids[:, None] == experts[None, :]).astype(jnp.int32)
    csum = jnp.cumsum(onehot, axis=0)
    rank = jnp.sum((csum - onehot) * onehot, axis=1)
    counts = csum[-1]
    tiles_per = (counts + tm - 1) // tm
    tile_end = jnp.cumsum(tiles_per)
    tile_start = tile_end - tiles_per
    n_used = tile_end[-1]
    pos = tile_start[ids] * tm + rank
    row_token = jnp.zeros((n_tiles * tm,), jnp.int32).at[pos].set(
        jnp.arange(n_assign, dtype=jnp.int32) // TOP_K)

    tile_ids = jnp.arange(n_tiles, dtype=jnp.int32)
    in_use = tile_ids < n_used
    tile_expert = jnp.sum((jnp.minimum(tile_ids, n_used - 1)[:, None] >= tile_end[None, :])
                          .astype(jnp.int32), axis=1)
    used = counts > 0
    ordinal = jnp.cumsum(used.astype(jnp.int32)) - 1
    later = lax.cummin(jnp.where(used, experts, n_experts), axis=0, reverse=True)
    next_used = jnp.concatenate([later[1:], jnp.full((1,), n_experts, jnp.int32)])
    next_used = jnp.where(next_used < n_experts, next_used, -1)
    plan = jnp.stack([
        tile_expert,
        jnp.logical_and(in_use, tile_ids == tile_start[tile_expert]).astype(jnp.int32),
        ordinal[tile_expert] % 2,
        next_used[tile_expert],
        jnp.where(in_use, _round_up(jnp.clip(
            counts[tile_expert] - (tile_ids - tile_start[tile_expert]) * tm, 0, tm),
            GATHER_UNROLL), 0),
    ]).astype(jnp.int32)
    return pos.astype(jnp.int32), row_token, plan, n_used.reshape(1).astype(jnp.int32)


def hierarchical_moe_ln(x, w_group, b_group, w_expert, b_expert, w1, w3, w2, g, b, *,
                        layer, with_bf16_copy, tm_route, tm_moe, tm_comb):
    T, D = x.shape
    E = w1.shape[1]
    n_route = N_GROUPS + E
    w_route = jnp.zeros((D, LANES), F32).at[:, :N_GROUPS].set(w_group).at[:, N_GROUPS:n_route].set(w_expert)
    b_route = jnp.zeros((1, LANES), F32).at[0, :N_GROUPS].set(b_group).at[0, N_GROUPS:n_route].set(b_expert)
    ids, wts = router(x, w_route, b_route, tm=tm_route)
    n_tiles = (T * TOP_K + E * (tm_moe - 1)) // tm_moe
    pos, row_token, plan, n_used = _dispatch_plan(
        ids[:, :TOP_K].reshape(-1), n_experts=E, tm=tm_moe, n_tiles=n_tiles)
    h = moe_up(x, w1, w3, plan, row_token, n_used, layer=layer, tm=tm_moe)
    y = moe_down(h, w2, plan, n_used, layer=layer, tm=tm_moe)
    out = moe_combine_ln(y, pos, x, wts, g, b, tm=tm_comb, with_bf16_copy=with_bf16_copy)
    return out[0], (out[1] if with_bf16_copy else None)


def kernel(x, even_w_in, even_w_pool, even_pool_scale, even_b_f, even_w_out, odd_w_in, odd_conv_w, odd_conv_b, odd_w_a, odd_b_a, odd_w_x, odd_b_x, odd_lambda, odd_w_out, moe_w_group, moe_b_group, moe_w_expert, moe_b_expert, moe_w1, moe_w3, moe_w2, ln_g, ln_b):
    B, S, D = x.shape
    T = B * S
    xf = x.reshape(T, D)
    xb = None
    tiles = _tile_plan(T, S, D)

    for layer in range(DEPTH):
        i = layer // 2
        if layer % 2 == 0:
            pool_w = even_w_pool.shape[1] * even_w_pool.shape[2]
            heads = even_b_f.shape[1]
            fox_w = heads * HEAD_DIM
            w_in_t = jnp.swapaxes(even_w_in[i], 0, 1)
            lane0 = LANES - heads
            w_f = w_in_t[w_in_t.shape[0] - LANES:, :]
            b_f = jnp.zeros((1, LANES), F32).at[0, lane0:].set(even_b_f[i])
            c, xb = forget_cumsum(xf, w_f, b_f, batch=B, tm=tiles["seq"])
            proj = matmul(xb, w_in_t, n_cols=pool_w + 3 * fox_w, tm=tiles["mm_m"],
                          tn=tiles["mm_n"], out_dtype=BF16, w_is_transposed=True)
            a_out = pool_mixer(proj, even_w_pool[i].astype(BF16), even_pool_scale[i],
                               batch=B, ts=tiles["seq"])
            qc = pool_w // HEAD_DIM
            b_out = forgetting_attention(proj, c, batch=B, heads=heads, head_lane0=lane0,
                                         q_col=qc, k_col=qc + heads, v_col=qc + 2 * heads,
                                         tq=tiles["attn_q"], tk=tiles["attn_k"])
            mixed = [a_out, b_out]
            w_out = even_w_out[i].astype(BF16)
        else:
            if xb is None:
                xb = xf.astype(BF16)
            proj = matmul(xb, odd_w_in[i], n_cols=odd_w_in.shape[2], tm=tiles["mm_m"],
                          tn=tiles["mm_n"], out_dtype=BF16)
            mixed = [rglru_mixer(proj, odd_conv_w[i], odd_conv_b[i], odd_w_a[i].astype(BF16),
                                 odd_b_a[i].reshape(-1), odd_w_x[i].astype(BF16),
                                 odd_b_x[i].reshape(-1), odd_lambda[i], batch=B, ts=tiles["seq"])]
            w_out = odd_w_out[i].astype(BF16)
        xf = matmul_residual_ln(mixed, w_out, xf, ln_g[layer, 0], ln_b[layer, 0],
                                tm=tiles["ln_m"])
        xf, xb = hierarchical_moe_ln(xf, moe_w_group[layer], moe_b_group[layer],
                                     moe_w_expert[layer], moe_b_expert[layer], moe_w1, moe_w3,
                                     moe_w2, ln_g[layer, 1], ln_b[layer, 1], layer=layer,
                                     with_bf16_copy=layer + 1 < DEPTH,
                                     tm_route=tiles["route"], tm_moe=tiles["moe"],
                                     tm_comb=tiles["comb"])
    return xf.reshape(B, S, D)


def _tile_plan(T, S, D):
    return {
        "mm_m": min(1024, T), "mm_n": min(1024, D),
        "seq": min(512, S), "attn_q": min(1024, S), "attn_k": min(512, S),
        "ln_m": min(256, T),
        "route": min(512, T), "moe": 256, "comb": min(256, T),
    }
```

```python
import functools

import jax
import jax.numpy as jnp
from jax import lax
from jax.experimental import pallas as pl
from jax.experimental.pallas import tpu as pltpu

F32 = jnp.float32
BF16 = jnp.bfloat16

POOL_WINDOWS = (2, 4, 8, 16)
POOL_HALO = 16
HEAD_DIM = 128
LRU_BLOCKS = 16
CONV_WIDTH = 4
CONV_HALO = 8
LRU_C = 8.0
N_GROUPS = 4
EXPERTS_PER_GROUP = 8
TOP_K = 2
DEPTH = 2
DN_ALPHA = (2.0 * DEPTH) ** 0.25
LN_EPS = 1e-5

LANES = 128
SUBLANES = 8
VMEM_LIMIT_MB = 56
OUT_PROJ_VMEM_MB = 60


def _params(semantics, vmem_mb=VMEM_LIMIT_MB):
    return pltpu.CompilerParams(dimension_semantics=semantics,
                                vmem_limit_bytes=vmem_mb * 1024 * 1024)


_CONTRACT_LAST = (((1,), (1,)), ((), ()))


def _mm_kernel(a_ref, w_hbm, o_ref, w_stage, w_bf, sem, *, w_is_transposed, tn):
    j = pl.program_id(0)
    i = pl.program_id(1)

    def w_copy(block):
        cols = pl.ds(pl.multiple_of(block * tn, tn), tn)
        src = w_hbm.at[cols, :] if w_is_transposed else w_hbm.at[:, cols]
        return pltpu.make_async_copy(src, w_stage, sem)

    @pl.when(jnp.logical_and(j == 0, i == 0))
    def _():
        w_copy(0).start()

    @pl.when(i == 0)
    def _():
        w_copy(j).wait()
        w_bf[...] = w_stage[...].astype(BF16)

        @pl.when(j + 1 < pl.num_programs(0))
        def _():
            w_copy(j + 1).start()

    if w_is_transposed:
        out = lax.dot_general(a_ref[...], w_bf[...], _CONTRACT_LAST, preferred_element_type=F32)
    else:
        out = jnp.dot(a_ref[...], w_bf[...], preferred_element_type=F32)
    o_ref[...] = out.astype(o_ref.dtype)


def matmul(a, w, *, n_cols, tm, tn, out_dtype, w_is_transposed=False):
    M, K = a.shape
    w_block = (tn, K) if w_is_transposed else (K, tn)
    return pl.pallas_call(
        functools.partial(_mm_kernel, w_is_transposed=w_is_transposed, tn=tn),
        grid=(n_cols // tn, M // tm),
        in_specs=[pl.BlockSpec((tm, K), lambda j, i: (i, 0)),
                  pl.BlockSpec(memory_space=pl.ANY)],
        out_specs=pl.BlockSpec((tm, tn), lambda j, i: (i, j)),
        out_shape=jax.ShapeDtypeStruct((M, n_cols), out_dtype),
        scratch_shapes=[pltpu.VMEM(w_block, F32), pltpu.VMEM(w_block, BF16),
                        pltpu.SemaphoreType.DMA(())],
        compiler_params=_params(("arbitrary", "arbitrary")),
        name="proj_matmul",
    )(a, w)


def _layer_norm_rows(z, g, b):
    mu = jnp.mean(z, axis=-1, keepdims=True)
    d = z - mu
    var = jnp.mean(d * d, axis=-1, keepdims=True)
    return d * lax.rsqrt(var + LN_EPS) * g + b


LN_CHUNK_ROWS = 64


def _row_major_store_wait(rows_hbm, stage_ref, sem, n_rows):
    def body(g, carry):
        for u in range(SUBLANES):
            pltpu.make_async_copy(stage_ref.at[0, :, pl.ds(u, 1), :], rows_hbm.at[0], sem).wait()
        return carry
    lax.fori_loop(0, n_rows // SUBLANES, body, 0)


def _row_major_store_start(rows_hbm, stage_ref, sem, row0, n_rows):
    def body(g, carry):
        for u in range(SUBLANES):
            pltpu.make_async_copy(stage_ref.at[g, :, pl.ds(u, 1), :],
                                  rows_hbm.at[row0 + g * SUBLANES + u], sem).start()
        return carry
    lax.fori_loop(0, n_rows // SUBLANES, body, 0)


def _mm_ln_kernel(*refs, n_a):
    a_refs = refs[:n_a]
    w_hbm, x_hbm, g_ref, b_ref, o_ref, rows_hbm, w_buf, acc_ref, stage_ref, sem = refs[n_a:]
    i = pl.program_id(0)
    tm = o_ref.shape[0]

    @pl.when(i == 0)
    def _():
        w_copy = pltpu.make_async_copy(w_hbm, w_buf, sem.at[0])
        w_copy.start()
        w_copy.wait()

    x_copy = pltpu.make_async_copy(x_hbm.at[pl.ds(pl.multiple_of(i * tm, tm), tm), :], o_ref,
                                   sem.at[1])
    x_copy.start()
    k_each = a_refs[0].shape[1]
    for idx, a_ref in enumerate(a_refs):
        part = jnp.dot(a_ref[...], w_buf[idx * k_each:(idx + 1) * k_each, :],
                       preferred_element_type=F32)
        if idx == 0:
            acc_ref[...] = part
        else:
            acc_ref[...] += part
    x_copy.wait()

    @pl.when(i > 0)
    def _():
        _row_major_store_wait(rows_hbm, stage_ref, sem.at[2], tm)

    groups = LN_CHUNK_ROWS // SUBLANES

    def normalise(c, carry):
        rows = pl.ds(pl.multiple_of(c * LN_CHUNK_ROWS, LN_CHUNK_ROWS), LN_CHUNK_ROWS)
        y = _layer_norm_rows(DN_ALPHA * o_ref[rows, :] + acc_ref[rows, :],
                             g_ref[...], b_ref[...])
        o_ref[rows, :] = y
        for lt in range(y.shape[1] // LANES):
            stage_ref[pl.ds(c * groups, groups), lt] = (
                y[:, lt * LANES:(lt + 1) * LANES].reshape(groups, SUBLANES, LANES))
        return carry

    lax.fori_loop(0, tm // LN_CHUNK_ROWS, normalise, 0)
    _row_major_store_start(rows_hbm, stage_ref, sem.at[2], i * tm, tm)

    @pl.when(i == pl.num_programs(0) - 1)
    def _():
        _row_major_store_wait(rows_hbm, stage_ref, sem.at[2], tm)


def matmul_residual_ln(a_list, w, x, g, b, *, tm):
    M, K_each = a_list[0].shape
    K, N = w.shape
    n_a = len(a_list)
    assert n_a * K_each == K and tm % LN_CHUNK_ROWS == 0
    return pl.pallas_call(
        functools.partial(_mm_ln_kernel, n_a=n_a),
        grid=(M // tm,),
        in_specs=[pl.BlockSpec((tm, K_each), lambda i: (i, 0)) for _ in range(n_a)] + [
            pl.BlockSpec(memory_space=pl.ANY),
            pl.BlockSpec(memory_space=pl.ANY),
            pl.BlockSpec((1, N), lambda i: (0, 0)),
            pl.BlockSpec((1, N), lambda i: (0, 0))],
        out_specs=[pl.BlockSpec((tm, N), lambda i: (i, 0)), pl.BlockSpec(memory_space=pl.ANY)],
        out_shape=[jax.ShapeDtypeStruct((M, N), F32),
                   jax.ShapeDtypeStruct((M, N // LANES, 1, LANES), F32)],
        scratch_shapes=[pltpu.VMEM((K, N), BF16), pltpu.VMEM((tm, N), F32),
                        pltpu.VMEM((tm // SUBLANES, N // LANES, SUBLANES, LANES), F32),
                        pltpu.SemaphoreType.DMA((3,))],
        compiler_params=_params(("arbitrary",), vmem_mb=OUT_PROJ_VMEM_MB),
        name="out_proj_ln",
    )(*a_list, w, x, g.reshape(1, N), b.reshape(1, N))


def _cumsum_rows(v, n_rows):
    row = lax.broadcasted_iota(jnp.int32, v.shape, 0)
    shift = 1
    while shift < n_rows:
        v = v + jnp.where(row >= shift, pltpu.roll(v, shift, 0), 0.0)
        shift *= 2
    return v


def _forget_kernel(x_ref, w_ref, bf_ref, c_ref, xb_ref, carry_ref, *, tm):
    @pl.when(pl.program_id(1) == 0)
    def _():
        carry_ref[...] = jnp.zeros_like(carry_ref)

    xb = x_ref[...].astype(BF16)
    xb_ref[...] = xb
    z = lax.dot_general(xb, w_ref[...].astype(BF16), _CONTRACT_LAST,
                        preferred_element_type=F32) + bf_ref[...]
    log_f = jnp.minimum(z, 0.0) - jnp.log1p(jnp.exp(-jnp.abs(z)))
    c = _cumsum_rows(log_f, tm) + carry_ref[...]
    c_ref[...] = c
    carry_ref[...] = c[tm - 1:tm, :]


def forget_cumsum(x, w_f, b_f, *, batch, tm):
    T, D = x.shape
    steps = T // batch // tm
    return pl.pallas_call(
        functools.partial(_forget_kernel, tm=tm),
        grid=(batch, steps),
        in_specs=[pl.BlockSpec((tm, D), lambda b, s: (b * steps + s, 0)),
                  pl.BlockSpec((LANES, D), lambda b, s: (0, 0)),
                  pl.BlockSpec((1, LANES), lambda b, s: (0, 0))],
        out_specs=[pl.BlockSpec((tm, LANES), lambda b, s: (b * steps + s, 0)),
                   pl.BlockSpec((tm, D), lambda b, s: (b * steps + s, 0))],
        out_shape=[jax.ShapeDtypeStruct((T, LANES), F32), jax.ShapeDtypeStruct((T, D), BF16)],
        scratch_shapes=[pltpu.VMEM((1, LANES), F32)],
        compiler_params=_params(("parallel", "arbitrary")),
        name="forget_cumsum",
    )(x, w_f, b_f)


def _pool_kernel(u_ref, w_ref, sc_ref, o_ref, ext_ref, *, ts):
    g = pl.program_id(1)
    si = pl.program_id(2)

    @pl.when(si == 0)
    def _():
        ext_ref[0:POOL_HALO, :] = jnp.zeros((POOL_HALO, ext_ref.shape[1]), F32)

    @pl.when(si > 0)
    def _():
        ext_ref[0:POOL_HALO, :] = ext_ref[ts:ts + POOL_HALO, :]

    u = u_ref[...].astype(F32)
    ext_ref[POOL_HALO:, :] = u
    e1 = ext_ref[...]
    e2 = e1 + pltpu.roll(e1, 1, 0)
    e4 = e2 + pltpu.roll(e2, 2, 0)
    e8 = e4 + pltpu.roll(e4, 4, 0)
    e16 = e8 + pltpu.roll(e8, 8, 0)
    win = jnp.where(g == 0, e2, jnp.where(g == 1, e4, jnp.where(g == 2, e8, e16)))
    win = win[POOL_HALO:, :]
    width = jnp.left_shift(2, g)
    t = si * ts + lax.broadcasted_iota(jnp.int32, (ts, 1), 0)
    cnt = jnp.minimum(t + 1, width).astype(F32)
    pooled = win / cnt - u
    y = jnp.dot(pooled.astype(BF16), w_ref[...], preferred_element_type=F32)
    o_ref[...] = (y * sc_ref[...]).astype(o_ref.dtype)


def pool_mixer(proj, w_pool_b, pool_scale, *, batch, ts):
    T = proj.shape[0]
    G, C, _ = w_pool_b.shape
    assert POOL_WINDOWS == (2, 4, 8, 16) and G == len(POOL_WINDOWS)
    steps = T // batch // ts
    return pl.pallas_call(
        functools.partial(_pool_kernel, ts=ts),
        grid=(batch, G, steps),
        in_specs=[pl.BlockSpec((ts, C), lambda b, g, s: (b * steps + s, g)),
                  pl.BlockSpec((None, C, C), lambda b, g, s: (g, 0, 0)),
                  pl.BlockSpec((1, C), lambda b, g, s: (0, g))],
        out_specs=pl.BlockSpec((ts, C), lambda b, g, s: (b * steps + s, g)),
        out_shape=jax.ShapeDtypeStruct((T, G * C), BF16),
        scratch_shapes=[pltpu.VMEM((ts + POOL_HALO, C), F32)],
        compiler_params=_params(("parallel", "parallel", "arbitrary")),
        name="pool_mixer",
    )(proj, w_pool_b, pool_scale.reshape(1, G * C))


LOG2E = 1.4426950408889634
N_BIAS_PARTS = 3


def _attn_kernel(q_ref, k_ref, v_ref, c_ref, o_ref, kaug_ref, vaug_ref, qext_ref,
                 qaug_ref, sa_ref, sb_ref, m_ref, acc_ref, *, tq, tk, scale, head_lane0):
    h = pl.program_id(1)
    qi = pl.program_id(2)
    S = k_ref.shape[0]

    @pl.when(qi == 0)
    def _():
        lane = lax.broadcasted_iota(jnp.int32, (S, LANES), 1)
        c = jnp.sum(jnp.where(lane == head_lane0 + h, c_ref[...], 0.0), axis=1,
                    keepdims=True) * LOG2E
        hi = c.astype(BF16).astype(F32)
        mid = (c - hi).astype(BF16).astype(F32)
        lo = c - hi - mid
        ones = jnp.where(lane < N_BIAS_PARTS, 1.0, 0.0)
        k_ext = jnp.where(lane == 3, -hi, jnp.where(lane == 4, -mid, jnp.where(lane == 5, -lo, ones)))
        q_ext = jnp.where(lane == 0, hi, jnp.where(lane == 1, mid, jnp.where(
            lane == 2, lo, jnp.where(lane < 2 * N_BIAS_PARTS, 1.0, 0.0))))
        kaug_ref[:, :HEAD_DIM] = k_ref[...]
        kaug_ref[:, HEAD_DIM:] = k_ext.astype(BF16)
        vaug_ref[:, :HEAD_DIM] = v_ref[...]
        vaug_ref[:, HEAD_DIM:] = jnp.where(lane == 0, 1.0, 0.0).astype(BF16)
        qext_ref[...] = q_ext.astype(BF16)

    q0 = pl.multiple_of(qi * tq, tq)
    qaug_ref[:, :HEAD_DIM] = (q_ref[...].astype(F32) * (scale * LOG2E)).astype(BF16)
    qaug_ref[:, HEAD_DIM:] = qext_ref[pl.ds(q0, tq), :]
    m_ref[...] = jnp.full(m_ref.shape, -jnp.inf, F32)
    acc_ref[...] = jnp.zeros(acc_ref.shape, F32)

    def scores(block, lo=0):
        k0 = pl.multiple_of(block * tk, tk)
        return lax.dot_general(qaug_ref[lo:, :], kaug_ref[pl.ds(k0, tk), :],
                               (((1,), (1,)), ((), ())), preferred_element_type=F32)

    def softmax_pv(s, block, lo=0):
        k0 = pl.multiple_of(block * tk, tk)
        m = m_ref[lo:, :]
        m_new = jnp.maximum(m, jnp.max(s, axis=1, keepdims=True))
        alpha = jnp.exp2(m - m_new)
        p = jnp.exp2((s - m_new).astype(BF16))
        acc_ref[lo:, :] = alpha * acc_ref[lo:, :] + jnp.dot(
            p, vaug_ref[pl.ds(k0, tk), :], preferred_element_type=F32)
        m_ref[lo:, :] = m_new

    n_full = qi * (tq // tk)
    sa_ref[...] = scores(0)

    def body(i, carry):
        j = 2 * i
        sb_ref[...] = scores(j + 1)
        softmax_pv(sa_ref[...], j)
        sa_ref[...] = scores(j + 2)
        softmax_pv(sb_ref[...], j + 1)
        return carry

    lax.fori_loop(0, n_full // 2, body, 0)

    diag_scores = [sa_ref[...]] + [scores(n_full + u, u * tk) for u in range(1, tq // tk)]
    for u, s in enumerate(diag_scores):
        lo = u * tk
        row = lax.broadcasted_iota(jnp.int32, s.shape, 0)
        col = lax.broadcasted_iota(jnp.int32, s.shape, 1)
        softmax_pv(jnp.where(col <= row, s, -jnp.inf), n_full + u, lo)
        acc = acc_ref[lo:lo + tk, :]
        o_ref[lo:lo + tk, :] = (acc[:, :HEAD_DIM] / acc[:, HEAD_DIM:HEAD_DIM + 1]
                                ).astype(o_ref.dtype)


def forgetting_attention(proj, c, *, batch, heads, head_lane0, q_col, k_col, v_col, tq, tk):
    T = proj.shape[0]
    S = T // batch
    nq = S // tq
    assert (tq // tk) % 2 == 0, "full key blocks are consumed two per loop trip"
    return pl.pallas_call(
        functools.partial(_attn_kernel, tq=tq, tk=tk, scale=HEAD_DIM ** -0.5,
                          head_lane0=head_lane0),
        grid=(batch, heads, nq),
        in_specs=[pl.BlockSpec((tq, HEAD_DIM), lambda b, h, i: (b * nq + i, q_col + h)),
                  pl.BlockSpec((S, HEAD_DIM), lambda b, h, i: (b, k_col + h)),
                  pl.BlockSpec((S, HEAD_DIM), lambda b, h, i: (b, v_col + h)),
                  pl.BlockSpec((S, LANES), lambda b, h, i: (b, 0))],
        out_specs=pl.BlockSpec((tq, HEAD_DIM), lambda b, h, i: (b * nq + i, h)),
        out_shape=jax.ShapeDtypeStruct((T, heads * HEAD_DIM), BF16),
        scratch_shapes=[pltpu.VMEM((S, 2 * HEAD_DIM), BF16), pltpu.VMEM((S, 2 * HEAD_DIM), BF16),
                        pltpu.VMEM((S, LANES), BF16), pltpu.VMEM((tq, 2 * HEAD_DIM), BF16),
                        pltpu.VMEM((tq, tk), F32), pltpu.VMEM((tq, tk), F32),
                        pltpu.VMEM((tq, 1), F32), pltpu.VMEM((tq, 2 * HEAD_DIM), F32)],
        compiler_params=_params(("parallel", "parallel", "arbitrary")),
        name="forgetting_attention",
    )(proj, proj, proj, c)


def _sigmoid(z):
    return 0.5 * jnp.tanh(0.5 * z) + 0.5


def _rglru_kernel(xb_ref, gate_ref, cw_ref, cb_ref, wa_ref, ba_ref, wx_ref, bx_ref, lam_ref,
                  y_ref, ext_ref, a_ref, b_ref, h_ref, carry_ref, *, ts):
    si = pl.program_id(2)
    C = xb_ref.shape[1]

    @pl.when(si == 0)
    def _():
        ext_ref[0:CONV_HALO, :] = jnp.zeros((CONV_HALO, C), F32)
        carry_ref[...] = jnp.zeros_like(carry_ref)

    @pl.when(si > 0)
    def _():
        ext_ref[0:CONV_HALO, :] = ext_ref[ts:ts + CONV_HALO, :]

    xb = xb_ref[...].astype(F32)
    ext_ref[CONV_HALO:, :] = xb
    cw = cw_ref[...]
    xc = cb_ref[...] + xb * cw[CONV_WIDTH - 1:CONV_WIDTH, :]
    for lag in range(1, CONV_WIDTH):
        tap = CONV_WIDTH - 1 - lag
        xc = xc + ext_ref[CONV_HALO - lag:CONV_HALO - lag + ts, :] * cw[tap:tap + 1, :]

    xcb = xc.astype(BF16)
    r = _sigmoid(jnp.dot(xcb, wa_ref[...], preferred_element_type=F32) + ba_ref[...])
    ig = _sigmoid(jnp.dot(xcb, wx_ref[...], preferred_element_type=F32) + bx_ref[...])
    lam = lam_ref[...]
    log_a_base = -(jnp.maximum(-lam, 0.0) + jnp.log1p(jnp.exp(-jnp.abs(lam))))
    log_a = LRU_C * r * log_a_base
    a = jnp.exp(log_a)
    mult = jnp.exp2(0.5 * jnp.log2(-jnp.tanh(log_a) * (1.0 + a * a)))
    b = mult * ig * xc

    groups = ts // SUBLANES
    a3 = a.reshape(groups, SUBLANES, C)
    b3 = b.reshape(groups, SUBLANES, C)
    sub = lax.broadcasted_iota(jnp.int32, a3.shape, 1)
    shift = 1
    while shift < SUBLANES:
        keep = sub >= shift
        b3 = jnp.where(keep, a3 * pltpu.roll(b3, shift, 1) + b3, b3)
        a3 = jnp.where(keep, a3 * pltpu.roll(a3, shift, 1), a3)
        shift *= 2
    a_ref[...] = a3.reshape(ts, C)
    b_ref[...] = b3.reshape(ts, C)

    def body(gi, carry):
        r0 = pl.multiple_of(gi * SUBLANES, SUBLANES)
        hg = a_ref[pl.ds(r0, SUBLANES), :] * carry + b_ref[pl.ds(r0, SUBLANES), :]
        h_ref[pl.ds(r0, SUBLANES), :] = hg
        return hg[SUBLANES - 1:SUBLANES, :]

    carry_ref[...] = lax.fori_loop(0, groups, body, carry_ref[...], unroll=8)

    gate = gate_ref[...].astype(F32)
    gelu = 0.5 * gate * (1.0 + jnp.tanh(0.7978845608028654 * (gate + 0.044715 * gate * gate * gate)))
    y_ref[...] = (h_ref[...] * gelu).astype(y_ref.dtype)


def rglru_mixer(proj, conv_w, conv_b, w_a_b, b_a, w_x_b, b_x, lam, *, batch, ts):
    T = proj.shape[0]
    NB, C, _ = w_a_b.shape
    W = NB * C
    steps = T // batch // ts
    vec = lambda: pl.BlockSpec((1, C), lambda b, c, s: (0, c))
    blk = lambda: pl.BlockSpec((None, C, C), lambda b, c, s: (c, 0, 0))
    return pl.pallas_call(
        functools.partial(_rglru_kernel, ts=ts),
        grid=(batch, NB, steps),
        in_specs=[pl.BlockSpec((ts, C), lambda b, c, s: (b * steps + s, NB + c)),
                  pl.BlockSpec((ts, C), lambda b, c, s: (b * steps + s, c)),
                  pl.BlockSpec((CONV_WIDTH, C), lambda b, c, s: (0, c)),
                  vec(), blk(), vec(), blk(), vec(), vec()],
        out_specs=pl.BlockSpec((ts, C), lambda b, c, s: (b * steps + s, c)),
        out_shape=jax.ShapeDtypeStruct((T, W), BF16),
        scratch_shapes=[pltpu.VMEM((ts + CONV_HALO, C), F32),
                        pltpu.VMEM((ts, C), F32), pltpu.VMEM((ts, C), F32),
                        pltpu.VMEM((ts, C), F32), pltpu.VMEM((1, C), F32)],
        compiler_params=_params(("parallel", "parallel", "arbitrary")),
        name="rglru_mixer",
    )(proj, proj, conv_w, conv_b.reshape(1, W), w_a_b, b_a.reshape(1, W), w_x_b,
      b_x.reshape(1, W), lam.reshape(1, W))


def _router_kernel(x_ref, w_ref, b_ref, id_ref, wt_ref):
    logits = jnp.dot(x_ref[...].astype(BF16), w_ref[...].astype(BF16),
                     preferred_element_type=F32) + b_ref[...]
    lane = lax.broadcasted_iota(jnp.int32, logits.shape, 1).astype(F32)
    neg = -jnp.inf
    big = float(LANES)

    is_group = lane < N_GROUPS
    gl = jnp.where(is_group, logits, neg)
    gmax = jnp.max(gl, axis=1, keepdims=True)
    g_idx = jnp.min(jnp.where(gl == gmax, lane, big), axis=1, keepdims=True)
    g_w = 1.0 / jnp.sum(jnp.where(is_group, jnp.exp(gl - gmax), 0.0), axis=1, keepdims=True)

    lo = N_GROUPS + g_idx * EXPERTS_PER_GROUP
    el = jnp.where(lane >= lo, jnp.where(lane < lo + EXPERTS_PER_GROUP, logits, neg), neg)
    v1 = jnp.max(el, axis=1, keepdims=True)
    i1 = jnp.min(jnp.where(el == v1, lane, big), axis=1, keepdims=True)
    el2 = jnp.where(lane == i1, neg, el)
    v2 = jnp.max(el2, axis=1, keepdims=True)
    i2 = jnp.min(jnp.where(el2 == v2, lane, big), axis=1, keepdims=True)
    t = jnp.exp(v2 - v1)
    w1 = g_w / (1.0 + t)
    w2 = w1 * t
    ids = jnp.where(lane == 0, i1 - N_GROUPS, jnp.where(lane == 1, i2 - N_GROUPS, 0.0))
    id_ref[...] = ids.astype(jnp.int32)
    wt_ref[...] = jnp.where(lane == 0, w1, jnp.where(lane == 1, w2, 0.0))


def router(x, w_route, b_route, *, tm):
    T, D = x.shape
    return pl.pallas_call(
        _router_kernel,
        grid=(T // tm,),
        in_specs=[pl.BlockSpec((tm, D), lambda i: (i, 0)),
                  pl.BlockSpec((D, LANES), lambda i: (0, 0)),
                  pl.BlockSpec((1, LANES), lambda i: (0, 0))],
        out_specs=[pl.BlockSpec((tm, LANES), lambda i: (i, 0)),
                   pl.BlockSpec((tm, LANES), lambda i: (i, 0))],
        out_shape=[jax.ShapeDtypeStruct((T, LANES), jnp.int32),
                   jax.ShapeDtypeStruct((T, LANES), F32)],
        compiler_params=_params(("parallel",)),
        name="router",
    )(x, w_route, b_route)


GATHER_UNROLL = SUBLANES


def _row_gather_start(src_hbm, dst_ref, sem, row_of, n_rows):
    def body(g, carry):
        for u in range(GATHER_UNROLL):
            r = g * GATHER_UNROLL + u
            pltpu.make_async_copy(src_hbm.at[pl.ds(row_of(r), 1), :],
                                  dst_ref.at[pl.ds(r, 1), :], sem).start()
        return carry
    lax.fori_loop(0, n_rows // GATHER_UNROLL, body, 0)


def _row_gather_wait(src_hbm, dst_ref, sem, n_rows):
    if not isinstance(n_rows, int):
        n_rows = pl.multiple_of(n_rows, GATHER_UNROLL)
    pltpu.make_async_copy(src_hbm.at[pl.ds(0, n_rows), :], dst_ref.at[pl.ds(0, n_rows), :],
                          sem).wait()


PLAN_EXPERT, PLAN_FIRST, PLAN_SLOT, PLAN_NEXT, PLAN_ROWS = range(5)


def _expert_weight_copies(w_hbm_list, buf_list, sem, layer, expert, slot):
    return [pltpu.make_async_copy(w.at[layer, expert], buf.at[slot], sem.at[slot])
            for w, buf in zip(w_hbm_list, buf_list)]


def _expert_weight_pipeline(plan_ref, n_used, w_hbm_list, buf_list, sem, layer):
    j = pl.program_id(0)
    slot = plan_ref[PLAN_SLOT, j]
    first = jnp.logical_and(j < n_used, plan_ref[PLAN_FIRST, j] == 1)

    @pl.when(j == 0)
    def _():
        for cp in _expert_weight_copies(w_hbm_list, buf_list, sem, layer,
                                        plan_ref[PLAN_EXPERT, 0], 0):
            cp.start()

    @pl.when(jnp.logical_and(first, plan_ref[PLAN_NEXT, j] >= 0))
    def _():
        for cp in _expert_weight_copies(w_hbm_list, buf_list, sem, layer,
                                        plan_ref[PLAN_NEXT, j], 1 - slot):
            cp.start()

    @pl.when(first)
    def _():
        for cp in _expert_weight_copies(w_hbm_list, buf_list, sem, layer,
                                        plan_ref[PLAN_EXPERT, j], slot):
            cp.wait()

    return slot


def _moe_up_kernel(plan_ref, rt_ref, nu_ref, x_hbm, w1_hbm, w3_hbm, h_ref,
                   xg_even, xg_odd, w1_buf, w3_buf, xsem, wsem, *, tm, layer):
    j = pl.program_id(0)
    n_used = nu_ref[0]
    xg = (xg_even, xg_odd)

    def gather_start(tile, dst, sem):
        def body(g, carry):
            for u in range(SUBLANES):
                tok = rt_ref[tile * tm + g * SUBLANES + u]
                pltpu.make_async_copy(x_hbm.at[tok], dst.at[g, :, pl.ds(u, 1), :], sem).start()
            return carry
        lax.fori_loop(0, plan_ref[PLAN_ROWS, tile] // SUBLANES, body, 0)

    def gather_wait(tile, dst, sem):
        def body(g, carry):
            for u in range(SUBLANES):
                pltpu.make_async_copy(x_hbm.at[0], dst.at[0, :, pl.ds(u, 1), :], sem).wait()
            return carry
        lax.fori_loop(0, plan_ref[PLAN_ROWS, tile] // SUBLANES, body, 0)

    @pl.when(j == 0)
    def _():
        xg_even[...] = jnp.zeros_like(xg_even)
        xg_odd[...] = jnp.zeros_like(xg_odd)
        gather_start(0, xg_even, xsem.at[0])

    wslot = _expert_weight_pipeline(plan_ref, n_used, (w1_hbm, w3_hbm), (w1_buf, w3_buf),
                                    wsem, layer)

    def swiglu(parity):
        x = jnp.concatenate([xg[parity][:, lt].reshape(tm, LANES)
                             for lt in range(xg[parity].shape[1])], axis=1).astype(BF16)
        a = jnp.dot(x, w1_buf[wslot].astype(BF16), preferred_element_type=F32)
        b = jnp.dot(x, w3_buf[wslot].astype(BF16), preferred_element_type=F32)
        h_ref[...] = (a * jax.nn.sigmoid(a) * b).astype(h_ref.dtype)

    for parity in (0, 1):
        mine = jnp.logical_and(j < n_used, j % 2 == parity)

        @pl.when(jnp.logical_and(mine, j + 1 < n_used))
        def _(parity=parity):
            gather_start(j + 1, xg[1 - parity], xsem.at[1 - parity])

        @pl.when(mine)
        def _(parity=parity):
            gather_wait(j, xg[parity], xsem.at[parity])
            swiglu(parity)

    @pl.when(j >= n_used)
    def _():
        h_ref[...] = jnp.zeros_like(h_ref)


def moe_up(x_rows, w1, w3, plan, row_token, n_used, *, layer, tm):
    x = x_rows
    D = x_rows.shape[1] * LANES
    FF = w1.shape[-1]
    n_tiles = plan.shape[1]
    any_spec = lambda: pl.BlockSpec(memory_space=pl.ANY)
    return pl.pallas_call(
        functools.partial(_moe_up_kernel, tm=tm, layer=layer),
        grid_spec=pltpu.PrefetchScalarGridSpec(
            num_scalar_prefetch=3,
            grid=(n_tiles,),
            in_specs=[any_spec(), any_spec(), any_spec()],
            out_specs=pl.BlockSpec((tm, FF), lambda j, plan, rt, nu: (j, 0)),
            scratch_shapes=[pltpu.VMEM((tm // SUBLANES, D // LANES, SUBLANES, LANES), F32),
                            pltpu.VMEM((tm // SUBLANES, D // LANES, SUBLANES, LANES), F32),
                            pltpu.VMEM((2, D, FF), F32), pltpu.VMEM((2, D, FF), F32),
                            pltpu.SemaphoreType.DMA((2,)), pltpu.SemaphoreType.DMA((2,))]),
        out_shape=jax.ShapeDtypeStruct((n_tiles * tm, FF), BF16),
        compiler_params=_params(("arbitrary",)),
        name="moe_up",
    )(plan, row_token, n_used, x, w1, w3)


def _moe_down_kernel(plan_ref, nu_ref, h_ref, w2_hbm, y_ref, w2_buf, wsem, *, layer):
    j = pl.program_id(0)
    n_used = nu_ref[0]
    wslot = _expert_weight_pipeline(plan_ref, n_used, (w2_hbm,), (w2_buf,), wsem, layer)

    @pl.when(j < n_used)
    def _():
        y_ref[...] = jnp.dot(h_ref[...], w2_buf[wslot].astype(BF16),
                             preferred_element_type=F32)

    @pl.when(j >= n_used)
    def _():
        y_ref[...] = jnp.zeros_like(y_ref)


def moe_down(h, w2, plan, n_used, *, layer, tm):
    FF, D = w2.shape[-2:]
    n_tiles = plan.shape[1]
    tile = lambda j, plan, nu: (j, 0)
    return pl.pallas_call(
        functools.partial(_moe_down_kernel, layer=layer),
        grid_spec=pltpu.PrefetchScalarGridSpec(
            num_scalar_prefetch=2,
            grid=(n_tiles,),
            in_specs=[pl.BlockSpec((tm, FF), tile), pl.BlockSpec(memory_space=pl.ANY)],
            out_specs=pl.BlockSpec((tm, D), tile),
            scratch_shapes=[pltpu.VMEM((2, FF, D), F32), pltpu.SemaphoreType.DMA((2,))]),
        out_shape=jax.ShapeDtypeStruct((n_tiles * tm, D), F32),
        compiler_params=_params(("arbitrary",)),
        name="moe_down",
    )(plan, n_used, h, w2)


def _combine_ln_kernel(pos_ref, y_hbm, x_ref, wt_ref, g_ref, b_ref, o_ref, *rest, tm):
    ob_ref = rest[0] if len(rest) == 3 else None
    yg_ref, sem = rest[-2:]
    i = pl.program_id(0)

    def gather(tile, slot):
        for k in range(TOP_K):
            _row_gather_start(y_hbm, yg_ref.at[slot, k], sem.at[slot],
                              lambda r: pos_ref[TOP_K * (tile * tm + r) + k], tm)

    @pl.when(i == 0)
    def _():
        gather(0, 0)

    @pl.when(i + 1 < pl.num_programs(0))
    def _():
        gather(i + 1, (i + 1) % 2)

    slot = i % 2
    for k in range(TOP_K):
        _row_gather_wait(y_hbm, yg_ref.at[slot, k], sem.at[slot], tm)
    wt = wt_ref[...]
    z = DN_ALPHA * x_ref[...]
    for k in range(TOP_K):
        z = z + wt[:, k:k + 1] * yg_ref[slot, k]
    y = _layer_norm_rows(z, g_ref[...], b_ref[...])
    o_ref[...] = y
    if ob_ref is not None:
        ob_ref[...] = y.astype(BF16)


def moe_combine_ln(y, pos, x, wts, g, b, *, tm, with_bf16_copy):
    T, D = x.shape
    row = lambda i, pos: (i, 0)
    fixed = lambda i, pos: (0, 0)
    n_out = 2 if with_bf16_copy else 1
    return pl.pallas_call(
        functools.partial(_combine_ln_kernel, tm=tm),
        grid_spec=pltpu.PrefetchScalarGridSpec(
            num_scalar_prefetch=1,
            grid=(T // tm,),
            in_specs=[pl.BlockSpec(memory_space=pl.ANY),
                      pl.BlockSpec((tm, D), row),
                      pl.BlockSpec((tm, LANES), row),
                      pl.BlockSpec((1, D), fixed),
                      pl.BlockSpec((1, D), fixed)],
            out_specs=[pl.BlockSpec((tm, D), row)] * n_out,
            scratch_shapes=[pltpu.VMEM((2, TOP_K, tm, D), F32),
                            pltpu.SemaphoreType.DMA((2,))]),
        out_shape=[jax.ShapeDtypeStruct((T, D), F32), jax.ShapeDtypeStruct((T, D), BF16)][:n_out],
        compiler_params=_params(("arbitrary",)),
        name="moe_combine_ln",
    )(pos, y, x, wts, g.reshape(1, D), b.reshape(1, D))


def _round_up(v, multiple):
    return (v + multiple - 1) // multiple * multiple


def _dispatch_plan(ids, *, n_experts, tm, n_tiles):
    n_assign = ids.shape[0]
    experts = jnp.arange(n_experts, dtype=jnp.int32)
    onehot = (ids[:, None] == experts[None, :]).astype(jnp.int32)
    csum = jnp.cumsum(onehot, axis=0)
    rank = jnp.sum((csum - onehot) * onehot, axis=1)
    counts = csum[-1]
    tiles_per = (counts + tm - 1) // tm
    tile_end = jnp.cumsum(tiles_per)
    tile_start = tile_end - tiles_per
    n_used = tile_end[-1]
    pos = tile_start[ids] * tm + rank
    row_token = jnp.zeros((n_tiles * tm,), jnp.int32).at[pos].set(
        jnp.arange(n_assign, dtype=jnp.int32) // TOP_K)

    tile_ids = jnp.arange(n_tiles, dtype=jnp.int32)
    in_use = tile_ids < n_used
    tile_expert = jnp.sum((jnp.minimum(tile_ids, n_used - 1)[:, None] >= tile_end[None, :])
                          .astype(jnp.int32), axis=1)
    used = counts > 0
    ordinal = jnp.cumsum(used.astype(jnp.int32)) - 1
    later = lax.cummin(jnp.where(used, experts, n_experts), axis=0, reverse=True)
    next_used = jnp.concatenate([later[1:], jnp.full((1,), n_experts, jnp.int32)])
    next_used = jnp.where(next_used < n_experts, next_used, -1)
    plan = jnp.stack([
        tile_expert,
        jnp.logical_and(in_use, tile_ids == tile_start[tile_expert]).astype(jnp.int32),
        ordinal[tile_expert] % 2,
        next_used[tile_expert],
        jnp.where(in_use, _round_up(jnp.clip(
            counts[tile_expert] - (tile_ids - tile_start[tile_expert]) * tm, 0, tm),
            GATHER_UNROLL), 0),
    ]).astype(jnp.int32)
    return pos.astype(jnp.int32), row_token, plan, n_used.reshape(1).astype(jnp.int32)


def hierarchical_moe_ln(x, x_rows, w_group, b_group, w_expert, b_expert, w1, w3, w2, g, b, *,
                        layer, with_bf16_copy, tm_route, tm_moe, tm_comb):
    T, D = x.shape
    E = w1.shape[1]
    n_route = N_GROUPS + E
    w_route = jnp.zeros((D, LANES), F32).at[:, :N_GROUPS].set(w_group).at[:, N_GROUPS:n_route].set(w_expert)
    b_route = jnp.zeros((1, LANES), F32).at[0, :N_GROUPS].set(b_group).at[0, N_GROUPS:n_route].set(b_expert)
    ids, wts = router(x, w_route, b_route, tm=tm_route)
    n_tiles = (T * TOP_K + E * (tm_moe - 1)) // tm_moe
    pos, row_token, plan, n_used = _dispatch_plan(
        ids[:, :TOP_K].reshape(-1), n_experts=E, tm=tm_moe, n_tiles=n_tiles)
    h = moe_up(x_rows, w1, w3, plan, row_token, n_used, layer=layer, tm=tm_moe)
    y = moe_down(h, w2, plan, n_used, layer=layer, tm=tm_moe)
    out = moe_combine_ln(y, pos, x, wts, g, b, tm=tm_comb, with_bf16_copy=with_bf16_copy)
    return out[0], (out[1] if with_bf16_copy else None)


def kernel(x, even_w_in, even_w_pool, even_pool_scale, even_b_f, even_w_out, odd_w_in, odd_conv_w, odd_conv_b, odd_w_a, odd_b_a, odd_w_x, odd_b_x, odd_lambda, odd_w_out, moe_w_group, moe_b_group, moe_w_expert, moe_b_expert, moe_w1, moe_w3, moe_w2, ln_g, ln_b):
    B, S, D = x.shape
    T = B * S
    xf = x.reshape(T, D)
    xb = None
    tiles = _tile_plan(T, S, D)

    for layer in range(DEPTH):
        i = layer // 2
        if layer % 2 == 0:
            pool_w = even_w_pool.shape[1] * even_w_pool.shape[2]
            heads = even_b_f.shape[1]
            fox_w = heads * HEAD_DIM
            w_in_t = jnp.swapaxes(even_w_in[i], 0, 1)
            lane0 = LANES - heads
            w_f = w_in_t[w_in_t.shape[0] - LANES:, :]
            b_f = jnp.zeros((1, LANES), F32).at[0, lane0:].set(even_b_f[i])
            c, xb = forget_cumsum(xf, w_f, b_f, batch=B, tm=tiles["seq"])
            proj = matmul(xb, w_in_t, n_cols=pool_w + 3 * fox_w, tm=tiles["mm_m"],
                          tn=tiles["mm_n"], out_dtype=BF16, w_is_transposed=True)
            a_out = pool_mixer(proj, even_w_pool[i].astype(BF16), even_pool_scale[i],
                               batch=B, ts=tiles["seq"])
            qc = pool_w // HEAD_DIM
            b_out = forgetting_attention(proj, c, batch=B, heads=heads, head_lane0=lane0,
                                         q_col=qc, k_col=qc + heads, v_col=qc + 2 * heads,
                                         tq=tiles["attn_q"], tk=tiles["attn_k"])
            mixed = [a_out, b_out]
            w_out = even_w_out[i].astype(BF16)
        else:
            if xb is None:
                xb = xf.astype(BF16)
            proj = matmul(xb, odd_w_in[i], n_cols=odd_w_in.shape[2], tm=tiles["mm_m"],
                          tn=tiles["mm_n"], out_dtype=BF16)
            mixed = [rglru_mixer(proj, odd_conv_w[i], odd_conv_b[i], odd_w_a[i].astype(BF16),
                                 odd_b_a[i].reshape(-1), odd_w_x[i].astype(BF16),
                                 odd_b_x[i].reshape(-1), odd_lambda[i], batch=B, ts=tiles["seq"])]
            w_out = odd_w_out[i].astype(BF16)
        xf, x_rows = matmul_residual_ln(mixed, w_out, xf, ln_g[layer, 0], ln_b[layer, 0],
                                        tm=tiles["ln_m"])
        xf, xb = hierarchical_moe_ln(xf, x_rows, moe_w_group[layer], moe_b_group[layer],
                                     moe_w_expert[layer], moe_b_expert[layer], moe_w1, moe_w3,
                                     moe_w2, ln_g[layer, 1], ln_b[layer, 1], layer=layer,
                                     with_bf16_copy=layer + 1 < DEPTH,
                                     tm_route=tiles["route"], tm_moe=tiles["moe"],
                                     tm_comb=tiles["comb"])
    return xf.reshape(B, S, D)


def _tile_plan(T, S, D):
    return {
        "mm_m": min(1024, T), "mm_n": min(1024, D),
        "seq": min(512, S), "attn_q": min(1024, S), "attn_k": min(512, S),
        "ln_m": min(256, T),
        "route": min(512, T), "moe": 256, "comb": min(256, T),
    }
```

```python
import functools

import jax
import jax.numpy as jnp
from jax import lax
from jax.experimental import pallas as pl
from jax.experimental.pallas import tpu as pltpu

F32 = jnp.float32
BF16 = jnp.bfloat16

POOL_WINDOWS = (2, 4, 8, 16)
POOL_HALO = 16
HEAD_DIM = 128
LRU_BLOCKS = 16
CONV_WIDTH = 4
CONV_HALO = 8
LRU_C = 8.0
N_GROUPS = 4
EXPERTS_PER_GROUP = 8
TOP_K = 2
DEPTH = 2
DN_ALPHA = (2.0 * DEPTH) ** 0.25
LN_EPS = 1e-5

LANES = 128
SUBLANES = 8
VMEM_LIMIT_MB = 56
OUT_PROJ_VMEM_MB = 60


def _params(semantics, vmem_mb=VMEM_LIMIT_MB):
    return pltpu.CompilerParams(dimension_semantics=semantics,
                                vmem_limit_bytes=vmem_mb * 1024 * 1024)


_CONTRACT_LAST = (((1,), (1,)), ((), ()))


def _mm_kernel(a_ref, w_hbm, o_ref, w_stage, w_bf, sem, *, w_is_transposed, tn):
    j = pl.program_id(0)
    i = pl.program_id(1)

    def w_copy(block):
        cols = pl.ds(pl.multiple_of(block * tn, tn), tn)
        src = w_hbm.at[cols, :] if w_is_transposed else w_hbm.at[:, cols]
        return pltpu.make_async_copy(src, w_stage, sem)

    @pl.when(jnp.logical_and(j == 0, i == 0))
    def _():
        w_copy(0).start()

    @pl.when(i == 0)
    def _():
        w_copy(j).wait()
        w_bf[...] = w_stage[...].astype(BF16)

        @pl.when(j + 1 < pl.num_programs(0))
        def _():
            w_copy(j + 1).start()

    if w_is_transposed:
        out = lax.dot_general(a_ref[...], w_bf[...], _CONTRACT_LAST, preferred_element_type=F32)
    else:
        out = jnp.dot(a_ref[...], w_bf[...], preferred_element_type=F32)
    o_ref[...] = out.astype(o_ref.dtype)


def matmul(a, w, *, n_cols, tm, tn, out_dtype, w_is_transposed=False):
    M, K = a.shape
    w_block = (tn, K) if w_is_transposed else (K, tn)
    return pl.pallas_call(
        functools.partial(_mm_kernel, w_is_transposed=w_is_transposed, tn=tn),
        grid=(n_cols // tn, M // tm),
        in_specs=[pl.BlockSpec((tm, K), lambda j, i: (i, 0)),
                  pl.BlockSpec(memory_space=pl.ANY)],
        out_specs=pl.BlockSpec((tm, tn), lambda j, i: (i, j)),
        out_shape=jax.ShapeDtypeStruct((M, n_cols), out_dtype),
        scratch_shapes=[pltpu.VMEM(w_block, F32), pltpu.VMEM(w_block, BF16),
                        pltpu.SemaphoreType.DMA(())],
        compiler_params=_params(("arbitrary", "arbitrary")),
        name="proj_matmul",
    )(a, w)


def _layer_norm_rows(z, g, b):
    mu = jnp.mean(z, axis=-1, keepdims=True)
    d = z - mu
    var = jnp.mean(d * d, axis=-1, keepdims=True)
    return d * lax.rsqrt(var + LN_EPS) * g + b


LN_CHUNK_ROWS = 64


def _mm_ln_kernel(*refs, n_a):
    a_refs = refs[:n_a]
    w_hbm, x_hbm, g_ref, b_ref, o_ref, w_buf, acc_ref, sem = refs[n_a:]
    i = pl.program_id(0)
    tm = o_ref.shape[0]

    @pl.when(i == 0)
    def _():
        w_copy = pltpu.make_async_copy(w_hbm, w_buf, sem.at[0])
        w_copy.start()
        w_copy.wait()

    x_copy = pltpu.make_async_copy(x_hbm.at[pl.ds(pl.multiple_of(i * tm, tm), tm), :], o_ref,
                                   sem.at[1])
    x_copy.start()
    k_each = a_refs[0].shape[1]
    for idx, a_ref in enumerate(a_refs):
        part = jnp.dot(a_ref[...], w_buf[idx * k_each:(idx + 1) * k_each, :],
                       preferred_element_type=F32)
        if idx == 0:
            acc_ref[...] = part
        else:
            acc_ref[...] += part
    x_copy.wait()

    def normalise(c, carry):
        rows = pl.ds(pl.multiple_of(c * LN_CHUNK_ROWS, LN_CHUNK_ROWS), LN_CHUNK_ROWS)
        o_ref[rows, :] = _layer_norm_rows(DN_ALPHA * o_ref[rows, :] + acc_ref[rows, :],
                                          g_ref[...], b_ref[...])
        return carry

    lax.fori_loop(0, tm // LN_CHUNK_ROWS, normalise, 0)


def matmul_residual_ln(a_list, w, x, g, b, *, tm):
    M, K_each = a_list[0].shape
    K, N = w.shape
    n_a = len(a_list)
    assert n_a * K_each == K and tm % LN_CHUNK_ROWS == 0
    return pl.pallas_call(
        functools.partial(_mm_ln_kernel, n_a=n_a),
        grid=(M // tm,),
        in_specs=[pl.BlockSpec((tm, K_each), lambda i: (i, 0)) for _ in range(n_a)] + [
            pl.BlockSpec(memory_space=pl.ANY),
            pl.BlockSpec(memory_space=pl.ANY),
            pl.BlockSpec((1, N), lambda i: (0, 0)),
            pl.BlockSpec((1, N), lambda i: (0, 0))],
        out_specs=pl.BlockSpec((tm, N), lambda i: (i, 0)),
        out_shape=jax.ShapeDtypeStruct((M, N), F32),
        scratch_shapes=[pltpu.VMEM((K, N), BF16), pltpu.VMEM((tm, N), F32),
                        pltpu.SemaphoreType.DMA((2,))],
        compiler_params=_params(("arbitrary",), vmem_mb=OUT_PROJ_VMEM_MB),
        name="out_proj_ln",
    )(*a_list, w, x, g.reshape(1, N), b.reshape(1, N))


def _cumsum_rows(v, n_rows):
    row = lax.broadcasted_iota(jnp.int32, v.shape, 0)
    shift = 1
    while shift < n_rows:
        v = v + jnp.where(row >= shift, pltpu.roll(v, shift, 0), 0.0)
        shift *= 2
    return v


def _forget_kernel(x_ref, w_ref, bf_ref, c_ref, xb_ref, carry_ref, *, tm):
    @pl.when(pl.program_id(1) == 0)
    def _():
        carry_ref[...] = jnp.zeros_like(carry_ref)

    xb = x_ref[...].astype(BF16)
    xb_ref[...] = xb
    z = lax.dot_general(xb, w_ref[...].astype(BF16), _CONTRACT_LAST,
                        preferred_element_type=F32) + bf_ref[...]
    log_f = jnp.minimum(z, 0.0) - jnp.log1p(jnp.exp(-jnp.abs(z)))
    c = _cumsum_rows(log_f, tm) + carry_ref[...]
    c_ref[...] = c
    carry_ref[...] = c[tm - 1:tm, :]


def forget_cumsum(x, w_f, b_f, *, batch, tm):
    T, D = x.shape
    steps = T // batch // tm
    return pl.pallas_call(
        functools.partial(_forget_kernel, tm=tm),
        grid=(batch, steps),
        in_specs=[pl.BlockSpec((tm, D), lambda b, s: (b * steps + s, 0)),
                  pl.BlockSpec((LANES, D), lambda b, s: (0, 0)),
                  pl.BlockSpec((1, LANES), lambda b, s: (0, 0))],
        out_specs=[pl.BlockSpec((tm, LANES), lambda b, s: (b * steps + s, 0)),
                   pl.BlockSpec((tm, D), lambda b, s: (b * steps + s, 0))],
        out_shape=[jax.ShapeDtypeStruct((T, LANES), F32), jax.ShapeDtypeStruct((T, D), BF16)],
        scratch_shapes=[pltpu.VMEM((1, LANES), F32)],
        compiler_params=_params(("parallel", "arbitrary")),
        name="forget_cumsum",
    )(x, w_f, b_f)


def _pool_kernel(u_ref, w_ref, sc_ref, o_ref, ext_ref, *, ts):
    g = pl.program_id(1)
    si = pl.program_id(2)

    @pl.when(si == 0)
    def _():
        ext_ref[0:POOL_HALO, :] = jnp.zeros((POOL_HALO, ext_ref.shape[1]), F32)

    @pl.when(si > 0)
    def _():
        ext_ref[0:POOL_HALO, :] = ext_ref[ts:ts + POOL_HALO, :]

    u = u_ref[...].astype(F32)
    ext_ref[POOL_HALO:, :] = u
    e1 = ext_ref[...]
    e2 = e1 + pltpu.roll(e1, 1, 0)
    e4 = e2 + pltpu.roll(e2, 2, 0)
    e8 = e4 + pltpu.roll(e4, 4, 0)
    e16 = e8 + pltpu.roll(e8, 8, 0)
    win = jnp.where(g == 0, e2, jnp.where(g == 1, e4, jnp.where(g == 2, e8, e16)))
    win = win[POOL_HALO:, :]
    width = jnp.left_shift(2, g)
    t = si * ts + lax.broadcasted_iota(jnp.int32, (ts, 1), 0)
    cnt = jnp.minimum(t + 1, width).astype(F32)
    pooled = win / cnt - u
    y = jnp.dot(pooled.astype(BF16), w_ref[...], preferred_element_type=F32)
    o_ref[...] = (y * sc_ref[...]).astype(o_ref.dtype)


def pool_mixer(proj, w_pool_b, pool_scale, *, batch, ts):
    T = proj.shape[0]
    G, C, _ = w_pool_b.shape
    assert POOL_WINDOWS == (2, 4, 8, 16) and G == len(POOL_WINDOWS)
    steps = T // batch // ts
    return pl.pallas_call(
        functools.partial(_pool_kernel, ts=ts),
        grid=(batch, G, steps),
        in_specs=[pl.BlockSpec((ts, C), lambda b, g, s: (b * steps + s, g)),
                  pl.BlockSpec((None, C, C), lambda b, g, s: (g, 0, 0)),
                  pl.BlockSpec((1, C), lambda b, g, s: (0, g))],
        out_specs=pl.BlockSpec((ts, C), lambda b, g, s: (b * steps + s, g)),
        out_shape=jax.ShapeDtypeStruct((T, G * C), BF16),
        scratch_shapes=[pltpu.VMEM((ts + POOL_HALO, C), F32)],
        compiler_params=_params(("parallel", "parallel", "arbitrary")),
        name="pool_mixer",
    )(proj, w_pool_b, pool_scale.reshape(1, G * C))


LOG2E = 1.4426950408889634
N_BIAS_PARTS = 3


def _attn_kernel(q_ref, k_ref, v_ref, c_ref, o_ref, kaug_ref, vaug_ref, qext_ref,
                 qaug_ref, sa_ref, sb_ref, m_ref, acc_ref, *, tq, tk, scale, head_lane0):
    h = pl.program_id(1)
    qi = pl.program_id(2)
    S = k_ref.shape[0]

    @pl.when(qi == 0)
    def _():
        lane = lax.broadcasted_iota(jnp.int32, (S, LANES), 1)
        c = jnp.sum(jnp.where(lane == head_lane0 + h, c_ref[...], 0.0), axis=1,
                    keepdims=True) * LOG2E
        hi = c.astype(BF16).astype(F32)
        mid = (c - hi).astype(BF16).astype(F32)
        lo = c - hi - mid
        ones = jnp.where(lane < N_BIAS_PARTS, 1.0, 0.0)
        k_ext = jnp.where(lane == 3, -hi, jnp.where(lane == 4, -mid, jnp.where(lane == 5, -lo, ones)))
        q_ext = jnp.where(lane == 0, hi, jnp.where(lane == 1, mid, jnp.where(
            lane == 2, lo, jnp.where(lane < 2 * N_BIAS_PARTS, 1.0, 0.0))))
        kaug_ref[:, :HEAD_DIM] = k_ref[...]
        kaug_ref[:, HEAD_DIM:] = k_ext.astype(BF16)
        vaug_ref[:, :HEAD_DIM] = v_ref[...]
        vaug_ref[:, HEAD_DIM:] = jnp.where(lane == 0, 1.0, 0.0).astype(BF16)
        qext_ref[...] = q_ext.astype(BF16)

    q0 = pl.multiple_of(qi * tq, tq)
    qaug_ref[:, :HEAD_DIM] = (q_ref[...].astype(F32) * (scale * LOG2E)).astype(BF16)
    qaug_ref[:, HEAD_DIM:] = qext_ref[pl.ds(q0, tq), :]
    m_ref[...] = jnp.full(m_ref.shape, -jnp.inf, F32)
    acc_ref[...] = jnp.zeros(acc_ref.shape, F32)

    def scores(block, lo=0):
        k0 = pl.multiple_of(block * tk, tk)
        return lax.dot_general(qaug_ref[lo:, :], kaug_ref[pl.ds(k0, tk), :],
                               (((1,), (1,)), ((), ())), preferred_element_type=F32)

    def softmax_pv(s, block, lo=0):
        k0 = pl.multiple_of(block * tk, tk)
        m = m_ref[lo:, :]
        m_new = jnp.maximum(m, jnp.max(s, axis=1, keepdims=True))
        alpha = jnp.exp2(m - m_new)
        p = jnp.exp2((s - m_new).astype(BF16))
        acc_ref[lo:, :] = alpha * acc_ref[lo:, :] + jnp.dot(
            p, vaug_ref[pl.ds(k0, tk), :], preferred_element_type=F32)
        m_ref[lo:, :] = m_new

    n_full = qi * (tq // tk)
    sa_ref[...] = scores(0)

    def body(i, carry):
        j = 2 * i
        sb_ref[...] = scores(j + 1)
        softmax_pv(sa_ref[...], j)
        sa_ref[...] = scores(j + 2)
        softmax_pv(sb_ref[...], j + 1)
        return carry

    lax.fori_loop(0, n_full // 2, body, 0)

    diag_scores = [sa_ref[...]] + [scores(n_full + u, u * tk) for u in range(1, tq // tk)]
    for u, s in enumerate(diag_scores):
        lo = u * tk
        row = lax.broadcasted_iota(jnp.int32, s.shape, 0)
        col = lax.broadcasted_iota(jnp.int32, s.shape, 1)
        softmax_pv(jnp.where(col <= row, s, -jnp.inf), n_full + u, lo)
        acc = acc_ref[lo:lo + tk, :]
        o_ref[lo:lo + tk, :] = (acc[:, :HEAD_DIM] / acc[:, HEAD_DIM:HEAD_DIM + 1]
                                ).astype(o_ref.dtype)


def forgetting_attention(proj, c, *, batch, heads, head_lane0, q_col, k_col, v_col, tq, tk):
    T = proj.shape[0]
    S = T // batch
    nq = S // tq
    assert (tq // tk) % 2 == 0, "full key blocks are consumed two per loop trip"
    return pl.pallas_call(
        functools.partial(_attn_kernel, tq=tq, tk=tk, scale=HEAD_DIM ** -0.5,
                          head_lane0=head_lane0),
        grid=(batch, heads, nq),
        in_specs=[pl.BlockSpec((tq, HEAD_DIM), lambda b, h, i: (b * nq + i, q_col + h)),
                  pl.BlockSpec((S, HEAD_DIM), lambda b, h, i: (b, k_col + h)),
                  pl.BlockSpec((S, HEAD_DIM), lambda b, h, i: (b, v_col + h)),
                  pl.BlockSpec((S, LANES), lambda b, h, i: (b, 0))],
        out_specs=pl.BlockSpec((tq, HEAD_DIM), lambda b, h, i: (b * nq + i, h)),
        out_shape=jax.ShapeDtypeStruct((T, heads * HEAD_DIM), BF16),
        scratch_shapes=[pltpu.VMEM((S, 2 * HEAD_DIM), BF16), pltpu.VMEM((S, 2 * HEAD_DIM), BF16),
                        pltpu.VMEM((S, LANES), BF16), pltpu.VMEM((tq, 2 * HEAD_DIM), BF16),
                        pltpu.VMEM((tq, tk), F32), pltpu.VMEM((tq, tk), F32),
                        pltpu.VMEM((tq, 1), F32), pltpu.VMEM((tq, 2 * HEAD_DIM), F32)],
        compiler_params=_params(("parallel", "parallel", "arbitrary")),
        name="forgetting_attention",
    )(proj, proj, proj, c)


def _sigmoid(z):
    return 0.5 * jnp.tanh(0.5 * z) + 0.5


def _rglru_kernel(xb_ref, gate_ref, cw_ref, cb_ref, wa_ref, ba_ref, wx_ref, bx_ref, lam_ref,
                  y_ref, ext_ref, a_ref, b_ref, h_ref, carry_ref, *, ts):
    si = pl.program_id(2)
    C = xb_ref.shape[1]

    @pl.when(si == 0)
    def _():
        ext_ref[0:CONV_HALO, :] = jnp.zeros((CONV_HALO, C), F32)
        carry_ref[...] = jnp.zeros_like(carry_ref)

    @pl.when(si > 0)
    def _():
        ext_ref[0:CONV_HALO, :] = ext_ref[ts:ts + CONV_HALO, :]

    xb = xb_ref[...].astype(F32)
    ext_ref[CONV_HALO:, :] = xb
    cw = cw_ref[...]
    xc = cb_ref[...] + xb * cw[CONV_WIDTH - 1:CONV_WIDTH, :]
    for lag in range(1, CONV_WIDTH):
        tap = CONV_WIDTH - 1 - lag
        xc = xc + ext_ref[CONV_HALO - lag:CONV_HALO - lag + ts, :] * cw[tap:tap + 1, :]

    xcb = xc.astype(BF16)
    r = _sigmoid(jnp.dot(xcb, wa_ref[...], preferred_element_type=F32) + ba_ref[...])
    ig = _sigmoid(jnp.dot(xcb, wx_ref[...], preferred_element_type=F32) + bx_ref[...])
    lam = lam_ref[...]
    log_a_base = -(jnp.maximum(-lam, 0.0) + jnp.log1p(jnp.exp(-jnp.abs(lam))))
    log_a = LRU_C * r * log_a_base
    a = jnp.exp(log_a)
    mult = jnp.exp2(0.5 * jnp.log2(-jnp.tanh(log_a) * (1.0 + a * a)))
    b = mult * ig * xc

    groups = ts // SUBLANES
    a3 = a.reshape(groups, SUBLANES, C)
    b3 = b.reshape(groups, SUBLANES, C)
    sub = lax.broadcasted_iota(jnp.int32, a3.shape, 1)
    shift = 1
    while shift < SUBLANES:
        keep = sub >= shift
        b3 = jnp.where(keep, a3 * pltpu.roll(b3, shift, 1) + b3, b3)
        a3 = jnp.where(keep, a3 * pltpu.roll(a3, shift, 1), a3)
        shift *= 2
    a_ref[...] = a3.reshape(ts, C)
    b_ref[...] = b3.reshape(ts, C)

    def body(gi, carry):
        r0 = pl.multiple_of(gi * SUBLANES, SUBLANES)
        hg = a_ref[pl.ds(r0, SUBLANES), :] * carry + b_ref[pl.ds(r0, SUBLANES), :]
        h_ref[pl.ds(r0, SUBLANES), :] = hg
        return hg[SUBLANES - 1:SUBLANES, :]

    carry_ref[...] = lax.fori_loop(0, groups, body, carry_ref[...], unroll=8)

    gate = gate_ref[...].astype(F32)
    gelu = 0.5 * gate * (1.0 + jnp.tanh(0.7978845608028654 * (gate + 0.044715 * gate * gate * gate)))
    y_ref[...] = (h_ref[...] * gelu).astype(y_ref.dtype)


def rglru_mixer(proj, conv_w, conv_b, w_a_b, b_a, w_x_b, b_x, lam, *, batch, ts):
    T = proj.shape[0]
    NB, C, _ = w_a_b.shape
    W = NB * C
    steps = T // batch // ts
    vec = lambda: pl.BlockSpec((1, C), lambda b, c, s: (0, c))
    blk = lambda: pl.BlockSpec((None, C, C), lambda b, c, s: (c, 0, 0))
    return pl.pallas_call(
        functools.partial(_rglru_kernel, ts=ts),
        grid=(batch, NB, steps),
        in_specs=[pl.BlockSpec((ts, C), lambda b, c, s: (b * steps + s, NB + c)),
                  pl.BlockSpec((ts, C), lambda b, c, s: (b * steps + s, c)),
                  pl.BlockSpec((CONV_WIDTH, C), lambda b, c, s: (0, c)),
                  vec(), blk(), vec(), blk(), vec(), vec()],
        out_specs=pl.BlockSpec((ts, C), lambda b, c, s: (b * steps + s, c)),
        out_shape=jax.ShapeDtypeStruct((T, W), BF16),
        scratch_shapes=[pltpu.VMEM((ts + CONV_HALO, C), F32),
                        pltpu.VMEM((ts, C), F32), pltpu.VMEM((ts, C), F32),
                        pltpu.VMEM((ts, C), F32), pltpu.VMEM((1, C), F32)],
        compiler_params=_params(("parallel", "parallel", "arbitrary")),
        name="rglru_mixer",
    )(proj, proj, conv_w, conv_b.reshape(1, W), w_a_b, b_a.reshape(1, W), w_x_b,
      b_x.reshape(1, W), lam.reshape(1, W))


def _router_kernel(x_ref, w_ref, b_ref, id_ref, wt_ref):
    logits = jnp.dot(x_ref[...].astype(BF16), w_ref[...].astype(BF16),
                     preferred_element_type=F32) + b_ref[...]
    lane = lax.broadcasted_iota(jnp.int32, logits.shape, 1).astype(F32)
    neg = -jnp.inf
    big = float(LANES)

    is_group = lane < N_GROUPS
    gl = jnp.where(is_group, logits, neg)
    gmax = jnp.max(gl, axis=1, keepdims=True)
    g_idx = jnp.min(jnp.where(gl == gmax, lane, big), axis=1, keepdims=True)
    g_w = 1.0 / jnp.sum(jnp.where(is_group, jnp.exp(gl - gmax), 0.0), axis=1, keepdims=True)

    lo = N_GROUPS + g_idx * EXPERTS_PER_GROUP
    el = jnp.where(lane >= lo, jnp.where(lane < lo + EXPERTS_PER_GROUP, logits, neg), neg)
    v1 = jnp.max(el, axis=1, keepdims=True)
    i1 = jnp.min(jnp.where(el == v1, lane, big), axis=1, keepdims=True)
    el2 = jnp.where(lane == i1, neg, el)
    v2 = jnp.max(el2, axis=1, keepdims=True)
    i2 = jnp.min(jnp.where(el2 == v2, lane, big), axis=1, keepdims=True)
    t = jnp.exp(v2 - v1)
    w1 = g_w / (1.0 + t)
    w2 = w1 * t
    ids = jnp.where(lane == 0, i1 - N_GROUPS, jnp.where(lane == 1, i2 - N_GROUPS, 0.0))
    id_ref[...] = ids.astype(jnp.int32)
    wt_ref[...] = jnp.where(lane == 0, w1, jnp.where(lane == 1, w2, 0.0))


def router(x, w_route, b_route, *, tm):
    T, D = x.shape
    return pl.pallas_call(
        _router_kernel,
        grid=(T // tm,),
        in_specs=[pl.BlockSpec((tm, D), lambda i: (i, 0)),
                  pl.BlockSpec((D, LANES), lambda i: (0, 0)),
                  pl.BlockSpec((1, LANES), lambda i: (0, 0))],
        out_specs=[pl.BlockSpec((tm, LANES), lambda i: (i, 0)),
                   pl.BlockSpec((tm, LANES), lambda i: (i, 0))],
        out_shape=[jax.ShapeDtypeStruct((T, LANES), jnp.int32),
                   jax.ShapeDtypeStruct((T, LANES), F32)],
        compiler_params=_params(("parallel",)),
        name="router",
    )(x, w_route, b_route)


GATHER_UNROLL = SUBLANES


def _row_gather_start(src_hbm, dst_ref, sem, row_of, n_rows):
    def body(g, carry):
        for u in range(GATHER_UNROLL):
            r = g * GATHER_UNROLL + u
            pltpu.make_async_copy(src_hbm.at[pl.ds(row_of(r), 1), :],
                                  dst_ref.at[pl.ds(r, 1), :], sem).start(priority=u % 2)
        return carry
    lax.fori_loop(0, n_rows // GATHER_UNROLL, body, 0)


def _row_gather_wait(src_hbm, dst_ref, sem, n_rows):
    if not isinstance(n_rows, int):
        n_rows = pl.multiple_of(n_rows, GATHER_UNROLL)
    pltpu.make_async_copy(src_hbm.at[pl.ds(0, n_rows), :], dst_ref.at[pl.ds(0, n_rows), :],
                          sem).wait()


PLAN_EXPERT, PLAN_FIRST, PLAN_SLOT, PLAN_NEXT, PLAN_ROWS = range(5)


def _expert_weight_copies(w_hbm_list, buf_list, sem, layer, expert, slot):
    return [pltpu.make_async_copy(w.at[layer, expert], buf.at[slot], sem.at[slot])
            for w, buf in zip(w_hbm_list, buf_list)]


def _expert_weight_pipeline(plan_ref, n_used, w_hbm_list, buf_list, sem, layer):
    j = pl.program_id(0)
    slot = plan_ref[PLAN_SLOT, j]
    first = jnp.logical_and(j < n_used, plan_ref[PLAN_FIRST, j] == 1)

    @pl.when(j == 0)
    def _():
        for cp in _expert_weight_copies(w_hbm_list, buf_list, sem, layer,
                                        plan_ref[PLAN_EXPERT, 0], 0):
            cp.start()

    @pl.when(jnp.logical_and(first, plan_ref[PLAN_NEXT, j] >= 0))
    def _():
        for cp in _expert_weight_copies(w_hbm_list, buf_list, sem, layer,
                                        plan_ref[PLAN_NEXT, j], 1 - slot):
            cp.start()

    @pl.when(first)
    def _():
        for cp in _expert_weight_copies(w_hbm_list, buf_list, sem, layer,
                                        plan_ref[PLAN_EXPERT, j], slot):
            cp.wait()

    return slot


def _moe_up_kernel(plan_ref, rt_ref, nu_ref, x_hbm, w1_hbm, w3_hbm, h_ref,
                   xg_even, xg_odd, w1_buf, w3_buf, xsem, wsem, *, tm, layer):
    j = pl.program_id(0)
    n_used = nu_ref[0]
    xg = (xg_even, xg_odd)

    @pl.when(j == 0)
    def _():
        xg_even[...] = jnp.zeros_like(xg_even)
        xg_odd[...] = jnp.zeros_like(xg_odd)
        _row_gather_start(x_hbm, xg_even, xsem.at[0], lambda r: rt_ref[r],
                          plan_ref[PLAN_ROWS, 0])

    wslot = _expert_weight_pipeline(plan_ref, n_used, (w1_hbm, w3_hbm), (w1_buf, w3_buf),
                                    wsem, layer)

    def swiglu(parity):
        x = xg[parity][...].astype(BF16)
        a = jnp.dot(x, w1_buf[wslot].astype(BF16), preferred_element_type=F32)
        b = jnp.dot(x, w3_buf[wslot].astype(BF16), preferred_element_type=F32)
        h_ref[...] = (a * jax.nn.sigmoid(a) * b).astype(h_ref.dtype)

    for parity in (0, 1):
        mine = jnp.logical_and(j < n_used, j % 2 == parity)

        @pl.when(jnp.logical_and(mine, j + 1 < n_used))
        def _(parity=parity):
            _row_gather_start(x_hbm, xg[1 - parity], xsem.at[1 - parity],
                              lambda r: rt_ref[(j + 1) * tm + r], plan_ref[PLAN_ROWS, j + 1])

        @pl.when(mine)
        def _(parity=parity):
            _row_gather_wait(x_hbm, xg[parity], xsem.at[parity], plan_ref[PLAN_ROWS, j])
            swiglu(parity)

    @pl.when(j >= n_used)
    def _():
        h_ref[...] = jnp.zeros_like(h_ref)


def moe_up(x, w1, w3, plan, row_token, n_used, *, layer, tm):
    D = x.shape[1]
    FF = w1.shape[-1]
    n_tiles = plan.shape[1]
    any_spec = lambda: pl.BlockSpec(memory_space=pl.ANY)
    return pl.pallas_call(
        functools.partial(_moe_up_kernel, tm=tm, layer=layer),
        grid_spec=pltpu.PrefetchScalarGridSpec(
            num_scalar_prefetch=3,
            grid=(n_tiles,),
            in_specs=[any_spec(), any_spec(), any_spec()],
            out_specs=pl.BlockSpec((tm, FF), lambda j, plan, rt, nu: (j, 0)),
            scratch_shapes=[pltpu.VMEM((tm, D), F32), pltpu.VMEM((tm, D), F32),
                            pltpu.VMEM((2, D, FF), F32), pltpu.VMEM((2, D, FF), F32),
                            pltpu.SemaphoreType.DMA((2,)), pltpu.SemaphoreType.DMA((2,))]),
        out_shape=jax.ShapeDtypeStruct((n_tiles * tm, FF), BF16),
        compiler_params=_params(("arbitrary",)),
        name="moe_up",
    )(plan, row_token, n_used, x, w1, w3)


def _moe_down_kernel(plan_ref, nu_ref, h_ref, w2_hbm, y_ref, w2_buf, wsem, *, layer):
    j = pl.program_id(0)
    n_used = nu_ref[0]
    wslot = _expert_weight_pipeline(plan_ref, n_used, (w2_hbm,), (w2_buf,), wsem, layer)

    @pl.when(j < n_used)
    def _():
        y_ref[...] = jnp.dot(h_ref[...], w2_buf[wslot].astype(BF16),
                             preferred_element_type=F32)

    @pl.when(j >= n_used)
    def _():
        y_ref[...] = jnp.zeros_like(y_ref)


def moe_down(h, w2, plan, n_used, *, layer, tm):
    FF, D = w2.shape[-2:]
    n_tiles = plan.shape[1]
    tile = lambda j, plan, nu: (j, 0)
    return pl.pallas_call(
        functools.partial(_moe_down_kernel, layer=layer),
        grid_spec=pltpu.PrefetchScalarGridSpec(
            num_scalar_prefetch=2,
            grid=(n_tiles,),
            in_specs=[pl.BlockSpec((tm, FF), tile), pl.BlockSpec(memory_space=pl.ANY)],
            out_specs=pl.BlockSpec((tm, D), tile),
            scratch_shapes=[pltpu.VMEM((2, FF, D), F32), pltpu.SemaphoreType.DMA((2,))]),
        out_shape=jax.ShapeDtypeStruct((n_tiles * tm, D), F32),
        compiler_params=_params(("arbitrary",)),
        name="moe_down",
    )(plan, n_used, h, w2)


def _combine_ln_kernel(pos_ref, y_hbm, x_ref, wt_ref, g_ref, b_ref, o_ref, *rest, tm):
    ob_ref = rest[0] if len(rest) == 3 else None
    yg_ref, sem = rest[-2:]
    i = pl.program_id(0)

    def gather(tile, slot):
        for k in range(TOP_K):
            _row_gather_start(y_hbm, yg_ref.at[slot, k], sem.at[slot],
                              lambda r: pos_ref[TOP_K * (tile * tm + r) + k], tm)

    @pl.when(i == 0)
    def _():
        gather(0, 0)

    @pl.when(i + 1 < pl.num_programs(0))
    def _():
        gather(i + 1, (i + 1) % 2)

    slot = i % 2
    for k in range(TOP_K):
        _row_gather_wait(y_hbm, yg_ref.at[slot, k], sem.at[slot], tm)
    wt = wt_ref[...]
    z = DN_ALPHA * x_ref[...]
    for k in range(TOP_K):
        z = z + wt[:, k:k + 1] * yg_ref[slot, k]
    y = _layer_norm_rows(z, g_ref[...], b_ref[...])
    o_ref[...] = y
    if ob_ref is not None:
        ob_ref[...] = y.astype(BF16)


def moe_combine_ln(y, pos, x, wts, g, b, *, tm, with_bf16_copy):
    T, D = x.shape
    row = lambda i, pos: (i, 0)
    fixed = lambda i, pos: (0, 0)
    n_out = 2 if with_bf16_copy else 1
    return pl.pallas_call(
        functools.partial(_combine_ln_kernel, tm=tm),
        grid_spec=pltpu.PrefetchScalarGridSpec(
            num_scalar_prefetch=1,
            grid=(T // tm,),
            in_specs=[pl.BlockSpec(memory_space=pl.ANY),
                      pl.BlockSpec((tm, D), row),
                      pl.BlockSpec((tm, LANES), row),
                      pl.BlockSpec((1, D), fixed),
                      pl.BlockSpec((1, D), fixed)],
            out_specs=[pl.BlockSpec((tm, D), row)] * n_out,
            scratch_shapes=[pltpu.VMEM((2, TOP_K, tm, D), F32),
                            pltpu.SemaphoreType.DMA((2,))]),
        out_shape=[jax.ShapeDtypeStruct((T, D), F32), jax.ShapeDtypeStruct((T, D), BF16)][:n_out],
        compiler_params=_params(("arbitrary",)),
        name="moe_combine_ln",
    )(pos, y, x, wts, g.reshape(1, D), b.reshape(1, D))


def _round_up(v, multiple):
    return (v + multiple - 1) // multiple * multiple


def _dispatch_plan(ids, *, n_experts, tm, n_tiles):
    n_assign = ids.shape[0]
    experts = jnp.arange(n_experts, dtype=jnp.int32)
    onehot = (ids[:, None] == experts[None, :]).astype(jnp.int32)
    csum = jnp.cumsum(onehot, axis=0)
    rank = jnp.sum((csum - onehot) * onehot, axis=1)
    counts = csum[-1]
    tiles_per = (counts + tm - 1) // tm
    tile_end = jnp.cumsum(tiles_per)
    tile_start = tile_end - tiles_per
    n_used = tile_end[-1]
    pos = tile_start[ids] * tm + rank
    row_token = jnp.zeros((n_tiles * tm,), jnp.int32).at[pos].set(
        jnp.arange(n_assign, dtype=jnp.int32) // TOP_K)

    tile_ids = jnp.arange(n_tiles, dtype=jnp.int32)
    in_use = tile_ids < n_used
    tile_expert = jnp.sum((jnp.minimum(tile_ids, n_used - 1)[:, None] >= tile_end[None, :])
                          .astype(jnp.int32), axis=1)
    used = counts > 0
    ordinal = jnp.cumsum(used.astype(jnp.int32)) - 1
    later = lax.cummin(jnp.where(used, experts, n_experts), axis=0, reverse=True)
    next_used = jnp.concatenate([later[1:], jnp.full((1,), n_experts, jnp.int32)])
    next_used = jnp.where(next_used < n_experts, next_used, -1)
    plan = jnp.stack([
        tile_expert,
        jnp.logical_and(in_use, tile_ids == tile_start[tile_expert]).astype(jnp.int32),
        ordinal[tile_expert] % 2,
        next_used[tile_expert],
        jnp.where(in_use, _round_up(jnp.clip(
            counts[tile_expert] - (tile_ids - tile_start[tile_expert]) * tm, 0, tm),
            GATHER_UNROLL), 0),
    ]).astype(jnp.int32)
    return pos.astype(jnp.int32), row_token, plan, n_used.reshape(1).astype(jnp.int32)


def hierarchical_moe_ln(x, w_group, b_group, w_expert, b_expert, w1, w3, w2, g, b, *,
                        layer, with_bf16_copy, tm_route, tm_moe, tm_comb):
    T, D = x.shape
    E = w1.shape[1]
    n_route = N_GROUPS + E
    w_route = jnp.zeros((D, LANES), F32).at[:, :N_GROUPS].set(w_group).at[:, N_GROUPS:n_route].set(w_expert)
    b_route = jnp.zeros((1, LANES), F32).at[0, :N_GROUPS].set(b_group).at[0, N_GROUPS:n_route].set(b_expert)
    ids, wts = router(x, w_route, b_route, tm=tm_route)
    n_tiles = (T * TOP_K + E * (tm_moe - 1)) // tm_moe
    pos, row_token, plan, n_used = _dispatch_plan(
        ids[:, :TOP_K].reshape(-1), n_experts=E, tm=tm_moe, n_tiles=n_tiles)
    h = moe_up(x, w1, w3, plan, row_token, n_used, layer=layer, tm=tm_moe)
    y = moe_down(h, w2, plan, n_used, layer=layer, tm=tm_moe)
    out = moe_combine_ln(y, pos, x, wts, g, b, tm=tm_comb, with_bf16_copy=with_bf16_copy)
    return out[0], (out[1] if with_bf16_copy else None)


def kernel(x, even_w_in, even_w_pool, even_pool_scale, even_b_f, even_w_out, odd_w_in, odd_conv_w, odd_conv_b, odd_w_a, odd_b_a, odd_w_x, odd_b_x, odd_lambda, odd_w_out, moe_w_group, moe_b_group, moe_w_expert, moe_b_expert, moe_w1, moe_w3, moe_w2, ln_g, ln_b):
    B, S, D = x.shape
    T = B * S
    xf = x.reshape(T, D)
    xb = None
    tiles = _tile_plan(T, S, D)

    for layer in range(DEPTH):
        i = layer // 2
        if layer % 2 == 0:
            pool_w = even_w_pool.shape[1] * even_w_pool.shape[2]
            heads = even_b_f.shape[1]
            fox_w = heads * HEAD_DIM
            w_in_t = jnp.swapaxes(even_w_in[i], 0, 1)
            lane0 = LANES - heads
            w_f = w_in_t[w_in_t.shape[0] - LANES:, :]
            b_f = jnp.zeros((1, LANES), F32).at[0, lane0:].set(even_b_f[i])
            c, xb = forget_cumsum(xf, w_f, b_f, batch=B, tm=tiles["seq"])
            proj = matmul(xb, w_in_t, n_cols=pool_w + 3 * fox_w, tm=tiles["mm_m"],
                          tn=tiles["mm_n"], out_dtype=BF16, w_is_transposed=True)
            a_out = pool_mixer(proj, even_w_pool[i].astype(BF16), even_pool_scale[i],
                               batch=B, ts=tiles["seq"])
            qc = pool_w // HEAD_DIM
            b_out = forgetting_attention(proj, c, batch=B, heads=heads, head_lane0=lane0,
                                         q_col=qc, k_col=qc + heads, v_col=qc + 2 * heads,
                                         tq=tiles["attn_q"], tk=tiles["attn_k"])
            mixed = [a_out, b_out]
            w_out = even_w_out[i].astype(BF16)
        else:
            if xb is None:
                xb = xf.astype(BF16)
            proj = matmul(xb, odd_w_in[i], n_cols=odd_w_in.shape[2], tm=tiles["mm_m"],
                          tn=tiles["mm_n"], out_dtype=BF16)
            mixed = [rglru_mixer(proj, odd_conv_w[i], odd_conv_b[i], odd_w_a[i].astype(BF16),
                                 odd_b_a[i].reshape(-1), odd_w_x[i].astype(BF16),
                                 odd_b_x[i].reshape(-1), odd_lambda[i], batch=B, ts=tiles["seq"])]
            w_out = odd_w_out[i].astype(BF16)
        xf = matmul_residual_ln(mixed, w_out, xf, ln_g[layer, 0], ln_b[layer, 0],
                                tm=tiles["ln_m"])
        xf, xb = hierarchical_moe_ln(xf, moe_w_group[layer], moe_b_group[layer],
                                     moe_w_expert[layer], moe_b_expert[layer], moe_w1, moe_w3,
                                     moe_w2, ln_g[layer, 1], ln_b[layer, 1], layer=layer,
                                     with_bf16_copy=layer + 1 < DEPTH,
                                     tm_route=tiles["route"], tm_moe=tiles["moe"],
                                     tm_comb=tiles["comb"])
    return xf.reshape(B, S, D)


def _tile_plan(T, S, D):
    return {
        "mm_m": min(1024, T), "mm_n": min(1024, D),
        "seq": min(512, S), "attn_q": min(1024, S), "attn_k": min(512, S),
        "ln_m": min(256, T),
        "route": min(512, T), "moe": 256, "comb": min(256, T),
    }
```

```python
import functools

import jax
import jax.numpy as jnp
from jax import lax
from jax.experimental import pallas as pl
from jax.experimental.pallas import tpu as pltpu

F32 = jnp.float32
BF16 = jnp.bfloat16

POOL_WINDOWS = (2, 4, 8, 16)
POOL_HALO = 16
HEAD_DIM = 128
LRU_BLOCKS = 16
CONV_WIDTH = 4
CONV_HALO = 8
LRU_C = 8.0
N_GROUPS = 4
EXPERTS_PER_GROUP = 8
TOP_K = 2
DEPTH = 2
DN_ALPHA = (2.0 * DEPTH) ** 0.25
LN_EPS = 1e-5

LANES = 128
SUBLANES = 8
VMEM_LIMIT_MB = 56
OUT_PROJ_VMEM_MB = 60


def _params(semantics, vmem_mb=VMEM_LIMIT_MB):
    return pltpu.CompilerParams(dimension_semantics=semantics,
                                vmem_limit_bytes=vmem_mb * 1024 * 1024)


_CONTRACT_LAST = (((1,), (1,)), ((), ()))


def _mm_kernel(a_ref, w_hbm, o_ref, w_stage, w_bf, sem, *, w_is_transposed, tn):
    j = pl.program_id(0)
    i = pl.program_id(1)

    def w_copy(block):
        cols = pl.ds(pl.multiple_of(block * tn, tn), tn)
        src = w_hbm.at[cols, :] if w_is_transposed else w_hbm.at[:, cols]
        return pltpu.make_async_copy(src, w_stage, sem)

    @pl.when(jnp.logical_and(j == 0, i == 0))
    def _():
        w_copy(0).start()

    @pl.when(i == 0)
    def _():
        w_copy(j).wait()
        w_bf[...] = w_stage[...].astype(BF16)

        @pl.when(j + 1 < pl.num_programs(0))
        def _():
            w_copy(j + 1).start()

    if w_is_transposed:
        out = lax.dot_general(a_ref[...], w_bf[...], _CONTRACT_LAST, preferred_element_type=F32)
    else:
        out = jnp.dot(a_ref[...], w_bf[...], preferred_element_type=F32)
    o_ref[...] = out.astype(o_ref.dtype)


def matmul(a, w, *, n_cols, tm, tn, out_dtype, w_is_transposed=False):
    M, K = a.shape
    w_block = (tn, K) if w_is_transposed else (K, tn)
    return pl.pallas_call(
        functools.partial(_mm_kernel, w_is_transposed=w_is_transposed, tn=tn),
        grid=(n_cols // tn, M // tm),
        in_specs=[pl.BlockSpec((tm, K), lambda j, i: (i, 0)),
                  pl.BlockSpec(memory_space=pl.ANY)],
        out_specs=pl.BlockSpec((tm, tn), lambda j, i: (i, j)),
        out_shape=jax.ShapeDtypeStruct((M, n_cols), out_dtype),
        scratch_shapes=[pltpu.VMEM(w_block, F32), pltpu.VMEM(w_block, BF16),
                        pltpu.SemaphoreType.DMA(())],
        compiler_params=_params(("arbitrary", "arbitrary")),
        name="proj_matmul",
    )(a, w)


def _layer_norm_rows(z, g, b):
    mu = jnp.mean(z, axis=-1, keepdims=True)
    d = z - mu
    var = jnp.mean(d * d, axis=-1, keepdims=True)
    return d * lax.rsqrt(var + LN_EPS) * g + b


LN_CHUNK_ROWS = 64


def _mm_ln_kernel(*refs, n_a):
    a_refs = refs[:n_a]
    w_hbm, x_hbm, g_ref, b_ref, o_ref, w_buf, acc_ref, sem = refs[n_a:]
    i = pl.program_id(0)
    tm = o_ref.shape[0]

    @pl.when(i == 0)
    def _():
        w_copy = pltpu.make_async_copy(w_hbm, w_buf, sem.at[0])
        w_copy.start()
        w_copy.wait()

    x_copy = pltpu.make_async_copy(x_hbm.at[pl.ds(pl.multiple_of(i * tm, tm), tm), :], o_ref,
                                   sem.at[1])
    x_copy.start()
    k_each = a_refs[0].shape[1]
    for idx, a_ref in enumerate(a_refs):
        part = jnp.dot(a_ref[...], w_buf[idx * k_each:(idx + 1) * k_each, :],
                       preferred_element_type=F32)
        if idx == 0:
            acc_ref[...] = part
        else:
            acc_ref[...] += part
    x_copy.wait()

    def normalise(c, carry):
        rows = pl.ds(pl.multiple_of(c * LN_CHUNK_ROWS, LN_CHUNK_ROWS), LN_CHUNK_ROWS)
        o_ref[rows, :] = _layer_norm_rows(DN_ALPHA * o_ref[rows, :] + acc_ref[rows, :],
                                          g_ref[...], b_ref[...])
        return carry

    lax.fori_loop(0, tm // LN_CHUNK_ROWS, normalise, 0)


def matmul_residual_ln(a_list, w, x, g, b, *, tm):
    M, K_each = a_list[0].shape
    K, N = w.shape
    n_a = len(a_list)
    assert n_a * K_each == K and tm % LN_CHUNK_ROWS == 0
    return pl.pallas_call(
        functools.partial(_mm_ln_kernel, n_a=n_a),
        grid=(M // tm,),
        in_specs=[pl.BlockSpec((tm, K_each), lambda i: (i, 0)) for _ in range(n_a)] + [
            pl.BlockSpec(memory_space=pl.ANY),
            pl.BlockSpec(memory_space=pl.ANY),
            pl.BlockSpec((1, N), lambda i: (0, 0)),
            pl.BlockSpec((1, N), lambda i: (0, 0))],
        out_specs=pl.BlockSpec((tm, N), lambda i: (i, 0)),
        out_shape=jax.ShapeDtypeStruct((M, N), F32),
        scratch_shapes=[pltpu.VMEM((K, N), BF16), pltpu.VMEM((tm, N), F32),
                        pltpu.SemaphoreType.DMA((2,))],
        compiler_params=_params(("arbitrary",), vmem_mb=OUT_PROJ_VMEM_MB),
        name="out_proj_ln",
    )(*a_list, w, x, g.reshape(1, N), b.reshape(1, N))


def _cumsum_rows(v, n_rows):
    row = lax.broadcasted_iota(jnp.int32, v.shape, 0)
    shift = 1
    while shift < n_rows:
        v = v + jnp.where(row >= shift, pltpu.roll(v, shift, 0), 0.0)
        shift *= 2
    return v


def _forget_kernel(x_ref, w_ref, bf_ref, c_ref, xb_ref, carry_ref, *, tm):
    @pl.when(pl.program_id(1) == 0)
    def _():
        carry_ref[...] = jnp.zeros_like(carry_ref)

    xb = x_ref[...].astype(BF16)
    xb_ref[...] = xb
    z = lax.dot_general(xb, w_ref[...].astype(BF16), _CONTRACT_LAST,
                        preferred_element_type=F32) + bf_ref[...]
    log_f = jnp.minimum(z, 0.0) - jnp.log1p(jnp.exp(-jnp.abs(z)))
    c = _cumsum_rows(log_f, tm) + carry_ref[...]
    c_ref[...] = c
    carry_ref[...] = c[tm - 1:tm, :]


def forget_cumsum(x, w_f, b_f, *, batch, tm):
    T, D = x.shape
    steps = T // batch // tm
    return pl.pallas_call(
        functools.partial(_forget_kernel, tm=tm),
        grid=(batch, steps),
        in_specs=[pl.BlockSpec((tm, D), lambda b, s: (b * steps + s, 0)),
                  pl.BlockSpec((LANES, D), lambda b, s: (0, 0)),
                  pl.BlockSpec((1, LANES), lambda b, s: (0, 0))],
        out_specs=[pl.BlockSpec((tm, LANES), lambda b, s: (b * steps + s, 0)),
                   pl.BlockSpec((tm, D), lambda b, s: (b * steps + s, 0))],
        out_shape=[jax.ShapeDtypeStruct((T, LANES), F32), jax.ShapeDtypeStruct((T, D), BF16)],
        scratch_shapes=[pltpu.VMEM((1, LANES), F32)],
        compiler_params=_params(("parallel", "arbitrary")),
        name="forget_cumsum",
    )(x, w_f, b_f)


def _pool_kernel(u_ref, w_ref, sc_ref, o_ref, ext_ref, *, ts):
    g = pl.program_id(1)
    si = pl.program_id(2)

    @pl.when(si == 0)
    def _():
        ext_ref[0:POOL_HALO, :] = jnp.zeros((POOL_HALO, ext_ref.shape[1]), F32)

    @pl.when(si > 0)
    def _():
        ext_ref[0:POOL_HALO, :] = ext_ref[ts:ts + POOL_HALO, :]

    u = u_ref[...].astype(F32)
    ext_ref[POOL_HALO:, :] = u
    e1 = ext_ref[...]
    e2 = e1 + pltpu.roll(e1, 1, 0)
    e4 = e2 + pltpu.roll(e2, 2, 0)
    e8 = e4 + pltpu.roll(e4, 4, 0)
    e16 = e8 + pltpu.roll(e8, 8, 0)
    win = jnp.where(g == 0, e2, jnp.where(g == 1, e4, jnp.where(g == 2, e8, e16)))
    win = win[POOL_HALO:, :]
    width = jnp.left_shift(2, g)
    t = si * ts + lax.broadcasted_iota(jnp.int32, (ts, 1), 0)
    cnt = jnp.minimum(t + 1, width).astype(F32)
    pooled = win / cnt - u
    y = jnp.dot(pooled.astype(BF16), w_ref[...], preferred_element_type=F32)
    o_ref[...] = (y * sc_ref[...]).astype(o_ref.dtype)


def pool_mixer(proj, w_pool_b, pool_scale, *, batch, ts):
    T = proj.shape[0]
    G, C, _ = w_pool_b.shape
    assert POOL_WINDOWS == (2, 4, 8, 16) and G == len(POOL_WINDOWS)
    steps = T // batch // ts
    return pl.pallas_call(
        functools.partial(_pool_kernel, ts=ts),
        grid=(batch, G, steps),
        in_specs=[pl.BlockSpec((ts, C), lambda b, g, s: (b * steps + s, g)),
                  pl.BlockSpec((None, C, C), lambda b, g, s: (g, 0, 0)),
                  pl.BlockSpec((1, C), lambda b, g, s: (0, g))],
        out_specs=pl.BlockSpec((ts, C), lambda b, g, s: (b * steps + s, g)),
        out_shape=jax.ShapeDtypeStruct((T, G * C), BF16),
        scratch_shapes=[pltpu.VMEM((ts + POOL_HALO, C), F32)],
        compiler_params=_params(("parallel", "parallel", "arbitrary")),
        name="pool_mixer",
    )(proj, w_pool_b, pool_scale.reshape(1, G * C))


LOG2E = 1.4426950408889634
N_BIAS_PARTS = 3


def _attn_kernel(q_ref, k_ref, v_ref, c_ref, o_ref, kaug_ref, vaug_ref, qext_ref,
                 qaug_ref, sa_ref, sb_ref, m_ref, acc_ref, *, tq, tk, scale, head_lane0):
    h = pl.program_id(1)
    qi = pl.program_id(2)
    S = k_ref.shape[0]

    @pl.when(qi == 0)
    def _():
        lane = lax.broadcasted_iota(jnp.int32, (S, LANES), 1)
        c = jnp.sum(jnp.where(lane == head_lane0 + h, c_ref[...], 0.0), axis=1,
                    keepdims=True) * LOG2E
        hi = c.astype(BF16).astype(F32)
        mid = (c - hi).astype(BF16).astype(F32)
        lo = c - hi - mid
        ones = jnp.where(lane < N_BIAS_PARTS, 1.0, 0.0)
        k_ext = jnp.where(lane == 3, -hi, jnp.where(lane == 4, -mid, jnp.where(lane == 5, -lo, ones)))
        q_ext = jnp.where(lane == 0, hi, jnp.where(lane == 1, mid, jnp.where(
            lane == 2, lo, jnp.where(lane < 2 * N_BIAS_PARTS, 1.0, 0.0))))
        kaug_ref[:, :HEAD_DIM] = k_ref[...]
        kaug_ref[:, HEAD_DIM:] = k_ext.astype(BF16)
        vaug_ref[:, :HEAD_DIM] = v_ref[...]
        vaug_ref[:, HEAD_DIM:] = jnp.where(lane == 0, 1.0, 0.0).astype(BF16)
        qext_ref[...] = q_ext.astype(BF16)

    q0 = pl.multiple_of(qi * tq, tq)
    qaug_ref[:, :HEAD_DIM] = (q_ref[...].astype(F32) * (scale * LOG2E)).astype(BF16)
    qaug_ref[:, HEAD_DIM:] = qext_ref[pl.ds(q0, tq), :]
    m_ref[...] = jnp.full(m_ref.shape, -jnp.inf, F32)
    acc_ref[...] = jnp.zeros(acc_ref.shape, F32)

    def scores(block, lo=0):
        k0 = pl.multiple_of(block * tk, tk)
        return lax.dot_general(qaug_ref[lo:, :], kaug_ref[pl.ds(k0, tk), :],
                               (((1,), (1,)), ((), ())), preferred_element_type=F32)

    def softmax_pv(s, block, lo=0):
        k0 = pl.multiple_of(block * tk, tk)
        m = m_ref[lo:, :]
        m_new = jnp.maximum(m, jnp.max(s, axis=1, keepdims=True))
        alpha = jnp.exp2(m - m_new)
        p = jnp.exp2((s - m_new).astype(BF16))
        acc_ref[lo:, :] = alpha * acc_ref[lo:, :] + jnp.dot(
            p, vaug_ref[pl.ds(k0, tk), :], preferred_element_type=F32)
        m_ref[lo:, :] = m_new

    n_full = qi * (tq // tk)
    sa_ref[...] = scores(0)

    def body(i, carry):
        j = 2 * i
        sb_ref[...] = scores(j + 1)
        softmax_pv(sa_ref[...], j)
        sa_ref[...] = scores(j + 2)
        softmax_pv(sb_ref[...], j + 1)
        return carry

    lax.fori_loop(0, n_full // 2, body, 0)

    diag_scores = [sa_ref[...]] + [scores(n_full + u, u * tk) for u in range(1, tq // tk)]
    for u, s in enumerate(diag_scores):
        lo = u * tk
        row = lax.broadcasted_iota(jnp.int32, s.shape, 0)
        col = lax.broadcasted_iota(jnp.int32, s.shape, 1)
        softmax_pv(jnp.where(col <= row, s, -jnp.inf), n_full + u, lo)
        acc = acc_ref[lo:lo + tk, :]
        o_ref[lo:lo + tk, :] = (acc[:, :HEAD_DIM] / acc[:, HEAD_DIM:HEAD_DIM + 1]
                                ).astype(o_ref.dtype)


def forgetting_attention(proj, c, *, batch, heads, head_lane0, q_col, k_col, v_col, tq, tk):
    T = proj.shape[0]
    S = T // batch
    nq = S // tq
    assert (tq // tk) % 2 == 0, "full key blocks are consumed two per loop trip"
    return pl.pallas_call(
        functools.partial(_attn_kernel, tq=tq, tk=tk, scale=HEAD_DIM ** -0.5,
                          head_lane0=head_lane0),
        grid=(batch, heads, nq),
        in_specs=[pl.BlockSpec((tq, HEAD_DIM), lambda b, h, i: (b * nq + i, q_col + h)),
                  pl.BlockSpec((S, HEAD_DIM), lambda b, h, i: (b, k_col + h)),
                  pl.BlockSpec((S, HEAD_DIM), lambda b, h, i: (b, v_col + h)),
                  pl.BlockSpec((S, LANES), lambda b, h, i: (b, 0))],
        out_specs=pl.BlockSpec((tq, HEAD_DIM), lambda b, h, i: (b * nq + i, h)),
        out_shape=jax.ShapeDtypeStruct((T, heads * HEAD_DIM), BF16),
        scratch_shapes=[pltpu.VMEM((S, 2 * HEAD_DIM), BF16), pltpu.VMEM((S, 2 * HEAD_DIM), BF16),
                        pltpu.VMEM((S, LANES), BF16), pltpu.VMEM((tq, 2 * HEAD_DIM), BF16),
                        pltpu.VMEM((tq, tk), F32), pltpu.VMEM((tq, tk), F32),
                        pltpu.VMEM((tq, 1), F32), pltpu.VMEM((tq, 2 * HEAD_DIM), F32)],
        compiler_params=_params(("parallel", "parallel", "arbitrary")),
        name="forgetting_attention",
    )(proj, proj, proj, c)


def _sigmoid(z):
    return 0.5 * jnp.tanh(0.5 * z) + 0.5


def _rglru_kernel(xb_ref, gate_ref, cw_ref, cb_ref, wa_ref, ba_ref, wx_ref, bx_ref, lam_ref,
                  y_ref, ext_ref, a_ref, b_ref, h_ref, carry_ref, *, ts):
    si = pl.program_id(2)
    C = xb_ref.shape[1]

    @pl.when(si == 0)
    def _():
        ext_ref[0:CONV_HALO, :] = jnp.zeros((CONV_HALO, C), F32)
        carry_ref[...] = jnp.zeros_like(carry_ref)

    @pl.when(si > 0)
    def _():
        ext_ref[0:CONV_HALO, :] = ext_ref[ts:ts + CONV_HALO, :]

    xb = xb_ref[...].astype(F32)
    ext_ref[CONV_HALO:, :] = xb
    cw = cw_ref[...]
    xc = cb_ref[...] + xb * cw[CONV_WIDTH - 1:CONV_WIDTH, :]
    for lag in range(1, CONV_WIDTH):
        tap = CONV_WIDTH - 1 - lag
        xc = xc + ext_ref[CONV_HALO - lag:CONV_HALO - lag + ts, :] * cw[tap:tap + 1, :]

    xcb = xc.astype(BF16)
    r = _sigmoid(jnp.dot(xcb, wa_ref[...], preferred_element_type=F32) + ba_ref[...])
    ig = _sigmoid(jnp.dot(xcb, wx_ref[...], preferred_element_type=F32) + bx_ref[...])
    lam = lam_ref[...]
    log_a_base = -(jnp.maximum(-lam, 0.0) + jnp.log1p(jnp.exp(-jnp.abs(lam))))
    log_a = LRU_C * r * log_a_base
    a = jnp.exp(log_a)
    mult = jnp.exp2(0.5 * jnp.log2(-jnp.tanh(log_a) * (1.0 + a * a)))
    b = mult * ig * xc

    groups = ts // SUBLANES
    a3 = a.reshape(groups, SUBLANES, C)
    b3 = b.reshape(groups, SUBLANES, C)
    sub = lax.broadcasted_iota(jnp.int32, a3.shape, 1)
    shift = 1
    while shift < SUBLANES:
        keep = sub >= shift
        b3 = jnp.where(keep, a3 * pltpu.roll(b3, shift, 1) + b3, b3)
        a3 = jnp.where(keep, a3 * pltpu.roll(a3, shift, 1), a3)
        shift *= 2
    a_ref[...] = a3.reshape(ts, C)
    b_ref[...] = b3.reshape(ts, C)

    def body(gi, carry):
        r0 = pl.multiple_of(gi * SUBLANES, SUBLANES)
        hg = a_ref[pl.ds(r0, SUBLANES), :] * carry + b_ref[pl.ds(r0, SUBLANES), :]
        h_ref[pl.ds(r0, SUBLANES), :] = hg
        return hg[SUBLANES - 1:SUBLANES, :]

    carry_ref[...] = lax.fori_loop(0, groups, body, carry_ref[...], unroll=8)

    gate = gate_ref[...].astype(F32)
    gelu = 0.5 * gate * (1.0 + jnp.tanh(0.7978845608028654 * (gate + 0.044715 * gate * gate * gate)))
    y_ref[...] = (h_ref[...] * gelu).astype(y_ref.dtype)


def rglru_mixer(proj, conv_w, conv_b, w_a_b, b_a, w_x_b, b_x, lam, *, batch, ts):
    T = proj.shape[0]
    NB, C, _ = w_a_b.shape
    W = NB * C
    steps = T // batch // ts
    vec = lambda: pl.BlockSpec((1, C), lambda b, c, s: (0, c))
    blk = lambda: pl.BlockSpec((None, C, C), lambda b, c, s: (c, 0, 0))
    return pl.pallas_call(
        functools.partial(_rglru_kernel, ts=ts),
        grid=(batch, NB, steps),
        in_specs=[pl.BlockSpec((ts, C), lambda b, c, s: (b * steps + s, NB + c)),
                  pl.BlockSpec((ts, C), lambda b, c, s: (b * steps + s, c)),
                  pl.BlockSpec((CONV_WIDTH, C), lambda b, c, s: (0, c)),
                  vec(), blk(), vec(), blk(), vec(), vec()],
        out_specs=pl.BlockSpec((ts, C), lambda b, c, s: (b * steps + s, c)),
        out_shape=jax.ShapeDtypeStruct((T, W), BF16),
        scratch_shapes=[pltpu.VMEM((ts + CONV_HALO, C), F32),
                        pltpu.VMEM((ts, C), F32), pltpu.VMEM((ts, C), F32),
                        pltpu.VMEM((ts, C), F32), pltpu.VMEM((1, C), F32)],
        compiler_params=_params(("parallel", "parallel", "arbitrary")),
        name="rglru_mixer",
    )(proj, proj, conv_w, conv_b.reshape(1, W), w_a_b, b_a.reshape(1, W), w_x_b,
      b_x.reshape(1, W), lam.reshape(1, W))


def _router_kernel(x_ref, w_ref, b_ref, id_ref, wt_ref):
    logits = jnp.dot(x_ref[...].astype(BF16), w_ref[...].astype(BF16),
                     preferred_element_type=F32) + b_ref[...]
    lane = lax.broadcasted_iota(jnp.int32, logits.shape, 1).astype(F32)
    neg = -jnp.inf
    big = float(LANES)

    is_group = lane < N_GROUPS
    gl = jnp.where(is_group, logits, neg)
    gmax = jnp.max(gl, axis=1, keepdims=True)
    g_idx = jnp.min(jnp.where(gl == gmax, lane, big), axis=1, keepdims=True)
    g_w = 1.0 / jnp.sum(jnp.where(is_group, jnp.exp(gl - gmax), 0.0), axis=1, keepdims=True)

    lo = N_GROUPS + g_idx * EXPERTS_PER_GROUP
    el = jnp.where(lane >= lo, jnp.where(lane < lo + EXPERTS_PER_GROUP, logits, neg), neg)
    v1 = jnp.max(el, axis=1, keepdims=True)
    i1 = jnp.min(jnp.where(el == v1, lane, big), axis=1, keepdims=True)
    el2 = jnp.where(lane == i1, neg, el)
    v2 = jnp.max(el2, axis=1, keepdims=True)
    i2 = jnp.min(jnp.where(el2 == v2, lane, big), axis=1, keepdims=True)
    t = jnp.exp(v2 - v1)
    w1 = g_w / (1.0 + t)
    w2 = w1 * t
    ids = jnp.where(lane == 0, i1 - N_GROUPS, jnp.where(lane == 1, i2 - N_GROUPS, 0.0))
    id_ref[...] = ids.astype(jnp.int32)
    wt_ref[...] = jnp.where(lane == 0, w1, jnp.where(lane == 1, w2, 0.0))


def router(x, w_route, b_route, *, tm):
    T, D = x.shape
    return pl.pallas_call(
        _router_kernel,
        grid=(T // tm,),
        in_specs=[pl.BlockSpec((tm, D), lambda i: (i, 0)),
                  pl.BlockSpec((D, LANES), lambda i: (0, 0)),
                  pl.BlockSpec((1, LANES), lambda i: (0, 0))],
        out_specs=[pl.BlockSpec((tm, LANES), lambda i: (i, 0)),
                   pl.BlockSpec((tm, LANES), lambda i: (i, 0))],
        out_shape=[jax.ShapeDtypeStruct((T, LANES), jnp.int32),
                   jax.ShapeDtypeStruct((T, LANES), F32)],
        compiler_params=_params(("parallel",)),
        name="router",
    )(x, w_route, b_route)


GATHER_UNROLL = SUBLANES


def _row_gather_start(src_hbm, dst_ref, sem, row_of, n_rows):
    def body(g, carry):
        for u in range(GATHER_UNROLL):
            r = g * GATHER_UNROLL + u
            pltpu.make_async_copy(src_hbm.at[pl.ds(row_of(r), 1), :],
                                  dst_ref.at[pl.ds(r, 1), :], sem).start()
        return carry
    lax.fori_loop(0, n_rows // GATHER_UNROLL, body, 0)


def _row_gather_wait(src_hbm, dst_ref, sem, n_rows):
    if not isinstance(n_rows, int):
        n_rows = pl.multiple_of(n_rows, GATHER_UNROLL)
    pltpu.make_async_copy(src_hbm.at[pl.ds(0, n_rows), :], dst_ref.at[pl.ds(0, n_rows), :],
                          sem).wait()


PLAN_EXPERT, PLAN_FIRST, PLAN_SLOT, PLAN_NEXT, PLAN_ROWS = range(5)


def _expert_weight_copies(w_hbm_list, buf_list, sem, layer, expert, slot):
    return [pltpu.make_async_copy(w.at[layer, expert], buf.at[slot], sem.at[slot])
            for w, buf in zip(w_hbm_list, buf_list)]


def _expert_weight_pipeline(plan_ref, n_used, w_hbm_list, buf_list, sem, layer):
    j = pl.program_id(0)
    slot = plan_ref[PLAN_SLOT, j]
    first = jnp.logical_and(j < n_used, plan_ref[PLAN_FIRST, j] == 1)

    @pl.when(j == 0)
    def _():
        for cp in _expert_weight_copies(w_hbm_list, buf_list, sem, layer,
                                        plan_ref[PLAN_EXPERT, 0], 0):
            cp.start()

    @pl.when(jnp.logical_and(first, plan_ref[PLAN_NEXT, j] >= 0))
    def _():
        for cp in _expert_weight_copies(w_hbm_list, buf_list, sem, layer,
                                        plan_ref[PLAN_NEXT, j], 1 - slot):
            cp.start()

    @pl.when(first)
    def _():
        for cp in _expert_weight_copies(w_hbm_list, buf_list, sem, layer,
                                        plan_ref[PLAN_EXPERT, j], slot):
            cp.wait()

    return slot


def _moe_up_kernel(plan_ref, rt_ref, nu_ref, x_hbm, w1_hbm, w3_hbm, h_ref,
                   xg_even, xg_odd, w1_buf, w3_buf, xsem, wsem, *, tm, layer):
    j = pl.program_id(0)
    n_used = nu_ref[0]
    xg = (xg_even, xg_odd)

    @pl.when(j == 0)
    def _():
        xg_even[...] = jnp.zeros_like(xg_even)
        xg_odd[...] = jnp.zeros_like(xg_odd)
        _row_gather_start(x_hbm, xg_even, xsem.at[0], lambda r: rt_ref[r],
                          plan_ref[PLAN_ROWS, 0])

    wslot = _expert_weight_pipeline(plan_ref, n_used, (w1_hbm, w3_hbm), (w1_buf, w3_buf),
                                    wsem, layer)

    def swiglu(parity):
        x = xg[parity][...].astype(BF16)
        a = jnp.dot(x, w1_buf[wslot].astype(BF16), preferred_element_type=F32)
        b = jnp.dot(x, w3_buf[wslot].astype(BF16), preferred_element_type=F32)
        h_ref[...] = (a * jax.nn.sigmoid(a) * b).astype(h_ref.dtype)

    for parity in (0, 1):
        mine = jnp.logical_and(j < n_used, j % 2 == parity)

        @pl.when(jnp.logical_and(mine, j + 1 < n_used))
        def _(parity=parity):
            _row_gather_start(x_hbm, xg[1 - parity], xsem.at[1 - parity],
                              lambda r: rt_ref[(j + 1) * tm + r], plan_ref[PLAN_ROWS, j + 1])

        @pl.when(mine)
        def _(parity=parity):
            _row_gather_wait(x_hbm, xg[parity], xsem.at[parity], plan_ref[PLAN_ROWS, j])
            swiglu(parity)

    @pl.when(j >= n_used)
    def _():
        h_ref[...] = jnp.zeros_like(h_ref)


def moe_up(x, w1, w3, plan, row_token, n_used, *, layer, tm):
    D = x.shape[1]
    FF = w1.shape[-1]
    n_tiles = plan.shape[1]
    any_spec = lambda: pl.BlockSpec(memory_space=pl.ANY)
    return pl.pallas_call(
        functools.partial(_moe_up_kernel, tm=tm, layer=layer),
        grid_spec=pltpu.PrefetchScalarGridSpec(
            num_scalar_prefetch=3,
            grid=(n_tiles,),
            in_specs=[any_spec(), any_spec(), any_spec()],
            out_specs=pl.BlockSpec((tm, FF), lambda j, plan, rt, nu: (j, 0)),
            scratch_shapes=[pltpu.VMEM((tm, D), F32), pltpu.VMEM((tm, D), F32),
                            pltpu.VMEM((2, D, FF), F32), pltpu.VMEM((2, D, FF), F32),
                            pltpu.SemaphoreType.DMA((2,)), pltpu.SemaphoreType.DMA((2,))]),
        out_shape=jax.ShapeDtypeStruct((n_tiles * tm, FF), BF16),
        compiler_params=_params(("arbitrary",)),
        name="moe_up",
    )(plan, row_token, n_used, x, w1, w3)


def _moe_down_kernel(plan_ref, nu_ref, h_ref, w2_hbm, y_ref, w2_buf, wsem, *, layer):
    j = pl.program_id(0)
    n_used = nu_ref[0]
    wslot = _expert_weight_pipeline(plan_ref, n_used, (w2_hbm,), (w2_buf,), wsem, layer)

    @pl.when(j < n_used)
    def _():
        y_ref[...] = jnp.dot(h_ref[...], w2_buf[wslot].astype(BF16),
                             preferred_element_type=F32)

    @pl.when(j >= n_used)
    def _():
        y_ref[...] = jnp.zeros_like(y_ref)


def moe_down(h, w2, plan, n_used, *, layer, tm):
    FF, D = w2.shape[-2:]
    n_tiles = plan.shape[1]
    tile = lambda j, plan, nu: (j, 0)
    return pl.pallas_call(
        functools.partial(_moe_down_kernel, layer=layer),
        grid_spec=pltpu.PrefetchScalarGridSpec(
            num_scalar_prefetch=2,
            grid=(n_tiles,),
            in_specs=[pl.BlockSpec((tm, FF), tile), pl.BlockSpec(memory_space=pl.ANY)],
            out_specs=pl.BlockSpec((tm, D), tile),
            scratch_shapes=[pltpu.VMEM((2, FF, D), F32), pltpu.SemaphoreType.DMA((2,))]),
        out_shape=jax.ShapeDtypeStruct((n_tiles * tm, D), F32),
        compiler_params=_params(("arbitrary",)),
        name="moe_down",
    )(plan, n_used, h, w2)


def _combine_ln_kernel(pos_ref, y_hbm, x_ref, wt_ref, g_ref, b_ref, o_ref, *rest, tm):
    ob_ref = rest[0] if len(rest) == 3 else None
    yg_ref, sem = rest[-2:]
    i = pl.program_id(0)

    def gather(tile, slot):
        for k in range(TOP_K):
            _row_gather_start(y_hbm, yg_ref.at[slot, k], sem.at[slot],
                              lambda r: pos_ref[TOP_K * (tile * tm + r) + k], tm)

    @pl.when(i == 0)
    def _():
        gather(0, 0)

    @pl.when(i + 1 < pl.num_programs(0))
    def _():
        gather(i + 1, (i + 1) % 2)

    slot = i % 2
    for k in range(TOP_K):
        _row_gather_wait(y_hbm, yg_ref.at[slot, k], sem.at[slot], tm)
    wt = wt_ref[...]
    z = DN_ALPHA * x_ref[...]
    for k in range(TOP_K):
        z = z + wt[:, k:k + 1] * yg_ref[slot, k]
    y = _layer_norm_rows(z, g_ref[...], b_ref[...])
    o_ref[...] = y
    if ob_ref is not None:
        ob_ref[...] = y.astype(BF16)


def moe_combine_ln(y, pos, x, wts, g, b, *, tm, with_bf16_copy):
    T, D = x.shape
    row = lambda i, pos: (i, 0)
    fixed = lambda i, pos: (0, 0)
    n_out = 2 if with_bf16_copy else 1
    return pl.pallas_call(
        functools.partial(_combine_ln_kernel, tm=tm),
        grid_spec=pltpu.PrefetchScalarGridSpec(
            num_scalar_prefetch=1,
            grid=(T // tm,),
            in_specs=[pl.BlockSpec(memory_space=pl.ANY),
                      pl.BlockSpec((tm, D), row),
                      pl.BlockSpec((tm, LANES), row),
                      pl.BlockSpec((1, D), fixed),
                      pl.BlockSpec((1, D), fixed)],
            out_specs=[pl.BlockSpec((tm, D), row)] * n_out,
            scratch_shapes=[pltpu.VMEM((2, TOP_K, tm, D), F32),
                            pltpu.SemaphoreType.DMA((2,))]),
        out_shape=[jax.ShapeDtypeStruct((T, D), F32), jax.ShapeDtypeStruct((T, D), BF16)][:n_out],
        compiler_params=_params(("arbitrary",)),
        name="moe_combine_ln",
    )(pos, y, x, wts, g.reshape(1, D), b.reshape(1, D))


def _round_up(v, multiple):
    return (v + multiple - 1) // multiple * multiple


def _dispatch_plan(ids, *, n_experts, tm, n_tiles):
    n_assign = ids.shape[0]
    experts = jnp.arange(n_experts, dtype=jnp.int32)
    onehot = (ids[:, None] == experts[None, :]).astype(jnp.int32)
    csum = jnp.cumsum(onehot, axis=0)
    rank = jnp.sum((csum - onehot) * onehot, axis=1)
    counts = csum[-1]
    tiles_per = (counts + tm - 1) // tm
    tile_end = jnp.cumsum(tiles_per)
    tile_start = tile_end - tiles_per
    n_used = tile_end[-1]
    pos = tile_start[ids] * tm + rank
    row_token = jnp.zeros((n_tiles * tm,), jnp.int32).at[pos].set(
        jnp.arange(n_assign, dtype=jnp.int32) // TOP_K)

    tile_ids = jnp.arange(n_tiles, dtype=jnp.int32)
    in_use = tile_ids < n_used
    tile_expert = jnp.sum((jnp.minimum(tile_ids, n_used - 1)[:, None] >= tile_end[None, :])
                          .astype(jnp.int32), axis=1)
    used = counts > 0
    ordinal = jnp.cumsum(used.astype(jnp.int32)) - 1
    later = lax.cummin(jnp.where(used, experts, n_experts), axis=0, reverse=True)
    next_used = jnp.concatenate([later[1:], jnp.full((1,), n_experts, jnp.int32)])
    next_used = jnp.where(next_used < n_experts, next_used, -1)
    plan = jnp.stack([
        tile_expert,
        jnp.logical_and(in_use, tile_ids == tile_start[tile_expert]).astype(jnp.int32),
        ordinal[tile_expert] % 2,
        next_used[tile_expert],
        jnp.where(in_use, _round_up(jnp.clip(
            counts[tile_expert] - (tile_ids - tile_start[tile_expert]) * tm, 0, tm),
            GATHER_UNROLL), 0),
    ]).astype(jnp.int32)
    return pos.astype(jnp.int32), row_token, plan, n_used.reshape(1).astype(jnp.int32)


def hierarchical_moe_ln(x, w_group, b_group, w_expert, b_expert, w1, w3, w2, g, b, *,
                        layer, with_bf16_copy, tm_route, tm_moe, tm_comb):
    T, D = x.shape
    E = w1.shape[1]
    n_route = N_GROUPS + E
    w_route = jnp.zeros((D, LANES), F32).at[:, :N_GROUPS].set(w_group).at[:, N_GROUPS:n_route].set(w_expert)
    b_route = jnp.zeros((1, LANES), F32).at[0, :N_GROUPS].set(b_group).at[0, N_GROUPS:n_route].set(b_expert)
    ids, wts = router(x, w_route, b_route, tm=tm_route)
    n_tiles = (T * TOP_K + E * (tm_moe - 1)) // tm_moe
    pos, row_token, plan, n_used = _dispatch_plan(
        ids[:, :TOP_K].reshape(-1), n_experts=E, tm=tm_moe, n_tiles=n_tiles)
    h = moe_up(x, w1, w3, plan, row_token, n_used, layer=layer, tm=tm_moe)
    y = moe_down(h, w2, plan, n_used, layer=layer, tm=tm_moe)
    out = moe_combine_ln(y, pos, x, wts, g, b, tm=tm_comb, with_bf16_copy=with_bf16_copy)
    return out[0], (out[1] if with_bf16_copy else None)


def kernel(x, even_w_in, even_w_pool, even_pool_scale, even_b_f, even_w_out, odd_w_in, odd_conv_w, odd_conv_b, odd_w_a, odd_b_a, odd_w_x, odd_b_x, odd_lambda, odd_w_out, moe_w_group, moe_b_group, moe_w_expert, moe_b_expert, moe_w1, moe_w3, moe_w2, ln_g, ln_b):
    B, S, D = x.shape
    T = B * S
    xf = x.reshape(T, D)
    xb = None
    tiles = _tile_plan(T, S, D)

    for layer in range(DEPTH):
        i = layer // 2
        if layer % 2 == 0:
            pool_w = even_w_pool.shape[1] * even_w_pool.shape[2]
            heads = even_b_f.shape[1]
            fox_w = heads * HEAD_DIM
            w_in_t = jnp.swapaxes(even_w_in[i], 0, 1)
            lane0 = LANES - heads
            w_f = w_in_t[w_in_t.shape[0] - LANES:, :]
            b_f = jnp.zeros((1, LANES), F32).at[0, lane0:].set(even_b_f[i])
            c, xb = forget_cumsum(xf, w_f, b_f, batch=B, tm=tiles["seq"])
            proj = matmul(xb, w_in_t, n_cols=pool_w + 3 * fox_w, tm=tiles["mm_m"],
                          tn=tiles["mm_n"], out_dtype=BF16, w_is_transposed=True)
            a_out = pool_mixer(proj, even_w_pool[i].astype(BF16), even_pool_scale[i],
                               batch=B, ts=tiles["mixer_seq"])
            qc = pool_w // HEAD_DIM
            b_out = forgetting_attention(proj, c, batch=B, heads=heads, head_lane0=lane0,
                                         q_col=qc, k_col=qc + heads, v_col=qc + 2 * heads,
                                         tq=tiles["attn_q"], tk=tiles["attn_k"])
            mixed = [a_out, b_out]
            w_out = even_w_out[i].astype(BF16)
        else:
            if xb is None:
                xb = xf.astype(BF16)
            proj = matmul(xb, odd_w_in[i], n_cols=odd_w_in.shape[2], tm=tiles["mm_m"],
                          tn=tiles["mm_n"], out_dtype=BF16)
            mixed = [rglru_mixer(proj, odd_conv_w[i], odd_conv_b[i], odd_w_a[i].astype(BF16),
                                 odd_b_a[i].reshape(-1), odd_w_x[i].astype(BF16),
                                 odd_b_x[i].reshape(-1), odd_lambda[i], batch=B,
                                 ts=tiles["mixer_seq"])]
            w_out = odd_w_out[i].astype(BF16)
        xf = matmul_residual_ln(mixed, w_out, xf, ln_g[layer, 0], ln_b[layer, 0],
                                tm=tiles["ln_m"])
        xf, xb = hierarchical_moe_ln(xf, moe_w_group[layer], moe_b_group[layer],
                                     moe_w_expert[layer], moe_b_expert[layer], moe_w1, moe_w3,
                                     moe_w2, ln_g[layer, 1], ln_b[layer, 1], layer=layer,
                                     with_bf16_copy=layer + 1 < DEPTH,
                                     tm_route=tiles["route"], tm_moe=tiles["moe"],
                                     tm_comb=tiles["comb"])
    return xf.reshape(B, S, D)


def _tile_plan(T, S, D):
    return {
        "mm_m": min(1024, T), "mm_n": min(1024, D),
        "seq": min(512, S), "mixer_seq": min(1024, S),
        "attn_q": min(1024, S), "attn_k": min(512, S),
        "ln_m": min(256, T),
        "route": min(512, T), "moe": 256, "comb": min(256, T),
    }
```

```python
import functools

import jax
import jax.numpy as jnp
from jax import lax
from jax.experimental import pallas as pl
from jax.experimental.pallas import tpu as pltpu

F32 = jnp.float32
BF16 = jnp.bfloat16

POOL_WINDOWS = (2, 4, 8, 16)
POOL_HALO = 16
HEAD_DIM = 128
LRU_BLOCKS = 16
CONV_WIDTH = 4
CONV_HALO = 8
LRU_C = 8.0
N_GROUPS = 4
EXPERTS_PER_GROUP = 8
TOP_K = 2
DEPTH = 2
DN_ALPHA = (2.0 * DEPTH) ** 0.25
LN_EPS = 1e-5

LANES = 128
SUBLANES = 8
VMEM_LIMIT_MB = 56
OUT_PROJ_VMEM_MB = 60


def _params(semantics, vmem_mb=VMEM_LIMIT_MB):
    return pltpu.CompilerParams(dimension_semantics=semantics,
                                vmem_limit_bytes=vmem_mb * 1024 * 1024)


_CONTRACT_LAST = (((1,), (1,)), ((), ()))


def _mm_kernel(a_ref, w_hbm, o_ref, w_stage, w_bf, sem, *, w_is_transposed, tn):
    j = pl.program_id(0)
    i = pl.program_id(1)

    def w_copy(block):
        cols = pl.ds(pl.multiple_of(block * tn, tn), tn)
        src = w_hbm.at[cols, :] if w_is_transposed else w_hbm.at[:, cols]
        return pltpu.make_async_copy(src, w_stage, sem)

    @pl.when(jnp.logical_and(j == 0, i == 0))
    def _():
        w_copy(0).start()

    @pl.when(i == 0)
    def _():
        w_copy(j).wait()
        w_bf[...] = w_stage[...].astype(BF16)

        @pl.when(j + 1 < pl.num_programs(0))
        def _():
            w_copy(j + 1).start()

    if w_is_transposed:
        out = lax.dot_general(a_ref[...], w_bf[...], _CONTRACT_LAST, preferred_element_type=F32)
    else:
        out = jnp.dot(a_ref[...], w_bf[...], preferred_element_type=F32)
    o_ref[...] = out.astype(o_ref.dtype)


def matmul(a, w, *, n_cols, tm, tn, out_dtype, w_is_transposed=False):
    M, K = a.shape
    w_block = (tn, K) if w_is_transposed else (K, tn)
    return pl.pallas_call(
        functools.partial(_mm_kernel, w_is_transposed=w_is_transposed, tn=tn),
        grid=(n_cols // tn, M // tm),
        in_specs=[pl.BlockSpec((tm, K), lambda j, i: (i, 0)),
                  pl.BlockSpec(memory_space=pl.ANY)],
        out_specs=pl.BlockSpec((tm, tn), lambda j, i: (i, j)),
        out_shape=jax.ShapeDtypeStruct((M, n_cols), out_dtype),
        scratch_shapes=[pltpu.VMEM(w_block, F32), pltpu.VMEM(w_block, BF16),
                        pltpu.SemaphoreType.DMA(())],
        compiler_params=_params(("arbitrary", "arbitrary")),
        name="proj_matmul",
    )(a, w)


def _layer_norm_rows(z, g, b):
    mu = jnp.mean(z, axis=-1, keepdims=True)
    d = z - mu
    var = jnp.mean(d * d, axis=-1, keepdims=True)
    return d * lax.rsqrt(var + LN_EPS) * g + b


LN_CHUNK_ROWS = 64


def _mm_ln_kernel(*refs, n_a):
    a_refs = refs[:n_a]
    w_hbm, x_hbm, g_ref, b_ref, o_ref, w_buf, acc_ref, sem = refs[n_a:]
    i = pl.program_id(0)
    tm = o_ref.shape[0]

    @pl.when(i == 0)
    def _():
        w_copy = pltpu.make_async_copy(w_hbm, w_buf, sem.at[0])
        w_copy.start()
        w_copy.wait()

    x_copy = pltpu.make_async_copy(x_hbm.at[pl.ds(pl.multiple_of(i * tm, tm), tm), :], o_ref,
                                   sem.at[1])
    x_copy.start()
    k_each = a_refs[0].shape[1]
    for idx, a_ref in enumerate(a_refs):
        part = jnp.dot(a_ref[...], w_buf[idx * k_each:(idx + 1) * k_each, :],
                       preferred_element_type=F32)
        if idx == 0:
            acc_ref[...] = part
        else:
            acc_ref[...] += part
    x_copy.wait()

    def normalise(c, carry):
        rows = pl.ds(pl.multiple_of(c * LN_CHUNK_ROWS, LN_CHUNK_ROWS), LN_CHUNK_ROWS)
        o_ref[rows, :] = _layer_norm_rows(DN_ALPHA * o_ref[rows, :] + acc_ref[rows, :],
                                          g_ref[...], b_ref[...])
        return carry

    lax.fori_loop(0, tm // LN_CHUNK_ROWS, normalise, 0)


def matmul_residual_ln(a_list, w, x, g, b, *, tm):
    M, K_each = a_list[0].shape
    K, N = w.shape
    n_a = len(a_list)
    assert n_a * K_each == K and tm % LN_CHUNK_ROWS == 0
    return pl.pallas_call(
        functools.partial(_mm_ln_kernel, n_a=n_a),
        grid=(M // tm,),
        in_specs=[pl.BlockSpec((tm, K_each), lambda i: (i, 0)) for _ in range(n_a)] + [
            pl.BlockSpec(memory_space=pl.ANY),
            pl.BlockSpec(memory_space=pl.ANY),
            pl.BlockSpec((1, N), lambda i: (0, 0)),
            pl.BlockSpec((1, N), lambda i: (0, 0))],
        out_specs=pl.BlockSpec((tm, N), lambda i: (i, 0)),
        out_shape=jax.ShapeDtypeStruct((M, N), F32),
        scratch_shapes=[pltpu.VMEM((K, N), BF16), pltpu.VMEM((tm, N), F32),
                        pltpu.SemaphoreType.DMA((2,))],
        compiler_params=_params(("arbitrary",), vmem_mb=OUT_PROJ_VMEM_MB),
        name="out_proj_ln",
    )(*a_list, w, x, g.reshape(1, N), b.reshape(1, N))


def _cumsum_rows(v, n_rows):
    row = lax.broadcasted_iota(jnp.int32, v.shape, 0)
    shift = 1
    while shift < n_rows:
        v = v + jnp.where(row >= shift, pltpu.roll(v, shift, 0), 0.0)
        shift *= 2
    return v


def _forget_kernel(x_ref, w_ref, bf_ref, c_ref, xb_ref, carry_ref, *, tm):
    @pl.when(pl.program_id(1) == 0)
    def _():
        carry_ref[...] = jnp.zeros_like(carry_ref)

    xb = x_ref[...].astype(BF16)
    xb_ref[...] = xb
    z = lax.dot_general(xb, w_ref[...].astype(BF16), _CONTRACT_LAST,
                        preferred_element_type=F32) + bf_ref[...]
    log_f = jnp.minimum(z, 0.0) - jnp.log1p(jnp.exp(-jnp.abs(z)))
    c = _cumsum_rows(log_f, tm) + carry_ref[...]
    c_ref[...] = c
    carry_ref[...] = c[tm - 1:tm, :]


def forget_cumsum(x, w_f, b_f, *, batch, tm):
    T, D = x.shape
    steps = T // batch // tm
    return pl.pallas_call(
        functools.partial(_forget_kernel, tm=tm),
        grid=(batch, steps),
        in_specs=[pl.BlockSpec((tm, D), lambda b, s: (b * steps + s, 0)),
                  pl.BlockSpec((LANES, D), lambda b, s: (0, 0)),
                  pl.BlockSpec((1, LANES), lambda b, s: (0, 0))],
        out_specs=[pl.BlockSpec((tm, LANES), lambda b, s: (b * steps + s, 0)),
                   pl.BlockSpec((tm, D), lambda b, s: (b * steps + s, 0))],
        out_shape=[jax.ShapeDtypeStruct((T, LANES), F32), jax.ShapeDtypeStruct((T, D), BF16)],
        scratch_shapes=[pltpu.VMEM((1, LANES), F32)],
        compiler_params=_params(("parallel", "arbitrary")),
        name="forget_cumsum",
    )(x, w_f, b_f)


def _pool_kernel(u_ref, w_ref, sc_ref, o_ref, ext_ref, *, ts):
    g = pl.program_id(1)
    si = pl.program_id(2)

    @pl.when(si == 0)
    def _():
        ext_ref[0:POOL_HALO, :] = jnp.zeros((POOL_HALO, ext_ref.shape[1]), F32)

    @pl.when(si > 0)
    def _():
        ext_ref[0:POOL_HALO, :] = ext_ref[ts:ts + POOL_HALO, :]

    u = u_ref[...].astype(F32)
    ext_ref[POOL_HALO:, :] = u
    e1 = ext_ref[...]
    e2 = e1 + pltpu.roll(e1, 1, 0)
    e4 = e2 + pltpu.roll(e2, 2, 0)
    e8 = e4 + pltpu.roll(e4, 4, 0)
    e16 = e8 + pltpu.roll(e8, 8, 0)
    win = jnp.where(g == 0, e2, jnp.where(g == 1, e4, jnp.where(g == 2, e8, e16)))
    win = win[POOL_HALO:, :]
    width = jnp.left_shift(2, g)
    t = si * ts + lax.broadcasted_iota(jnp.int32, (ts, 1), 0)
    cnt = jnp.minimum(t + 1, width).astype(F32)
    pooled = win / cnt - u
    y = jnp.dot(pooled.astype(BF16), w_ref[...], preferred_element_type=F32)
    o_ref[...] = (y * sc_ref[...]).astype(o_ref.dtype)


def pool_mixer(proj, w_pool_b, pool_scale, *, batch, ts):
    T = proj.shape[0]
    G, C, _ = w_pool_b.shape
    assert POOL_WINDOWS == (2, 4, 8, 16) and G == len(POOL_WINDOWS)
    steps = T // batch // ts
    return pl.pallas_call(
        functools.partial(_pool_kernel, ts=ts),
        grid=(batch, G, steps),
        in_specs=[pl.BlockSpec((ts, C), lambda b, g, s: (b * steps + s, g)),
                  pl.BlockSpec((None, C, C), lambda b, g, s: (g, 0, 0)),
                  pl.BlockSpec((1, C), lambda b, g, s: (0, g))],
        out_specs=pl.BlockSpec((ts, C), lambda b, g, s: (b * steps + s, g)),
        out_shape=jax.ShapeDtypeStruct((T, G * C), BF16),
        scratch_shapes=[pltpu.VMEM((ts + POOL_HALO, C), F32)],
        compiler_params=_params(("parallel", "parallel", "arbitrary")),
        name="pool_mixer",
    )(proj, w_pool_b, pool_scale.reshape(1, G * C))


LOG2E = 1.4426950408889634
N_BIAS_PARTS = 3


def _attn_kernel(q_ref, k_ref, v_ref, c_ref, o_ref, kaug_ref, vaug_ref, qext_ref,
                 qaug_ref, sa_ref, sb_ref, m_ref, acc_ref, *, tq, tk, scale, head_lane0):
    h = pl.program_id(1)
    qi = pl.program_id(2)
    S = k_ref.shape[0]

    @pl.when(qi == 0)
    def _():
        lane = lax.broadcasted_iota(jnp.int32, (S, LANES), 1)
        c = jnp.sum(jnp.where(lane == head_lane0 + h, c_ref[...], 0.0), axis=1,
                    keepdims=True) * LOG2E
        hi = c.astype(BF16).astype(F32)
        mid = (c - hi).astype(BF16).astype(F32)
        lo = c - hi - mid
        ones = jnp.where(lane < N_BIAS_PARTS, 1.0, 0.0)
        k_ext = jnp.where(lane == 3, -hi, jnp.where(lane == 4, -mid, jnp.where(lane == 5, -lo, ones)))
        q_ext = jnp.where(lane == 0, hi, jnp.where(lane == 1, mid, jnp.where(
            lane == 2, lo, jnp.where(lane < 2 * N_BIAS_PARTS, 1.0, 0.0))))
        kaug_ref[:, :HEAD_DIM] = k_ref[...]
        kaug_ref[:, HEAD_DIM:] = k_ext.astype(BF16)
        vaug_ref[:, :HEAD_DIM] = v_ref[...]
        vaug_ref[:, HEAD_DIM:] = jnp.where(lane == 0, 1.0, 0.0).astype(BF16)
        qext_ref[...] = q_ext.astype(BF16)

    q0 = pl.multiple_of(qi * tq, tq)
    qaug_ref[:, :HEAD_DIM] = (q_ref[...].astype(F32) * (scale * LOG2E)).astype(BF16)
    qaug_ref[:, HEAD_DIM:] = qext_ref[pl.ds(q0, tq), :]
    m_ref[...] = jnp.full(m_ref.shape, -jnp.inf, F32)
    acc_ref[...] = jnp.zeros(acc_ref.shape, F32)

    def scores(block, lo=0):
        k0 = pl.multiple_of(block * tk, tk)
        return lax.dot_general(qaug_ref[lo:, :], kaug_ref[pl.ds(k0, tk), :],
                               (((1,), (1,)), ((), ())), preferred_element_type=F32)

    def softmax_pv(s, block, lo=0):
        k0 = pl.multiple_of(block * tk, tk)
        m = m_ref[lo:, :]
        m_new = jnp.maximum(m, jnp.max(s, axis=1, keepdims=True))
        alpha = jnp.exp2(m - m_new)
        p = jnp.exp2((s - m_new).astype(BF16))
        acc_ref[lo:, :] = alpha * acc_ref[lo:, :] + jnp.dot(
            p, vaug_ref[pl.ds(k0, tk), :], preferred_element_type=F32)
        m_ref[lo:, :] = m_new

    n_full = qi * (tq // tk)
    sa_ref[...] = scores(0)

    def body(i, carry):
        j = 2 * i
        sb_ref[...] = scores(j + 1)
        softmax_pv(sa_ref[...], j)
        sa_ref[...] = scores(j + 2)
        softmax_pv(sb_ref[...], j + 1)
        return carry

    lax.fori_loop(0, n_full // 2, body, 0)

    diag_scores = [sa_ref[...]] + [scores(n_full + u, u * tk) for u in range(1, tq // tk)]
    for u, s in enumerate(diag_scores):
        lo = u * tk
        row = lax.broadcasted_iota(jnp.int32, s.shape, 0)
        col = lax.broadcasted_iota(jnp.int32, s.shape, 1)
        softmax_pv(jnp.where(col <= row, s, -jnp.inf), n_full + u, lo)
        acc = acc_ref[lo:lo + tk, :]
        o_ref[lo:lo + tk, :] = (acc[:, :HEAD_DIM] / acc[:, HEAD_DIM:HEAD_DIM + 1]
                                ).astype(o_ref.dtype)


def forgetting_attention(proj, c, *, batch, heads, head_lane0, q_col, k_col, v_col, tq, tk):
    T = proj.shape[0]
    S = T // batch
    nq = S // tq
    assert (tq // tk) % 2 == 0, "full key blocks are consumed two per loop trip"
    return pl.pallas_call(
        functools.partial(_attn_kernel, tq=tq, tk=tk, scale=HEAD_DIM ** -0.5,
                          head_lane0=head_lane0),
        grid=(batch, heads, nq),
        in_specs=[pl.BlockSpec((tq, HEAD_DIM), lambda b, h, i: (b * nq + i, q_col + h)),
                  pl.BlockSpec((S, HEAD_DIM), lambda b, h, i: (b, k_col + h)),
                  pl.BlockSpec((S, HEAD_DIM), lambda b, h, i: (b, v_col + h)),
                  pl.BlockSpec((S, LANES), lambda b, h, i: (b, 0))],
        out_specs=pl.BlockSpec((tq, HEAD_DIM), lambda b, h, i: (b * nq + i, h)),
        out_shape=jax.ShapeDtypeStruct((T, heads * HEAD_DIM), BF16),
        scratch_shapes=[pltpu.VMEM((S, 2 * HEAD_DIM), BF16), pltpu.VMEM((S, 2 * HEAD_DIM), BF16),
                        pltpu.VMEM((S, LANES), BF16), pltpu.VMEM((tq, 2 * HEAD_DIM), BF16),
                        pltpu.VMEM((tq, tk), F32), pltpu.VMEM((tq, tk), F32),
                        pltpu.VMEM((tq, 1), F32), pltpu.VMEM((tq, 2 * HEAD_DIM), F32)],
        compiler_params=_params(("parallel", "parallel", "arbitrary")),
        name="forgetting_attention",
    )(proj, proj, proj, c)


def _sigmoid(z):
    return 0.5 * jnp.tanh(0.5 * z) + 0.5


def _rglru_kernel(xb_ref, gate_ref, cw_ref, cb_ref, wa_ref, ba_ref, wx_ref, bx_ref, lam_ref,
                  y_ref, ext_ref, a_ref, b_ref, h_ref, carry_ref, *, ts):
    si = pl.program_id(2)
    C = xb_ref.shape[1]

    @pl.when(si == 0)
    def _():
        ext_ref[0:CONV_HALO, :] = jnp.zeros((CONV_HALO, C), F32)
        carry_ref[...] = jnp.zeros_like(carry_ref)

    @pl.when(si > 0)
    def _():
        ext_ref[0:CONV_HALO, :] = ext_ref[ts:ts + CONV_HALO, :]

    xb = xb_ref[...].astype(F32)
    ext_ref[CONV_HALO:, :] = xb
    cw = cw_ref[...]
    xc = cb_ref[...] + xb * cw[CONV_WIDTH - 1:CONV_WIDTH, :]
    for lag in range(1, CONV_WIDTH):
        tap = CONV_WIDTH - 1 - lag
        xc = xc + ext_ref[CONV_HALO - lag:CONV_HALO - lag + ts, :] * cw[tap:tap + 1, :]

    xcb = xc.astype(BF16)
    r = _sigmoid(jnp.dot(xcb, wa_ref[...], preferred_element_type=F32) + ba_ref[...])
    ig = _sigmoid(jnp.dot(xcb, wx_ref[...], preferred_element_type=F32) + bx_ref[...])
    lam = lam_ref[...]
    log_a_base = -(jnp.maximum(-lam, 0.0) + jnp.log1p(jnp.exp(-jnp.abs(lam))))
    log_a = LRU_C * r * log_a_base
    a = jnp.exp(log_a)
    mult = jnp.exp2(0.5 * jnp.log2(-jnp.tanh(log_a) * (1.0 + a * a)))
    b = mult * ig * xc

    groups = ts // SUBLANES
    a3 = a.reshape(groups, SUBLANES, C)
    b3 = b.reshape(groups, SUBLANES, C)
    sub = lax.broadcasted_iota(jnp.int32, a3.shape, 1)
    shift = 1
    while shift < SUBLANES:
        keep = sub >= shift
        b3 = jnp.where(keep, a3 * pltpu.roll(b3, shift, 1) + b3, b3)
        a3 = jnp.where(keep, a3 * pltpu.roll(a3, shift, 1), a3)
        shift *= 2
    a_ref[...] = a3.reshape(ts, C)
    b_ref[...] = b3.reshape(ts, C)

    def body(gi, carry):
        r0 = pl.multiple_of(gi * SUBLANES, SUBLANES)
        hg = a_ref[pl.ds(r0, SUBLANES), :] * carry + b_ref[pl.ds(r0, SUBLANES), :]
        h_ref[pl.ds(r0, SUBLANES), :] = hg
        return hg[SUBLANES - 1:SUBLANES, :]

    carry_ref[...] = lax.fori_loop(0, groups, body, carry_ref[...], unroll=8)

    gate = gate_ref[...].astype(F32)
    gelu = 0.5 * gate * (1.0 + jnp.tanh(0.7978845608028654 * (gate + 0.044715 * gate * gate * gate)))
    y_ref[...] = (h_ref[...] * gelu).astype(y_ref.dtype)


def rglru_mixer(proj, conv_w, conv_b, w_a_b, b_a, w_x_b, b_x, lam, *, batch, ts):
    T = proj.shape[0]
    NB, C, _ = w_a_b.shape
    W = NB * C
    steps = T // batch // ts
    vec = lambda: pl.BlockSpec((1, C), lambda b, c, s: (0, c))
    blk = lambda: pl.BlockSpec((None, C, C), lambda b, c, s: (c, 0, 0))
    return pl.pallas_call(
        functools.partial(_rglru_kernel, ts=ts),
        grid=(batch, NB, steps),
        in_specs=[pl.BlockSpec((ts, C), lambda b, c, s: (b * steps + s, NB + c)),
                  pl.BlockSpec((ts, C), lambda b, c, s: (b * steps + s, c)),
                  pl.BlockSpec((CONV_WIDTH, C), lambda b, c, s: (0, c)),
                  vec(), blk(), vec(), blk(), vec(), vec()],
        out_specs=pl.BlockSpec((ts, C), lambda b, c, s: (b * steps + s, c)),
        out_shape=jax.ShapeDtypeStruct((T, W), BF16),
        scratch_shapes=[pltpu.VMEM((ts + CONV_HALO, C), F32),
                        pltpu.VMEM((ts, C), F32), pltpu.VMEM((ts, C), F32),
                        pltpu.VMEM((ts, C), F32), pltpu.VMEM((1, C), F32)],
        compiler_params=_params(("parallel", "parallel", "arbitrary")),
        name="rglru_mixer",
    )(proj, proj, conv_w, conv_b.reshape(1, W), w_a_b, b_a.reshape(1, W), w_x_b,
      b_x.reshape(1, W), lam.reshape(1, W))


def _router_kernel(x_ref, w_ref, b_ref, id_ref, wt_ref):
    logits = jnp.dot(x_ref[...].astype(BF16), w_ref[...].astype(BF16),
                     preferred_element_type=F32) + b_ref[...]
    lane = lax.broadcasted_iota(jnp.int32, logits.shape, 1).astype(F32)
    neg = -jnp.inf
    big = float(LANES)

    is_group = lane < N_GROUPS
    gl = jnp.where(is_group, logits, neg)
    gmax = jnp.max(gl, axis=1, keepdims=True)
    g_idx = jnp.min(jnp.where(gl == gmax, lane, big), axis=1, keepdims=True)
    g_w = 1.0 / jnp.sum(jnp.where(is_group, jnp.exp(gl - gmax), 0.0), axis=1, keepdims=True)

    lo = N_GROUPS + g_idx * EXPERTS_PER_GROUP
    el = jnp.where(lane >= lo, jnp.where(lane < lo + EXPERTS_PER_GROUP, logits, neg), neg)
    v1 = jnp.max(el, axis=1, keepdims=True)
    i1 = jnp.min(jnp.where(el == v1, lane, big), axis=1, keepdims=True)
    el2 = jnp.where(lane == i1, neg, el)
    v2 = jnp.max(el2, axis=1, keepdims=True)
    i2 = jnp.min(jnp.where(el2 == v2, lane, big), axis=1, keepdims=True)
    t = jnp.exp(v2 - v1)
    w1 = g_w / (1.0 + t)
    w2 = w1 * t
    ids = jnp.where(lane == 0, i1 - N_GROUPS, jnp.where(lane == 1, i2 - N_GROUPS, 0.0))
    id_ref[...] = ids.astype(jnp.int32)
    wt_ref[...] = jnp.where(lane == 0, w1, jnp.where(lane == 1, w2, 0.0))


def router(x, w_route, b_route, *, tm):
    T, D = x.shape
    return pl.pallas_call(
        _router_kernel,
        grid=(T // tm,),
        in_specs=[pl.BlockSpec((tm, D), lambda i: (i, 0)),
                  pl.BlockSpec((D, LANES), lambda i: (0, 0)),
                  pl.BlockSpec((1, LANES), lambda i: (0, 0))],
        out_specs=[pl.BlockSpec((tm, LANES), lambda i: (i, 0)),
                   pl.BlockSpec((tm, LANES), lambda i: (i, 0))],
        out_shape=[jax.ShapeDtypeStruct((T, LANES), jnp.int32),
                   jax.ShapeDtypeStruct((T, LANES), F32)],
        compiler_params=_params(("parallel",)),
        name="router",
    )(x, w_route, b_route)


GATHER_UNROLL = SUBLANES


def _row_gather_start(src_hbm, dst_ref, sem, row_of, n_rows):
    def body(g, carry):
        for u in range(GATHER_UNROLL):
            r = g * GATHER_UNROLL + u
            pltpu.make_async_copy(src_hbm.at[pl.ds(row_of(r), 1), :],
                                  dst_ref.at[pl.ds(r, 1), :], sem).start()
        return carry
    lax.fori_loop(0, n_rows // GATHER_UNROLL, body, 0)


def _row_gather_wait(src_hbm, dst_ref, sem, n_rows):
    if not isinstance(n_rows, int):
        n_rows = pl.multiple_of(n_rows, GATHER_UNROLL)
    pltpu.make_async_copy(src_hbm.at[pl.ds(0, n_rows), :], dst_ref.at[pl.ds(0, n_rows), :],
                          sem).wait()


PLAN_EXPERT, PLAN_FIRST, PLAN_SLOT, PLAN_NEXT, PLAN_ROWS = range(5)


def _expert_weight_copies(w_hbm_list, buf_list, sem, layer, expert, slot):
    return [pltpu.make_async_copy(w.at[layer, expert], buf.at[slot], sem.at[slot])
            for w, buf in zip(w_hbm_list, buf_list)]


def _expert_weight_pipeline(plan_ref, n_used, w_hbm_list, buf_list, sem, layer):
    j = pl.program_id(0)
    slot = plan_ref[PLAN_SLOT, j]
    first = jnp.logical_and(j < n_used, plan_ref[PLAN_FIRST, j] == 1)

    @pl.when(j == 0)
    def _():
        for cp in _expert_weight_copies(w_hbm_list, buf_list, sem, layer,
                                        plan_ref[PLAN_EXPERT, 0], 0):
            cp.start()

    @pl.when(jnp.logical_and(first, plan_ref[PLAN_NEXT, j] >= 0))
    def _():
        for cp in _expert_weight_copies(w_hbm_list, buf_list, sem, layer,
                                        plan_ref[PLAN_NEXT, j], 1 - slot):
            cp.start()

    @pl.when(first)
    def _():
        for cp in _expert_weight_copies(w_hbm_list, buf_list, sem, layer,
                                        plan_ref[PLAN_EXPERT, j], slot):
            cp.wait()

    return slot


def _moe_up_kernel(plan_ref, rt_ref, nu_ref, x_hbm, w1_hbm, w3_hbm, h_ref,
                   xg_0, xg_1, xg_2, w1_buf, w3_buf, xsem, wsem, *, tm, layer):
    j = pl.program_id(0)
    n_used = nu_ref[0]
    xg = (xg_0, xg_1, xg_2)
    ring = len(xg)

    def gather(tile, slot):
        _row_gather_start(x_hbm, xg[slot], xsem.at[slot],
                          lambda r: rt_ref[tile * tm + r], plan_ref[PLAN_ROWS, tile])

    @pl.when(j == 0)
    def _():
        for buf in xg:
            buf[...] = jnp.zeros_like(buf)
        gather(0, 0)

        @pl.when(n_used > 1)
        def _():
            gather(1, 1)

    wslot = _expert_weight_pipeline(plan_ref, n_used, (w1_hbm, w3_hbm), (w1_buf, w3_buf),
                                    wsem, layer)

    def swiglu(slot):
        x = xg[slot][...].astype(BF16)
        a = jnp.dot(x, w1_buf[wslot].astype(BF16), preferred_element_type=F32)
        b = jnp.dot(x, w3_buf[wslot].astype(BF16), preferred_element_type=F32)
        h_ref[...] = (a * jax.nn.sigmoid(a) * b).astype(h_ref.dtype)

    for slot in range(ring):
        mine = jnp.logical_and(j < n_used, j % ring == slot)

        @pl.when(jnp.logical_and(mine, j + 2 < n_used))
        def _(slot=slot):
            gather(j + 2, (slot + 2) % ring)

        @pl.when(mine)
        def _(slot=slot):
            _row_gather_wait(x_hbm, xg[slot], xsem.at[slot], plan_ref[PLAN_ROWS, j])
            swiglu(slot)

    @pl.when(j >= n_used)
    def _():
        h_ref[...] = jnp.zeros_like(h_ref)


def moe_up(x, w1, w3, plan, row_token, n_used, *, layer, tm):
    D = x.shape[1]
    FF = w1.shape[-1]
    n_tiles = plan.shape[1]
    any_spec = lambda: pl.BlockSpec(memory_space=pl.ANY)
    return pl.pallas_call(
        functools.partial(_moe_up_kernel, tm=tm, layer=layer),
        grid_spec=pltpu.PrefetchScalarGridSpec(
            num_scalar_prefetch=3,
            grid=(n_tiles,),
            in_specs=[any_spec(), any_spec(), any_spec()],
            out_specs=pl.BlockSpec((tm, FF), lambda j, plan, rt, nu: (j, 0)),
            scratch_shapes=[pltpu.VMEM((tm, D), F32), pltpu.VMEM((tm, D), F32),
                            pltpu.VMEM((tm, D), F32),
                            pltpu.VMEM((2, D, FF), F32), pltpu.VMEM((2, D, FF), F32),
                            pltpu.SemaphoreType.DMA((3,)), pltpu.SemaphoreType.DMA((2,))]),
        out_shape=jax.ShapeDtypeStruct((n_tiles * tm, FF), BF16),
        compiler_params=_params(("arbitrary",)),
        name="moe_up",
    )(plan, row_token, n_used, x, w1, w3)


def _moe_down_kernel(plan_ref, nu_ref, h_ref, w2_hbm, y_ref, w2_buf, wsem, *, layer):
    j = pl.program_id(0)
    n_used = nu_ref[0]
    wslot = _expert_weight_pipeline(plan_ref, n_used, (w2_hbm,), (w2_buf,), wsem, layer)

    @pl.when(j < n_used)
    def _():
        y_ref[...] = jnp.dot(h_ref[...], w2_buf[wslot].astype(BF16),
                             preferred_element_type=F32)

    @pl.when(j >= n_used)
    def _():
        y_ref[...] = jnp.zeros_like(y_ref)


def moe_down(h, w2, plan, n_used, *, layer, tm):
    FF, D = w2.shape[-2:]
    n_tiles = plan.shape[1]
    tile = lambda j, plan, nu: (j, 0)
    return pl.pallas_call(
        functools.partial(_moe_down_kernel, layer=layer),
        grid_spec=pltpu.PrefetchScalarGridSpec(
            num_scalar_prefetch=2,
            grid=(n_tiles,),
            in_specs=[pl.BlockSpec((tm, FF), tile), pl.BlockSpec(memory_space=pl.ANY)],
            out_specs=pl.BlockSpec((tm, D), tile),
            scratch_shapes=[pltpu.VMEM((2, FF, D), F32), pltpu.SemaphoreType.DMA((2,))]),
        out_shape=jax.ShapeDtypeStruct((n_tiles * tm, D), F32),
        compiler_params=_params(("arbitrary",)),
        name="moe_down",
    )(plan, n_used, h, w2)


def _combine_ln_kernel(pos_ref, y_hbm, x_ref, wt_ref, g_ref, b_ref, o_ref, *rest, tm):
    ob_ref = rest[0] if len(rest) == 3 else None
    yg_ref, sem = rest[-2:]
    i = pl.program_id(0)

    def gather(tile, slot):
        for k in range(TOP_K):
            _row_gather_start(y_hbm, yg_ref.at[slot, k], sem.at[slot],
                              lambda r: pos_ref[TOP_K * (tile * tm + r) + k], tm)

    @pl.when(i == 0)
    def _():
        gather(0, 0)

    @pl.when(i + 1 < pl.num_programs(0))
    def _():
        gather(i + 1, (i + 1) % 2)

    slot = i % 2
    for k in range(TOP_K):
        _row_gather_wait(y_hbm, yg_ref.at[slot, k], sem.at[slot], tm)
    wt = wt_ref[...]
    z = DN_ALPHA * x_ref[...]
    for k in range(TOP_K):
        z = z + wt[:, k:k + 1] * yg_ref[slot, k]
    y = _layer_norm_rows(z, g_ref[...], b_ref[...])
    o_ref[...] = y
    if ob_ref is not None:
        ob_ref[...] = y.astype(BF16)


def moe_combine_ln(y, pos, x, wts, g, b, *, tm, with_bf16_copy):
    T, D = x.shape
    row = lambda i, pos: (i, 0)
    fixed = lambda i, pos: (0, 0)
    n_out = 2 if with_bf16_copy else 1
    return pl.pallas_call(
        functools.partial(_combine_ln_kernel, tm=tm),
        grid_spec=pltpu.PrefetchScalarGridSpec(
            num_scalar_prefetch=1,
            grid=(T // tm,),
            in_specs=[pl.BlockSpec(memory_space=pl.ANY),
                      pl.BlockSpec((tm, D), row),
                      pl.BlockSpec((tm, LANES), row),
                      pl.BlockSpec((1, D), fixed),
                      pl.BlockSpec((1, D), fixed)],
            out_specs=[pl.BlockSpec((tm, D), row)] * n_out,
            scratch_shapes=[pltpu.VMEM((2, TOP_K, tm, D), F32),
                            pltpu.SemaphoreType.DMA((2,))]),
        out_shape=[jax.ShapeDtypeStruct((T, D), F32), jax.ShapeDtypeStruct((T, D), BF16)][:n_out],
        compiler_params=_params(("arbitrary",)),
        name="moe_combine_ln",
    )(pos, y, x, wts, g.reshape(1, D), b.reshape(1, D))


def _round_up(v, multiple):
    return (v + multiple - 1) // multiple * multiple


def _dispatch_plan(ids, *, n_experts, tm, n_tiles):
    n_assign = ids.shape[0]
    experts = jnp.arange(n_experts, dtype=jnp.int32)
    onehot = (ids[:, None] == experts[None, :]).astype(jnp.int32)
    csum = jnp.cumsum(onehot, axis=0)
    rank = jnp.sum((csum - onehot) * onehot, axis=1)
    counts = csum[-1]
    tiles_per = (counts + tm - 1) // tm
    tile_end = jnp.cumsum(tiles_per)
    tile_start = tile_end - tiles_per
    n_used = tile_end[-1]
    pos = tile_start[ids] * tm + rank
    row_token = jnp.zeros((n_tiles * tm,), jnp.int32).at[pos].set(
        jnp.arange(n_assign, dtype=jnp.int32) // TOP_K)

    tile_ids = jnp.arange(n_tiles, dtype=jnp.int32)
    in_use = tile_ids < n_used
    tile_expert = jnp.sum((jnp.minimum(tile_ids, n_used - 1)[:, None] >= tile_end[None, :])
                          .astype(jnp.int32), axis=1)
    used = counts > 0
    ordinal = jnp.cumsum(used.astype(jnp.int32)) - 1
    later = lax.cummin(jnp.where(used, experts, n_experts), axis=0, reverse=True)
    next_used = jnp.concatenate([later[1:], jnp.full((1,), n_experts, jnp.int32)])
    next_used = jnp.where(next_used < n_experts, next_used, -1)
    plan = jnp.stack([
        tile_expert,
        jnp.logical_and(in_use, tile_ids == tile_start[tile_expert]).astype(jnp.int32),
        ordinal[tile_expert] % 2,
        next_used[tile_expert],
        jnp.where(in_use, _round_up(jnp.clip(
            counts[tile_expert] - (tile_ids - tile_start[tile_expert]) * tm, 0, tm),
            GATHER_UNROLL), 0),
    ]).astype(jnp.int32)
    return pos.astype(jnp.int32), row_token, plan, n_used.reshape(1).astype(jnp.int32)


def hierarchical_moe_ln(x, w_group, b_group, w_expert, b_expert, w1, w3, w2, g, b, *,
                        layer, with_bf16_copy, tm_route, tm_moe, tm_comb):
    T, D = x.shape
    E = w1.shape[1]
    n_route = N_GROUPS + E
    w_route = jnp.zeros((D, LANES), F32).at[:, :N_GROUPS].set(w_group).at[:, N_GROUPS:n_route].set(w_expert)
    b_route = jnp.zeros((1, LANES), F32).at[0, :N_GROUPS].set(b_group).at[0, N_GROUPS:n_route].set(b_expert)
    ids, wts = router(x, w_route, b_route, tm=tm_route)
    n_tiles = (T * TOP_K + E * (tm_moe - 1)) // tm_moe
    pos, row_token, plan, n_used = _dispatch_plan(
        ids[:, :TOP_K].reshape(-1), n_experts=E, tm=tm_moe, n_tiles=n_tiles)
    h = moe_up(x, w1, w3, plan, row_token, n_used, layer=layer, tm=tm_moe)
    y = moe_down(h, w2, plan, n_used, layer=layer, tm=tm_moe)
    out = moe_combine_ln(y, pos, x, wts, g, b, tm=tm_comb, with_bf16_copy=with_bf16_copy)
    return out[0], (out[1] if with_bf16_copy else None)


def kernel(x, even_w_in, even_w_pool, even_pool_scale, even_b_f, even_w_out, odd_w_in, odd_conv_w, odd_conv_b, odd_w_a, odd_b_a, odd_w_x, odd_b_x, odd_lambda, odd_w_out, moe_w_group, moe_b_group, moe_w_expert, moe_b_expert, moe_w1, moe_w3, moe_w2, ln_g, ln_b):
    B, S, D = x.shape
    T = B * S
    xf = x.reshape(T, D)
    xb = None
    tiles = _tile_plan(T, S, D)

    for layer in range(DEPTH):
        i = layer // 2
        if layer % 2 == 0:
            pool_w = even_w_pool.shape[1] * even_w_pool.shape[2]
            heads = even_b_f.shape[1]
            fox_w = heads * HEAD_DIM
            w_in_t = jnp.swapaxes(even_w_in[i], 0, 1)
            lane0 = LANES - heads
            w_f = w_in_t[w_in_t.shape[0] - LANES:, :]
            b_f = jnp.zeros((1, LANES), F32).at[0, lane0:].set(even_b_f[i])
            c, xb = forget_cumsum(xf, w_f, b_f, batch=B, tm=tiles["seq"])
            proj = matmul(xb, w_in_t, n_cols=pool_w + 3 * fox_w, tm=tiles["mm_m"],
                          tn=tiles["mm_n"], out_dtype=BF16, w_is_transposed=True)
            a_out = pool_mixer(proj, even_w_pool[i].astype(BF16), even_pool_scale[i],
                               batch=B, ts=tiles["mixer_seq"])
            qc = pool_w // HEAD_DIM
            b_out = forgetting_attention(proj, c, batch=B, heads=heads, head_lane0=lane0,
                                         q_col=qc, k_col=qc + heads, v_col=qc + 2 * heads,
                                         tq=tiles["attn_q"], tk=tiles["attn_k"])
            mixed = [a_out, b_out]
            w_out = even_w_out[i].astype(BF16)
        else:
            if xb is None:
                xb = xf.astype(BF16)
            proj = matmul(xb, odd_w_in[i], n_cols=odd_w_in.shape[2], tm=tiles["mm_m"],
                          tn=tiles["mm_n"], out_dtype=BF16)
            mixed = [rglru_mixer(proj, odd_conv_w[i], odd_conv_b[i], odd_w_a[i].astype(BF16),
                                 odd_b_a[i].reshape(-1), odd_w_x[i].astype(BF16),
                                 odd_b_x[i].reshape(-1), odd_lambda[i], batch=B,
                                 ts=tiles["mixer_seq"])]
            w_out = odd_w_out[i].astype(BF16)
        xf = matmul_residual_ln(mixed, w_out, xf, ln_g[layer, 0], ln_b[layer, 0],
                                tm=tiles["ln_m"])
        xf, xb = hierarchical_moe_ln(xf, moe_w_group[layer], moe_b_group[layer],
                                     moe_w_expert[layer], moe_b_expert[layer], moe_w1, moe_w3,
                                     moe_w2, ln_g[layer, 1], ln_b[layer, 1], layer=layer,
                                     with_bf16_copy=layer + 1 < DEPTH,
                                     tm_route=tiles["route"], tm_moe=tiles["moe"],
                                     tm_comb=tiles["comb"])
    return xf.reshape(B, S, D)


def _tile_plan(T, S, D):
    return {
        "mm_m": min(1024, T), "mm_n": min(1024, D),
        "seq": min(512, S), "mixer_seq": min(1024, S),
        "attn_q": min(1024, S), "attn_k": min(512, S),
        "ln_m": min(256, T),
        "route": min(512, T), "moe": 256, "comb": min(256, T),
    }
```
